```python
import math
import jax
import jax.numpy as jnp
from jax import lax
import numpy as np

D_MODEL = 2048
BATCH = 8
SEQ = 2048
DEPTH = 1
DEC_BATCH = 32
DEC_SEQ = 1
PAST_LEN = 8192
PAGE_SIZE = 128

HEAD_DIM = 128
HEADS_PER_GROUP = 4
DIL_GROUPS = ((128, 1), (512, 4), (2048, 16))
N_DIL = len(DIL_GROUPS)
ATT_WIDTH = N_DIL * HEADS_PER_GROUP * HEAD_DIM
ATT_OUT = HEADS_PER_GROUP * HEAD_DIM
ATT_BLOCK = 128
ROT_DIM = HEAD_DIM // 4
ROPE_THETA = 500000.0
LRU_WIDTH = 1536
LRU_BLOCKS = 12
LRU_BLOCK = LRU_WIDTH // LRU_BLOCKS
CONV_W = 4
LRU_C = 8.0
MEM_TOKENS = 256
MEM_HEADS = 4
MEM_HEAD_DIM = 256
MEM_WIDTH = MEM_HEADS * MEM_HEAD_DIM
N_BRANCH = 3
N_EXPERTS = 32
TOP_K = 4
D_FF = D_MODEL
SWIGLU_LIMIT = 7.0
SWIGLU_ALPHA = 1.702
MOE_BLOCK = 128
LN_EPS = 1e-5
DN_ALPHA = (2 * DEPTH) ** 0.25
DN_BETA = (8 * DEPTH) ** -0.25
IN_SIZES = (ATT_WIDTH, ATT_WIDTH, ATT_WIDTH, LRU_WIDTH, LRU_WIDTH, MEM_WIDTH, N_BRANCH * D_MODEL)
IN_SPLITS = tuple(sum(IN_SIZES[:i + 1]) for i in range(len(IN_SIZES) - 1))
IN_WIDTH = sum(IN_SIZES)

kernel_name = 'hybrid_dilated_lru_memory_moe_step'


def layer_norm(x, g, b):
    xf = x.astype(jnp.float32)
    mu = jnp.mean(xf, axis=-1, keepdims=True)
    var = jnp.mean(jnp.square(xf - mu), axis=-1, keepdims=True)
    y = (xf - mu) * lax.rsqrt(var + LN_EPS) * g.astype(jnp.float32) + b.astype(jnp.float32)
    return y.astype(x.dtype)


def rope_partial(x, pos):
    half = ROT_DIM // 2
    inv_freq = ROPE_THETA ** (-jnp.arange(half, dtype=jnp.float32) / half)
    ang = pos.astype(jnp.float32)[:, None] * inv_freq[None, :]
    cos = jnp.cos(ang)[:, None, :]
    sin = jnp.sin(ang)[:, None, :]
    xr = x[..., :ROT_DIM].astype(jnp.float32)
    x1, x2 = xr[..., :half], xr[..., half:]
    rot = jnp.concatenate([x1 * cos - x2 * sin, x2 * cos + x1 * sin], axis=-1)
    return jnp.concatenate([rot.astype(x.dtype), x[..., ROT_DIM:]], axis=-1)


def masked_softmax_lse(s, mask):
    s = jnp.where(mask, s.astype(jnp.float32), -jnp.inf)
    m = jnp.max(s, axis=-1, keepdims=True)
    p = jnp.exp(s - m)
    den = jnp.sum(p, axis=-1, keepdims=True)
    return p / den, (m + jnp.log(den))[..., 0]


def dilated_group_prompt(q, k, v, window, dil):
    n, s, h, d = q.shape
    steps = window // dil
    sub = s // dil
    nb = -(-sub // ATT_BLOCK)
    sub_p = nb * ATT_BLOCK

    def to_sub(t):
        t = t.reshape(n, sub, dil, h, d).transpose(0, 2, 1, 3, 4)
        return jnp.pad(t, ((0, 0), (0, 0), (0, sub_p - sub), (0, 0), (0, 0)))

    def band(t):
        t = jnp.pad(t, ((0, 0), (0, 0), (ATT_BLOCK, 0), (0, 0), (0, 0)))
        t = t.reshape(n, dil, nb + 1, ATT_BLOCK, h, d)
        return jnp.concatenate([t[:, :, :-1], t[:, :, 1:]], axis=3)

    qb = to_sub(q).reshape(n, dil, nb, ATT_BLOCK, h, d)
    kb = band(to_sub(k))
    vb = band(to_sub(v))
    qi = jnp.arange(ATT_BLOCK)[:, None]
    kj = jnp.arange(2 * ATT_BLOCK)[None, :] - ATT_BLOCK
    dist = qi - kj
    blk0 = (jnp.arange(nb) * ATT_BLOCK)[:, None, None]
    mask = (dist >= 0) & (dist <= steps) & (blk0 + kj >= 0)
    sc = jnp.einsum('ngbqhd,ngbkhd->ngbhqk', qb, kb) * (HEAD_DIM ** -0.5)
    p, lse = masked_softmax_lse(sc, mask[:, None])
    o = jnp.einsum('ngbhqk,ngbkhd->ngbqhd', p.astype(v.dtype), vb)
    o = o.reshape(n, dil, sub_p, h, d)[:, :, :sub].transpose(0, 2, 1, 3, 4).reshape(n, s, h, d)
    lse = jnp.moveaxis(lse, -1, -2).reshape(n, dil, sub_p, h)[:, :, :sub]
    lse = lse.transpose(0, 2, 1, 3).reshape(n, s, h)
    return o, lse


def dilated_group_sample(q, k_all, v_all, pos, first_pos, window, dil):
    steps = window // dil
    kpos = pos[:, None] - dil * jnp.arange(steps + 1)[None, :]
    valid = kpos >= 0
    row = jnp.clip(kpos - first_pos, 0, k_all.shape[1] - 1)
    kg = jnp.take(k_all, row, axis=1)
    vg = jnp.take(v_all, row, axis=1)
    sc = jnp.einsum('nthd,ntjhd->nthj', q, kg) * (HEAD_DIM ** -0.5)
    p, lse = masked_softmax_lse(sc, valid[None, :, None, :])
    o = jnp.einsum('nthj,ntjhd->nthd', p.astype(v_all.dtype), vg)
    return o, lse


def causal_conv(u, conv_state, w, b):
    t = u.shape[1]
    ext = jnp.concatenate([conv_state.astype(u.dtype), u], axis=1)
    y = b + sum(ext[:, i:i + t] * w[i] for i in range(CONV_W))
    return y, ext[:, t:]


def rg_lru(xc, h0, w_a, b_a, w_x, b_x, lam):
    n, t, c = xc.shape
    xb = xc.reshape(n, t, LRU_BLOCKS, LRU_BLOCK)
    r = jax.nn.sigmoid(jnp.einsum('ntbi,bij->ntbj', xb, w_a).reshape(n, t, c) + b_a)
    i = jax.nn.sigmoid(jnp.einsum('ntbi,bij->ntbj', xb, w_x).reshape(n, t, c) + b_x)
    log_a = -LRU_C * r.astype(jnp.float32) * jax.nn.softplus(-lam.astype(jnp.float32))
    a = jnp.exp(log_a)
    gated_x = jnp.sqrt(-jnp.expm1(2.0 * log_a)) * (i * xc).astype(jnp.float32)
    gated_x = gated_x.at[:, 0].add(a[:, 0] * h0.astype(jnp.float32))

    def combine(left, right):
        a_l, b_l = left
        a_r, b_r = right
        return a_l * a_r, a_r * b_l + b_r

    _, h = lax.associative_scan(combine, (a, gated_x), axis=1)
    return h, h[:, -1]


def memory_attend(qm, mem_kv):
    sc = jnp.einsum('nthd,nmhd->nhtm', qm, mem_kv[:, :, 0]) * (MEM_HEAD_DIM ** -0.5)
    p = jax.nn.softmax(sc.astype(jnp.float32), axis=-1)
    return jnp.einsum('nhtm,nmhd->nthd', p.astype(qm.dtype), mem_kv[:, :, 1])


def token_mixer(x, start, mem_kv, win_caches, conv_state, h0, lw):
    n, t, _ = x.shape
    pos = start + jnp.arange(t, dtype=jnp.int32)
    q, k, v, u, g_lru, qm, gate_logits = jnp.split(x @ lw['w_in'], IN_SPLITS, axis=-1)
    heads = N_DIL * HEADS_PER_GROUP
    q = rope_partial(q.reshape(n, t, heads, HEAD_DIM), pos).reshape(n, t, N_DIL, HEADS_PER_GROUP, HEAD_DIM)
    k = rope_partial(k.reshape(n, t, heads, HEAD_DIM), pos).reshape(n, t, N_DIL, HEADS_PER_GROUP, HEAD_DIM)
    v = v.reshape(n, t, N_DIL, HEADS_PER_GROUP, HEAD_DIM)
    outs, lses, new_win = [], [], []
    for gi, (window, dil) in enumerate(DIL_GROUPS):
        qg, kg, vg = q[:, :, gi], k[:, :, gi], v[:, :, gi]
        if win_caches is None:
            o, lse = dilated_group_prompt(qg, kg, vg, window, dil)
            keep = min(window, t)
            new_win.append(jnp.stack([kg[:, t - keep:], vg[:, t - keep:]], axis=2))
        else:
            cache = win_caches[gi]
            k_all = jnp.concatenate([cache[:, :, 0].astype(kg.dtype), kg], axis=1)
            v_all = jnp.concatenate([cache[:, :, 1].astype(vg.dtype), vg], axis=1)
            o, lse = dilated_group_sample(qg, k_all, v_all, pos, start - cache.shape[1], window, dil)
            new_win.append(jnp.stack([kg, vg], axis=2))
        outs.append(o)
        lses.append(lse)
    w_grp = jax.nn.softmax(jnp.stack(lses, axis=0), axis=0)
    y_att = jnp.einsum('gnth,gnthd->nthd', w_grp.astype(x.dtype), jnp.stack(outs, axis=0)).reshape(n, t, ATT_OUT)
    uc, new_conv = causal_conv(u, conv_state, lw['conv_w'], lw['conv_b'])
    h, h_last = rg_lru(uc, h0, lw['w_rg_a'], lw['b_rg_a'], lw['w_rg_x'], lw['b_rg_x'], lw['lru_lambda'])
    y_lru = h.astype(x.dtype) * jax.nn.gelu(g_lru)
    y_mem = memory_attend(qm.reshape(n, t, MEM_HEADS, MEM_HEAD_DIM), mem_kv.astype(x.dtype)).reshape(n, t, MEM_WIDTH)
    g_att, g_rec, g_mem = jnp.split(jax.nn.sigmoid(gate_logits), N_BRANCH, axis=-1)
    merged = (g_att * (y_att @ lw['p_att']) + g_rec * (y_lru @ lw['p_lru'])
              + g_mem * (y_mem @ lw['p_mem']))
    return merged @ lw['w_out'], new_win, new_conv, h_last


def moe(x, w_router, b_router, w_gate_up, b_gate_up, w_down, b_down):
    n, d = x.shape
    logits = (x @ w_router).astype(jnp.float32) + b_router.astype(jnp.float32)
    top_v, top_e = lax.top_k(logits, TOP_K)
    gate = jax.nn.softmax(top_v, axis=-1)
    n_assign = n * TOP_K
    e_flat = top_e.reshape(-1)
    tok_flat = jnp.repeat(jnp.arange(n, dtype=jnp.int32), TOP_K)
    order = jnp.argsort(e_flat)
    e_sorted = e_flat[order]
    counts = jnp.zeros((N_EXPERTS,), jnp.int32).at[e_flat].add(1)
    starts = jnp.cumsum(counts) - counts
    padded = (counts + MOE_BLOCK - 1) // MOE_BLOCK * MOE_BLOCK
    pad_end = jnp.cumsum(padded)
    pad_start = pad_end - padded
    dest = pad_start[e_sorted] + jnp.arange(n_assign, dtype=jnp.int32) - starts[e_sorted]
    n_blocks = -(-n_assign // MOE_BLOCK) + N_EXPERTS
    slot_tok = jnp.zeros((n_blocks * MOE_BLOCK,), jnp.int32).at[dest].set(tok_flat[order])
    slot_gate = jnp.zeros((n_blocks * MOE_BLOCK,), jnp.float32).at[dest].set(gate.reshape(-1)[order])
    blk_expert = jnp.minimum(
        jnp.searchsorted(pad_end, jnp.arange(n_blocks, dtype=jnp.int32) * MOE_BLOCK, side='right'),
        N_EXPERTS - 1)
    x_blocks = x[slot_tok].reshape(n_blocks, MOE_BLOCK, d)

    def expert_block(args):
        xb, e = args
        gu = xb @ w_gate_up[e] + b_gate_up[e]
        g, u = gu[:, :D_FF], gu[:, D_FF:]
        g = jnp.minimum(g, SWIGLU_LIMIT)
        u = jnp.clip(u, -SWIGLU_LIMIT, SWIGLU_LIMIT)
        hdn = (u + 1.0) * (g * jax.nn.sigmoid(SWIGLU_ALPHA * g))
        return hdn @ w_down[e] + b_down[e]

    y_slots = lax.map(expert_block, (x_blocks, blk_expert)).reshape(n_blocks * MOE_BLOCK, d)
    return jax.ops.segment_sum(y_slots * slot_gate[:, None].astype(y_slots.dtype), slot_tok, num_segments=n)


def trunk_layer(x, start, mem_kv, win_caches, conv_state, h0, lw):
    n, t, d = x.shape
    mix, new_win, new_conv, h_last = token_mixer(x, start, mem_kv, win_caches, conv_state, h0, lw)
    x = layer_norm(DN_ALPHA * x + mix, lw['ln1_g'], lw['ln1_b'])
    ffn = moe(x.reshape(n * t, d), lw['w_router'], lw['b_router'], lw['w_gate_up'], lw['b_gate_up'],
              lw['w_down'], lw['b_down']).reshape(n, t, d)
    x = layer_norm(DN_ALPHA * x + ffn, lw['ln2_g'], lw['ln2_b'])
    return x, new_win, new_conv, h_last


def setup_inputs(seed: int = 0) -> dict:
    key = jax.random.key(seed)
    keys = iter(jax.random.split(key, 64))

    def nrm(shape, scale):
        return jax.random.normal(next(keys), shape, jnp.float32) * scale

    d = D_MODEL
    nl = DEPTH
    inp = {}
    inp['x_prompt'] = nrm((BATCH, SEQ, d), 1.0)
    inp['x_sample'] = nrm((DEC_BATCH, DEC_SEQ, d), 1.0)
    inp['mem_prompt'] = nrm((BATCH, MEM_TOKENS, d), 1.0)
    for window, _ in DIL_GROUPS:
        inp[f'cache_kv_w{window}'] = nrm((nl, DEC_BATCH, min(window, PAST_LEN), 2, HEADS_PER_GROUP, HEAD_DIM), 1.0)
    inp['cache_mem_kv'] = nrm((nl, DEC_BATCH, MEM_TOKENS, 2, MEM_HEADS, MEM_HEAD_DIM), 1.0)
    inp['state_conv'] = nrm((nl, DEC_BATCH, CONV_W - 1, LRU_WIDTH), 1.0)
    inp['state_lru_h'] = nrm((nl, DEC_BATCH, LRU_WIDTH), 0.5)
    inp['w_in'] = nrm((nl, d, IN_WIDTH), d ** -0.5)
    inp['conv_w'] = nrm((nl, CONV_W, LRU_WIDTH), CONV_W ** -0.5)
    inp['conv_b'] = nrm((nl, LRU_WIDTH), 0.01)
    inp['w_rg_a'] = nrm((nl, LRU_BLOCKS, LRU_BLOCK, LRU_BLOCK), LRU_BLOCK ** -0.5)
    inp['b_rg_a'] = nrm((nl, LRU_WIDTH), 0.01)
    inp['w_rg_x'] = nrm((nl, LRU_BLOCKS, LRU_BLOCK, LRU_BLOCK), LRU_BLOCK ** -0.5)
    inp['b_rg_x'] = nrm((nl, LRU_WIDTH), 0.01)
    a_c = jax.random.uniform(next(keys), (nl, LRU_WIDTH), jnp.float32, 0.9, 0.999)
    sig = a_c ** (1.0 / LRU_C)
    inp['lru_lambda'] = jnp.log(sig) - jnp.log1p(-sig)
    inp['w_mem_kv'] = nrm((nl, d, 2 * MEM_WIDTH), d ** -0.5)
    inp['p_att'] = nrm((nl, ATT_OUT, d), ATT_OUT ** -0.5)
    inp['p_lru'] = nrm((nl, LRU_WIDTH, d), LRU_WIDTH ** -0.5)
    inp['p_mem'] = nrm((nl, MEM_WIDTH, d), MEM_WIDTH ** -0.5)
    inp['w_out'] = nrm((nl, d, d), d ** -0.5 * DN_BETA)
    inp['ln1_g'] = 1.0 + nrm((nl, d), 0.02)
    inp['ln1_b'] = nrm((nl, d), 0.02)
    inp['w_router'] = nrm((nl, d, N_EXPERTS), d ** -0.5)
    inp['b_router'] = nrm((nl, N_EXPERTS), 0.01)
    inp['w_gate_up'] = nrm((nl, N_EXPERTS, d, 2 * D_FF), d ** -0.5)
    inp['b_gate_up'] = nrm((nl, N_EXPERTS, 2 * D_FF), 0.01)
    inp['w_down'] = nrm((nl, N_EXPERTS, D_FF, d), D_FF ** -0.5 * DN_BETA)
    inp['b_down'] = nrm((nl, N_EXPERTS, d), 0.01)
    inp['ln2_g'] = 1.0 + nrm((nl, d), 0.02)
    inp['ln2_b'] = nrm((nl, d), 0.02)
    return inp


def reference(x_prompt, x_sample, mem_prompt, cache_kv_w128, cache_kv_w512, cache_kv_w2048,
              cache_mem_kv, state_conv, state_lru_h, w_in, conv_w, conv_b, w_rg_a, b_rg_a,
              w_rg_x, b_rg_x, lru_lambda, w_mem_kv, p_att, p_lru, p_mem, w_out, ln1_g, ln1_b,
              w_router, b_router, w_gate_up, b_gate_up, w_down, b_down, ln2_g, ln2_b):
    xp, xs = x_prompt, x_sample
    n_p = xp.shape[0]
    p_win = [[] for _ in DIL_GROUPS]
    s_win = [[] for _ in DIL_GROUPS]
    p_mem_l, p_conv_l, p_h_l, s_conv_l, s_h_l = [], [], [], [], []
    for l in range(DEPTH):
        lw = {'w_in': w_in[l], 'conv_w': conv_w[l], 'conv_b': conv_b[l], 'w_rg_a': w_rg_a[l],
              'b_rg_a': b_rg_a[l], 'w_rg_x': w_rg_x[l], 'b_rg_x': b_rg_x[l], 'lru_lambda': lru_lambda[l],
              'p_att': p_att[l], 'p_lru': p_lru[l], 'p_mem': p_mem[l], 'w_out': w_out[l],
              'ln1_g': ln1_g[l], 'ln1_b': ln1_b[l], 'w_router': w_router[l], 'b_router': b_router[l],
              'w_gate_up': w_gate_up[l], 'b_gate_up': b_gate_up[l], 'w_down': w_down[l],
              'b_down': b_down[l], 'ln2_g': ln2_g[l], 'ln2_b': ln2_b[l]}
        mem_kv_p = jnp.einsum('nmd,de->nme', mem_prompt, w_mem_kv[l]).reshape(
            mem_prompt.shape[0], MEM_TOKENS, 2, MEM_HEADS, MEM_HEAD_DIM)
        xp, win_p, conv_p, h_p = trunk_layer(
            xp, 0, mem_kv_p, None, jnp.zeros((n_p, CONV_W - 1, LRU_WIDTH), xp.dtype),
            jnp.zeros((n_p, LRU_WIDTH), jnp.float32), lw)
        xs, win_s, conv_s, h_s = trunk_layer(
            xs, PAST_LEN, cache_mem_kv[l], (cache_kv_w128[l], cache_kv_w512[l], cache_kv_w2048[l]),
            state_conv[l], state_lru_h[l], lw)
        for gi in range(N_DIL):
            p_win[gi].append(win_p[gi])
            s_win[gi].append(win_s[gi])
        p_mem_l.append(mem_kv_p)
        p_conv_l.append(conv_p)
        p_h_l.append(h_p)
        s_conv_l.append(conv_s)
        s_h_l.append(h_s)
    p_kv_w128, p_kv_w512, p_kv_w2048 = [jnp.stack(a, axis=0) for a in p_win]
    s_kv_w128, s_kv_w512, s_kv_w2048 = [jnp.stack(a, axis=0) for a in s_win]
    p_mem_kv = jnp.stack(p_mem_l, axis=0)
    p_conv = jnp.stack(p_conv_l, axis=0)
    p_lru_h = jnp.stack(p_h_l, axis=0)
    s_conv = jnp.stack(s_conv_l, axis=0)
    s_lru_h = jnp.stack(s_h_l, axis=0)
    y_prompt, y_sample = xp, xs
    return (y_prompt, y_sample, p_kv_w128, p_kv_w512, p_kv_w2048, p_mem_kv, p_conv, p_lru_h,
            s_kv_w128, s_kv_w512, s_kv_w2048, s_conv, s_lru_h)
```

```python
import jax
import jax.numpy as jnp
from jax import lax
from jax.experimental import pallas as pl
from jax.experimental.pallas import tpu as pltpu

F32 = jnp.float32
BF16 = jnp.bfloat16
I32 = jnp.int32

D_MODEL = 2048
BATCH = 8
SEQ = 2048
DEC_BATCH = 32
PAST_LEN = 8192
HEAD_DIM = 128
HEADS = 4
DILATIONS = (1, 4, 16)
WINDOWS = (128, 512, 2048)
ATT_BLOCK = 128
ATT_WIDTH = 1536
ATT_OUT = 512
ROT_DIM = 32
ROPE_THETA = 500000.0
LRU_WIDTH = 1536
LRU_BLOCK = 128
LRU_C = 8.0
CONV_W = 4
MEM_TOKENS = 256
MEM_HEADS = 4
MEM_HEAD_DIM = 256
MEM_WIDTH = 1024
N_EXPERTS = 32
TOP_K = 4
D_FF = 2048
SWIGLU_LIMIT = 7.0
SWIGLU_ALPHA = 1.702
LN_EPS = 1e-5
DN_ALPHA = 2.0 ** 0.25

N_PROMPT = BATCH * SEQ
TOKENS = N_PROMPT + 128
UG_WIDTH = 2 * LRU_WIDTH + MEM_WIDTH

V7X_VMEM_LIMIT = 56 * 1024 * 1024

LRU_COLS = 512
LRU_ROWS = 256
SUBLANES = 8
MOE_ROWS = 256
MOE_BLOCKS = TOKENS * TOP_K // MOE_ROWS + N_EXPERTS
MOE_SLOTS = MOE_BLOCKS * MOE_ROWS
FF_TILE = 512
COMBINE_TOKENS = 128
SAMPLE_STEP = 8


def _params(*sem):
    return pltpu.CompilerParams(dimension_semantics=sem, vmem_limit_bytes=V7X_VMEM_LIMIT)


def _bdot(a, b):
    return jnp.dot(a.astype(BF16), b.astype(BF16), preferred_element_type=F32)


def _hdot(a, b):
    return jnp.dot(a, b, precision=lax.Precision.HIGHEST, preferred_element_type=F32)


def _round_bf16(x):
    return x.astype(BF16).astype(F32)


def _keep_f32(x):
    return x


def _sigmoid(x):
    return 1.0 / (1.0 + jnp.exp(-x))


def _gelu_tanh(x):
    return 0.5 * x * (1.0 + jnp.tanh(0.7978845608028654 * (x + 0.044715 * (x * x * x))))


def _layer_norm(x, g, b):
    mu = jnp.mean(x, axis=-1, keepdims=True)
    xc = x - mu
    var = jnp.mean(xc * xc, axis=-1, keepdims=True)
    return xc * lax.rsqrt(var + LN_EPS) * g + b


def _mm_kernel(x_ref, w_ref, o_ref):
    o_ref[...] = _bdot(x_ref[...], w_ref[...]).astype(o_ref.dtype)


def _mm_hi_kernel(x_ref, w_ref, o_ref):
    o_ref[...] = _hdot(x_ref[...], w_ref[...])


def _matmul(x, w, tm, tn, name, full_precision=False):
    m, k = x.shape
    n = w.shape[1]
    return pl.pallas_call(
        _mm_hi_kernel if full_precision else _mm_kernel,
        out_shape=jax.ShapeDtypeStruct((m, n), F32),
        grid=(n // tn, m // tm),
        in_specs=[pl.BlockSpec((tm, k), lambda j, i: (i, 0)),
                  pl.BlockSpec((k, tn), lambda j, i: (0, j))],
        out_specs=pl.BlockSpec((tm, tn), lambda j, i: (i, j)),
        compiler_params=_params("arbitrary", "arbitrary"),
        name=name,
    )(x, w)


def _rope_tables(pos):
    half = ROT_DIM // 2
    inv_freq = ROPE_THETA ** (-jnp.arange(half, dtype=F32) / half)
    ang = pos.astype(F32)[:, None] * inv_freq[None, :]
    cos, sin = jnp.cos(ang), jnp.sin(ang)
    t = pos.shape[0]
    rest = HEAD_DIM - ROT_DIM
    c = jnp.concatenate([cos, cos, jnp.ones((t, rest), F32)], axis=1)
    sa = jnp.concatenate([-sin, jnp.zeros((t, half + rest), F32)], axis=1)
    sb = jnp.concatenate([jnp.zeros((t, half), F32), sin, jnp.zeros((t, rest), F32)], axis=1)
    return c, sa, sb


def _rope(x, c, sa, sb):
    half = ROT_DIM // 2
    return x * c + pltpu.roll(x, HEAD_DIM - half, 1) * sa + pltpu.roll(x, half, 1) * sb


def _group_merge(lses, outs, operand=_round_bf16):
    m = jnp.maximum(jnp.maximum(lses[0], lses[1]), lses[2])
    es = [jnp.exp(l - m) for l in lses]
    den = es[0] + es[1] + es[2]
    y = None
    for e, o in zip(es, outs):
        term = operand(e / den) * operand(o)
        y = term if y is None else y + term
    return y


def _attn_kernel(q0, q1, q2, k0, k1, k2, v0, v1, v2, c_ref, sa_ref, sb_ref,
                 y_ref, kr0, kr1, kr2, qs, os_, ls):
    c, sa, sb = c_ref[...], sa_ref[...], sb_ref[...]
    scale = HEAD_DIM ** -0.5
    ii = lax.broadcasted_iota(I32, (ATT_BLOCK, ATT_BLOCK), 0)
    jj = lax.broadcasted_iota(I32, (ATT_BLOCK, ATT_BLOCK), 1)
    cur_mask = jj <= ii
    prev_mask = jj >= ii
    dn = (((1,), (1,)), ((), ()))
    for g, (q_ref, k_ref, v_ref, kr_ref) in enumerate(((q0, k0, v0, kr0), (q1, k1, v1, kr1), (q2, k2, v2, kr2))):
        dil = DILATIONS[g]
        qs[...] = _rope(q_ref[...], c, sa, sb)
        kr_ref[...] = _rope(k_ref[...], c, sa, sb)
        o_g, l_g = os_.at[g], ls.at[g]

        def rows_of(start, dil=dil):
            return pl.ds(start, ATT_BLOCK, stride=dil) if dil > 1 else pl.ds(start, ATT_BLOCK)

        for cls in range(dil):
            for b in range(SEQ // (dil * ATT_BLOCK)):
                start = cls + dil * ATT_BLOCK * b
                rows = rows_of(start)
                q = qs[rows, :].astype(BF16)
                kc = kr_ref[rows, :].astype(BF16)
                vc = v_ref[rows, :].astype(BF16)
                s_c = lax.dot_general(q, kc, dn, preferred_element_type=F32) * scale
                s_c = jnp.where(cur_mask, s_c, -jnp.inf)
                m = jnp.max(s_c, axis=-1, keepdims=True)
                if b > 0:
                    prows = rows_of(start - dil * ATT_BLOCK)
                    kp = kr_ref[prows, :].astype(BF16)
                    vp = v_ref[prows, :].astype(BF16)
                    s_p = lax.dot_general(q, kp, dn, preferred_element_type=F32) * scale
                    s_p = jnp.where(prev_mask, s_p, -jnp.inf)
                    m = jnp.maximum(m, jnp.max(s_p, axis=-1, keepdims=True))
                    p_p = jnp.exp(s_p - m)
                p_c = jnp.exp(s_c - m)
                den = jnp.sum(p_c, axis=-1, keepdims=True)
                if b > 0:
                    den = den + jnp.sum(p_p, axis=-1, keepdims=True)
                o = jnp.dot((p_c / den).astype(BF16), vc, preferred_element_type=F32)
                if b > 0:
                    o = o + jnp.dot((p_p / den).astype(BF16), vp, preferred_element_type=F32)
                lse = m + jnp.log(den)
                o_g[rows, :] = o
                l_g[rows, :] = jnp.broadcast_to(lse, (ATT_BLOCK, HEAD_DIM))
    y = _group_merge([ls[0], ls[1], ls[2]], [os_[0], os_[1], os_[2]])
    y_ref[...] = y.astype(y_ref.dtype)


def _prompt_attention(qkv, c, sa, sb):
    blk = (SEQ, HEAD_DIM)
    head_cols = ATT_WIDTH // HEAD_DIM

    def col(base, g):
        return pl.BlockSpec(blk, lambda n, h: (n, base + g * HEADS + h))

    in_specs = ([col(0, g) for g in range(3)] + [col(head_cols, g) for g in range(3)]
                + [col(2 * head_cols, g) for g in range(3)] + [pl.BlockSpec(blk, lambda n, h: (0, 0))] * 3)
    res = pl.pallas_call(
        _attn_kernel,
        out_shape=[jax.ShapeDtypeStruct((N_PROMPT, ATT_OUT), BF16)]
        + [jax.ShapeDtypeStruct((N_PROMPT, ATT_OUT), F32)] * 3,
        grid=(BATCH, HEADS),
        in_specs=in_specs,
        out_specs=[pl.BlockSpec(blk, lambda n, h: (n, h))] * 4,
        scratch_shapes=[pltpu.VMEM(blk, F32), pltpu.VMEM((3,) + blk, F32), pltpu.VMEM((3,) + blk, F32)],
        compiler_params=_params("arbitrary", "arbitrary"),
        name="prompt_attention",
    )(*([qkv] * 9), c, sa, sb)
    return res[0], res[1:]


def _lru_gates(uc, wa_ref, ba, wx_ref, bx, lam, dot=_bdot):
    n_blk = uc.shape[1] // LRU_BLOCK
    r = jnp.concatenate([dot(uc[:, j * LRU_BLOCK:(j + 1) * LRU_BLOCK], wa_ref[j]) for j in range(n_blk)], axis=1)
    i = jnp.concatenate([dot(uc[:, j * LRU_BLOCK:(j + 1) * LRU_BLOCK], wx_ref[j]) for j in range(n_blk)], axis=1)
    r = _sigmoid(r + ba)
    i = _sigmoid(i + bx)
    neg = -lam
    softplus = jnp.maximum(neg, 0.0) + jnp.log1p(jnp.exp(-jnp.abs(neg)))
    log_a = (-LRU_C * r) * softplus
    a = jnp.exp(log_a)
    b = jnp.sqrt(-jnp.tanh(log_a) * (a * a + 1.0)) * (i * uc)
    return a, b


def _lru_kernel(u_ref, g_ref, cw_ref, cb_ref, wa_ref, ba_ref, wx_ref, bx_ref, lam_ref,
                y_ref, hl_ref, hist, carry):
    @pl.when(pl.program_id(2) == 0)
    def _():
        hist[...] = jnp.zeros_like(hist)
        carry[...] = jnp.zeros_like(carry)

    u = u_ref[...]
    cw = cw_ref[...]
    ext = jnp.concatenate([hist[...], u], axis=0)
    off = SUBLANES - (CONV_W - 1)
    conv = ext[off:off + LRU_ROWS] * cw[0:1]
    for t in range(1, CONV_W):
        conv = conv + ext[off + t:off + t + LRU_ROWS] * cw[t:t + 1]
    uc = cb_ref[...] + conv
    hist[...] = u[LRU_ROWS - SUBLANES:]

    a, b = _lru_gates(uc, wa_ref, ba_ref[...], wx_ref, bx_ref[...], lam_ref[...])

    step = lax.broadcasted_iota(I32, a.shape, 0) & (SUBLANES - 1)
    for s in (1, 2, 4):
        a_s = pltpu.roll(a, s, 0)
        b_s = pltpu.roll(b, s, 0)
        valid = step >= s
        b = jnp.where(valid, a * b_s + b, b)
        a = jnp.where(valid, a * a_s, a)
    h_prev = carry[0:1, :]
    hs = []
    for j in range(LRU_ROWS // SUBLANES):
        h_j = a[j * SUBLANES:(j + 1) * SUBLANES] * h_prev + b[j * SUBLANES:(j + 1) * SUBLANES]
        h_prev = h_j[SUBLANES - 1:SUBLANES]
        hs.append(h_j)
    h = jnp.concatenate(hs, axis=0)
    carry[...] = jnp.broadcast_to(h_prev, carry.shape)
    y_ref[...] = (h * _gelu_tanh(g_ref[...])).astype(y_ref.dtype)
    hl_ref[...] = h_prev


def _prompt_lru(ug, conv_w, conv_b, wa, ba, wx, bx, lam):
    t_blocks = SEQ // LRU_ROWS
    c_blocks = LRU_WIDTH // LRU_COLS
    row = pl.BlockSpec((1, LRU_COLS), lambda n, cb, tb: (0, cb))
    gate_w = pl.BlockSpec((LRU_COLS // LRU_BLOCK, LRU_BLOCK, LRU_BLOCK), lambda n, cb, tb: (cb, 0, 0))
    return pl.pallas_call(
        _lru_kernel,
        out_shape=[jax.ShapeDtypeStruct((N_PROMPT, LRU_WIDTH), BF16),
                   jax.ShapeDtypeStruct((BATCH, 1, LRU_WIDTH), F32)],
        grid=(BATCH, c_blocks, t_blocks),
        in_specs=[pl.BlockSpec((LRU_ROWS, LRU_COLS), lambda n, cb, tb: (n * t_blocks + tb, cb)),
                  pl.BlockSpec((LRU_ROWS, LRU_COLS), lambda n, cb, tb: (n * t_blocks + tb, c_blocks + cb)),
                  pl.BlockSpec((CONV_W, LRU_COLS), lambda n, cb, tb: (0, cb)),
                  row, gate_w, row, gate_w, row, row],
        out_specs=[pl.BlockSpec((LRU_ROWS, LRU_COLS), lambda n, cb, tb: (n * t_blocks + tb, cb)),
                   pl.BlockSpec((None, 1, LRU_COLS), lambda n, cb, tb: (n, 0, cb))],
        scratch_shapes=[pltpu.VMEM((SUBLANES, LRU_COLS), F32), pltpu.VMEM((SUBLANES, LRU_COLS), F32)],
        compiler_params=_params("arbitrary", "arbitrary", "arbitrary"),
        name="prompt_lru",
    )(ug, ug, conv_w, conv_b, wa, ba, wx, bx, lam)


def _sample_lru_kernel(h_ref_in, sc_ref, h0_ref, cw_ref, cb_ref, wa_ref, ba_ref, wx_ref, bx_ref, lam_ref,
                       y_ref, h_ref):
    u = h_ref_in[:, 3 * ATT_WIDTH:3 * ATT_WIDTH + LRU_WIDTH]
    g = h_ref_in[:, 3 * ATT_WIDTH + LRU_WIDTH:3 * ATT_WIDTH + 2 * LRU_WIDTH]
    cw = cw_ref[...]
    conv = sc_ref[:, 0:LRU_WIDTH] * cw[0:1]
    for t in range(1, CONV_W - 1):
        conv = conv + sc_ref[:, t * LRU_WIDTH:(t + 1) * LRU_WIDTH] * cw[t:t + 1]
    conv = conv + u * cw[CONV_W - 1:CONV_W]
    uc = cb_ref[...] + conv
    a, b = _lru_gates(uc, wa_ref, ba_ref[...], wx_ref, bx_ref[...], lam_ref[...], dot=_hdot)
    h = b + a * h0_ref[...]
    h_ref[...] = h
    y_ref[...] = h * _gelu_tanh(g)


def _sample_lru(h_in, state_conv, h0, conv_w, conv_b, wa, ba, wx, bx, lam):
    args = (h_in, state_conv, h0, conv_w, conv_b, wa, ba, wx, bx, lam)
    full = lambda shape: pl.BlockSpec(shape, lambda i: (0,) * len(shape))
    return pl.pallas_call(
        _sample_lru_kernel,
        out_shape=[jax.ShapeDtypeStruct((DEC_BATCH, LRU_WIDTH), F32)] * 2,
        grid=(1,),
        in_specs=[full(a.shape) for a in args],
        out_specs=[full((DEC_BATCH, LRU_WIDTH))] * 2,
        compiler_params=_params("arbitrary"),
        name="sample_lru",
    )(*args)


def _mem_attn_kernel(q_ref, k_ref, v_ref, y_ref):
    dn = (((1,), (1,)), ((), ()))
    s = lax.dot_general(q_ref[...].astype(BF16), k_ref[...].astype(BF16), dn,
                        preferred_element_type=F32) * (MEM_HEAD_DIM ** -0.5)
    m = jnp.max(s, axis=-1, keepdims=True)
    p = jnp.exp(s - m)
    p = p / jnp.sum(p, axis=-1, keepdims=True)
    y_ref[...] = _bdot(p, v_ref[...]).astype(y_ref.dtype)


def _prompt_mem_attention(ug, mem_kv):
    rows = 1024
    r_blocks = SEQ // rows
    q_col = 2 * LRU_WIDTH // MEM_HEAD_DIM
    return pl.pallas_call(
        _mem_attn_kernel,
        out_shape=jax.ShapeDtypeStruct((N_PROMPT, MEM_WIDTH), BF16),
        grid=(BATCH, MEM_HEADS, r_blocks),
        in_specs=[pl.BlockSpec((rows, MEM_HEAD_DIM), lambda n, h, r: (n * r_blocks + r, q_col + h)),
                  pl.BlockSpec((MEM_TOKENS, MEM_HEAD_DIM), lambda n, h, r: (n, h)),
                  pl.BlockSpec((MEM_TOKENS, MEM_HEAD_DIM), lambda n, h, r: (n, MEM_HEADS + h))],
        out_specs=pl.BlockSpec((rows, MEM_HEAD_DIM), lambda n, h, r: (n * r_blocks + r, h)),
        compiler_params=_params("arbitrary", "arbitrary", "arbitrary"),
        name="prompt_mem_attention",
    )(ug, mem_kv, mem_kv)


def _column_attend(q, keys, values, scale, extra=None):
    s = jnp.sum(keys * q, axis=1, keepdims=True) * scale
    m = jnp.max(s, axis=0, keepdims=True)
    if extra is not None:
        s_x = jnp.sum(extra[0] * q, axis=1, keepdims=True) * scale
        m = jnp.maximum(m, s_x)
        p_x = jnp.exp(s_x - m)
    p = jnp.exp(s - m)
    den = jnp.sum(p, axis=0, keepdims=True)
    if extra is not None:
        den = den + p_x
    o = jnp.sum((p / den) * values, axis=0, keepdims=True)
    if extra is not None:
        o = o + (p_x / den) * extra[1]
    return o, m + jnp.log(den)


def _sample_win_kernel(qkv_ref, c_ref, sa_ref, sb_ref, w0_ref, w1_ref, w2_ref, ya_ref, kr_ref):
    c, sa, sb = c_ref[...], sa_ref[...], sb_ref[...]
    scale = HEAD_DIM ** -0.5
    kv_cols = HEADS * HEAD_DIM
    for s in range(SAMPLE_STEP):
        row = slice(s, s + 1)
        for h in range(HEADS):
            outs, lses = [], []
            for g, w_ref in enumerate((w0_ref, w1_ref, w2_ref)):
                col = (g * HEADS + h) * HEAD_DIM
                q = _rope(qkv_ref[row, col:col + HEAD_DIM], c, sa, sb)
                k_new = _rope(qkv_ref[row, ATT_WIDTH + col:ATT_WIDTH + col + HEAD_DIM], c, sa, sb)
                v_new = qkv_ref[row, 2 * ATT_WIDTH + col:2 * ATT_WIDTH + col + HEAD_DIM]
                kr_ref[row, col:col + HEAD_DIM] = k_new
                keys = w_ref[s, :, h * HEAD_DIM:(h + 1) * HEAD_DIM]
                values = w_ref[s, :, kv_cols + h * HEAD_DIM:kv_cols + (h + 1) * HEAD_DIM]
                o, lse = _column_attend(q, keys, values, scale, extra=(k_new, v_new))
                outs.append(o)
                lses.append(lse)
            ya_ref[row, h * HEAD_DIM:(h + 1) * HEAD_DIM] = _group_merge(lses, outs, operand=_keep_f32)


def _sample_mem_kernel(h_ref, mem_ref, ym_ref):
    q_col = 3 * ATT_WIDTH + 2 * LRU_WIDTH
    for s in range(SAMPLE_STEP):
        row = slice(s, s + 1)
        for h in range(MEM_HEADS):
            q = h_ref[row, q_col + h * MEM_HEAD_DIM:q_col + (h + 1) * MEM_HEAD_DIM]
            keys = mem_ref[s, :, h * MEM_HEAD_DIM:(h + 1) * MEM_HEAD_DIM]
            values = mem_ref[s, :, MEM_WIDTH + h * MEM_HEAD_DIM:MEM_WIDTH + (h + 1) * MEM_HEAD_DIM]
            o, _ = _column_attend(q, keys, values, MEM_HEAD_DIM ** -0.5)
            ym_ref[row, h * MEM_HEAD_DIM:(h + 1) * MEM_HEAD_DIM] = o


def _sample_attention(h_in, c, sa, sb, caches, cache_mem):
    kv_cols = 2 * HEADS * HEAD_DIM
    full = lambda shape: pl.BlockSpec(shape, lambda i: (0,) * len(shape))
    rows = lambda w: pl.BlockSpec((SAMPLE_STEP, w), lambda i: (i, 0))
    y_att, k_rot = pl.pallas_call(
        _sample_win_kernel,
        out_shape=[jax.ShapeDtypeStruct((DEC_BATCH, ATT_OUT), F32),
                   jax.ShapeDtypeStruct((DEC_BATCH, ATT_WIDTH), F32)],
        grid=(DEC_BATCH // SAMPLE_STEP,),
        in_specs=[rows(h_in.shape[1]), full(c.shape), full(sa.shape), full(sb.shape)]
        + [pl.BlockSpec((SAMPLE_STEP, ATT_BLOCK, kv_cols), lambda i: (i, 0, 0))] * 3,
        out_specs=[rows(ATT_OUT), rows(ATT_WIDTH)],
        compiler_params=_params("arbitrary"),
        name="sample_window_attention",
    )(h_in, c, sa, sb, *caches)
    y_mem = pl.pallas_call(
        _sample_mem_kernel,
        out_shape=jax.ShapeDtypeStruct((DEC_BATCH, MEM_WIDTH), F32),
        grid=(DEC_BATCH // SAMPLE_STEP,),
        in_specs=[rows(h_in.shape[1]),
                  pl.BlockSpec((SAMPLE_STEP, MEM_TOKENS, 2 * MEM_WIDTH), lambda i: (i, 0, 0))],
        out_specs=rows(MEM_WIDTH),
        compiler_params=_params("arbitrary"),
        name="sample_mem_attention",
    )(h_in, cache_mem)
    return y_att, y_mem, k_rot


def _merge_kernel(ya_ref, yl_ref, ym_ref, ga_ref, gl_ref, gm_ref, pa_ref, pl_ref, pm_ref, o_ref):
    merged = _sigmoid(ga_ref[...]) * _bdot(ya_ref[...], pa_ref[...])
    merged = merged + _sigmoid(gl_ref[...]) * _bdot(yl_ref[...], pl_ref[...])
    merged = merged + _sigmoid(gm_ref[...]) * _bdot(ym_ref[...], pm_ref[...])
    o_ref[...] = merged.astype(o_ref.dtype)


def _merge(y_att, y_lru, y_mem, gates, p_att, p_lru, p_mem):
    tm = 256
    rows = lambda w: pl.BlockSpec((tm, w), lambda i: (i, 0))
    full = lambda a: pl.BlockSpec(a.shape, lambda i: (0, 0))
    return pl.pallas_call(
        _merge_kernel,
        out_shape=jax.ShapeDtypeStruct((N_PROMPT, D_MODEL), BF16),
        grid=(N_PROMPT // tm,),
        in_specs=[rows(ATT_OUT), rows(LRU_WIDTH), rows(MEM_WIDTH)]
        + [pl.BlockSpec((tm, D_MODEL), lambda i, b=b: (i, b)) for b in range(3)]
        + [full(p_att), full(p_lru), full(p_mem)],
        out_specs=rows(D_MODEL),
        compiler_params=_params("arbitrary"),
        name="branch_merge",
    )(y_att, y_lru, y_mem, gates, gates, gates, p_att, p_lru, p_mem)


def _sample_merge_kernel(h_ref, pa_ref, pl_ref, pm_ref, o_ref):
    gate_col = 3 * ATT_WIDTH + UG_WIDTH
    gate = lambda b: _sigmoid(h_ref[:, gate_col + b * D_MODEL:gate_col + (b + 1) * D_MODEL])
    o_ref[...] = (gate(0) * pa_ref[...] + gate(1) * pl_ref[...]) + gate(2) * pm_ref[...]


def _sample_merge(h_in, pa, pl_, pm):
    args = (h_in, pa, pl_, pm)
    full = lambda a: pl.BlockSpec(a.shape, lambda i: (0, 0))
    return pl.pallas_call(
        _sample_merge_kernel,
        out_shape=jax.ShapeDtypeStruct((DEC_BATCH, D_MODEL), F32),
        grid=(1,),
        in_specs=[full(a) for a in args],
        out_specs=pl.BlockSpec((DEC_BATCH, D_MODEL), lambda i: (0, 0)),
        compiler_params=_params("arbitrary"),
        name="sample_merge",
    )(*args)


def _ln_route(mix, x_ref, g_ref, b_ref, wr_ref, br_ref, x1_ref, e_ref, gt_ref, dot):
    x1 = _layer_norm(DN_ALPHA * x_ref[...] + mix, g_ref[...], b_ref[...])
    x1_ref[...] = x1
    logits = dot(x1, wr_ref[...]) + br_ref[...]
    lane = lax.broadcasted_iota(I32, logits.shape, 1).astype(F32)
    out_lane = lax.broadcasted_iota(I32, e_ref.shape, 1)
    top_v = []
    e_out = jnp.zeros(e_ref.shape, I32)
    for k in range(TOP_K):
        v = jnp.max(logits, axis=-1, keepdims=True)
        e = jnp.min(jnp.where(logits == v, lane, float(N_EXPERTS)), axis=-1, keepdims=True)
        logits = jnp.where(lane == e, -jnp.inf, logits)
        top_v.append(v)
        e_out = jnp.where(out_lane == k, e.astype(I32), e_out)
    ps = [jnp.exp(v - top_v[0]) for v in top_v]
    den = ps[0] + ps[1] + ps[2] + ps[3]
    g_out = jnp.zeros(gt_ref.shape, F32)
    for k in range(TOP_K):
        g_out = jnp.where(out_lane == k, ps[k] / den, g_out)
    e_ref[...] = e_out
    gt_ref[...] = g_out


def _outproj_kernel(m_ref, x_ref, w_ref, g_ref, b_ref, wr_ref, br_ref, x1_ref, e_ref, gt_ref):
    mix = jnp.dot(m_ref[...], w_ref[...], preferred_element_type=F32)
    _ln_route(mix, x_ref, g_ref, b_ref, wr_ref, br_ref, x1_ref, e_ref, gt_ref, _bdot)


def _sample_ln_router_kernel(mix_ref, x_ref, g_ref, b_ref, wr_ref, br_ref, x1_ref, e_ref, gt_ref):
    _ln_route(mix_ref[...], x_ref, g_ref, b_ref, wr_ref, br_ref, x1_ref, e_ref, gt_ref, _hdot)


def _sample_ln_router(mix, x, ln_g, ln_b, w_router, b_router):
    args = (mix, x, ln_g, ln_b, w_router, b_router)
    full = lambda shape: pl.BlockSpec(shape, lambda i: (0, 0))
    out_shapes = [jax.ShapeDtypeStruct((DEC_BATCH, D_MODEL), F32), jax.ShapeDtypeStruct((DEC_BATCH, HEAD_DIM), I32),
                  jax.ShapeDtypeStruct((DEC_BATCH, HEAD_DIM), F32)]
    return pl.pallas_call(
        _sample_ln_router_kernel,
        out_shape=out_shapes,
        grid=(1,),
        in_specs=[full(a.shape) for a in args],
        out_specs=[full(s.shape) for s in out_shapes],
        compiler_params=_params("arbitrary"),
        name="sample_ln_router",
    )(*args)


def _outproj_router(merged, x, w_out, ln_g, ln_b, w_router, b_router):
    tm = 256
    rows = lambda w: pl.BlockSpec((tm, w), lambda i: (i, 0))
    full = lambda a: pl.BlockSpec(a.shape, lambda i: (0, 0))
    return pl.pallas_call(
        _outproj_kernel,
        out_shape=[jax.ShapeDtypeStruct((N_PROMPT, D_MODEL), F32),
                   jax.ShapeDtypeStruct((N_PROMPT, HEAD_DIM), I32),
                   jax.ShapeDtypeStruct((N_PROMPT, HEAD_DIM), F32)],
        grid=(N_PROMPT // tm,),
        in_specs=[rows(D_MODEL), rows(D_MODEL), full(w_out), full(ln_g), full(ln_b), full(w_router),
                  full(b_router)],
        out_specs=[rows(D_MODEL), rows(HEAD_DIM), rows(HEAD_DIM)],
        compiler_params=_params("arbitrary"),
        name="outproj_ln_router",
    )(merged, x, w_out, ln_g, ln_b, w_router, b_router)


def _gather_rows(idx_hbm, idx_smem, idx_sem, src_hbm, dst, row_sem, per_step, rows_of):
    i = pl.program_id(0)
    n = pl.num_programs(0)
    slot = i % 2

    def idx_copy(step, slot):
        return pltpu.make_async_copy(idx_hbm.at[step], idx_smem.at[slot], idx_sem.at[slot])

    @pl.when(i == 0)
    def _():
        idx_copy(0, 0).start()

    idx_copy(i, slot).wait()

    @pl.when(i + 1 < n)
    def _():
        idx_copy(i + 1, 1 - slot).start()

    def row_copy(r):
        return pltpu.make_async_copy(src_hbm.at[pl.ds(idx_smem[slot, 0, r], 1)], rows_of(r), row_sem)

    def start(r, carry):
        row_copy(r).start()
        return carry

    def wait(r, carry):
        row_copy(r).wait()
        return carry

    lax.fori_loop(0, per_step, start, 0)
    lax.fori_loop(0, per_step, wait, 0)


def _dispatch_kernel(tok_hbm, x_hbm, o_ref, idx_smem, idx_sem, row_sem):
    _gather_rows(tok_hbm, idx_smem, idx_sem, x_hbm, o_ref, row_sem, MOE_ROWS,
                 lambda r: o_ref.at[pl.ds(r, 1)])


def _dispatch(slot_tok, x1):
    return pl.pallas_call(
        _dispatch_kernel,
        out_shape=jax.ShapeDtypeStruct((MOE_SLOTS, D_MODEL), F32),
        grid=(MOE_BLOCKS,),
        in_specs=[pl.BlockSpec(memory_space=pl.ANY), pl.BlockSpec(memory_space=pl.ANY)],
        out_specs=pl.BlockSpec((MOE_ROWS, D_MODEL), lambda i: (i, 0)),
        scratch_shapes=[pltpu.SMEM((2, 1, MOE_ROWS), I32), pltpu.SemaphoreType.DMA((2,)),
                        pltpu.SemaphoreType.DMA],
        compiler_params=_params("arbitrary"),
        name="moe_dispatch",
    )(slot_tok.reshape(MOE_BLOCKS, 1, MOE_ROWS), x1)


def _expert_changed(be_ref, i):
    return jnp.logical_or(i == 0, be_ref[i] != be_ref[jnp.maximum(i - 1, 0)])


def _gate_up_kernel(be_ref, nu_ref, x_ref, wg_ref, wu_ref, bg_ref, bu_ref, o_ref, wg_b, wu_b):
    i = pl.program_id(1)

    @pl.when(_expert_changed(be_ref, i))
    def _():
        wg_b[...] = wg_ref[...].astype(BF16)
        wu_b[...] = wu_ref[...].astype(BF16)

    @pl.when(i < nu_ref[0])
    def _():
        x = x_ref[...].astype(BF16)
        g = jnp.dot(x, wg_b[...], preferred_element_type=F32) + bg_ref[...]
        u = jnp.dot(x, wu_b[...], preferred_element_type=F32) + bu_ref[...]
        g = jnp.minimum(g, SWIGLU_LIMIT)
        u = jnp.clip(u, -SWIGLU_LIMIT, SWIGLU_LIMIT)
        o_ref[...] = ((u + 1.0) * (g * _sigmoid(SWIGLU_ALPHA * g))).astype(o_ref.dtype)

    @pl.when(i >= nu_ref[0])
    def _():
        o_ref[...] = jnp.zeros_like(o_ref)


def _expert_gate_up(blk_expert, n_used, xs, w_gate_up, b_gate_up):
    ff_tiles = D_FF // FF_TILE
    row_blk = lambda j, i, be, nu: jnp.minimum(i, nu[0] - 1)
    return pl.pallas_call(
        _gate_up_kernel,
        out_shape=jax.ShapeDtypeStruct((MOE_SLOTS, D_FF), BF16),
        grid_spec=pltpu.PrefetchScalarGridSpec(
            num_scalar_prefetch=2,
            grid=(ff_tiles, MOE_BLOCKS),
            in_specs=[pl.BlockSpec((MOE_ROWS, D_MODEL), lambda j, i, be, nu: (row_blk(j, i, be, nu), 0)),
                      pl.BlockSpec((None, D_MODEL, FF_TILE), lambda j, i, be, nu: (be[i], 0, j)),
                      pl.BlockSpec((None, D_MODEL, FF_TILE), lambda j, i, be, nu: (be[i], 0, ff_tiles + j)),
                      pl.BlockSpec((None, 1, FF_TILE), lambda j, i, be, nu: (be[i], 0, j)),
                      pl.BlockSpec((None, 1, FF_TILE), lambda j, i, be, nu: (be[i], 0, ff_tiles + j))],
            out_specs=pl.BlockSpec((MOE_ROWS, FF_TILE), lambda j, i, be, nu: (i, j)),
            scratch_shapes=[pltpu.VMEM((D_MODEL, FF_TILE), BF16), pltpu.VMEM((D_MODEL, FF_TILE), BF16)]),
        compiler_params=_params("arbitrary", "arbitrary"),
        name="moe_gate_up",
    )(blk_expert, n_used, xs, w_gate_up, w_gate_up, b_gate_up, b_gate_up)


def _down_kernel(be_ref, nu_ref, h_ref, w_ref, b_ref, o_ref, w_b):
    i = pl.program_id(1)

    @pl.when(_expert_changed(be_ref, i))
    def _():
        w_b[...] = w_ref[...].astype(BF16)

    @pl.when(i < nu_ref[0])
    def _():
        o_ref[...] = jnp.dot(h_ref[...], w_b[...], preferred_element_type=F32) + b_ref[...]

    @pl.when(i >= nu_ref[0])
    def _():
        o_ref[...] = jnp.zeros_like(o_ref)


def _expert_down(blk_expert, n_used, hdn, w_down, b_down):
    n_tiles = D_MODEL // FF_TILE
    row_blk = lambda j, i, be, nu: jnp.minimum(i, nu[0] - 1)
    return pl.pallas_call(
        _down_kernel,
        out_shape=jax.ShapeDtypeStruct((MOE_SLOTS, D_MODEL), F32),
        grid_spec=pltpu.PrefetchScalarGridSpec(
            num_scalar_prefetch=2,
            grid=(n_tiles, MOE_BLOCKS),
            in_specs=[pl.BlockSpec((MOE_ROWS, D_FF), lambda j, i, be, nu: (row_blk(j, i, be, nu), 0)),
                      pl.BlockSpec((None, D_FF, FF_TILE), lambda j, i, be, nu: (be[i], 0, j)),
                      pl.BlockSpec((None, 1, FF_TILE), lambda j, i, be, nu: (be[i], 0, j))],
            out_specs=pl.BlockSpec((MOE_ROWS, FF_TILE), lambda j, i, be, nu: (i, j)),
            scratch_shapes=[pltpu.VMEM((D_FF, FF_TILE), BF16)]),
        compiler_params=_params("arbitrary", "arbitrary"),
        name="moe_down",
    )(blk_expert, n_used, hdn, w_down, b_down)


def _combine_kernel(dest_hbm, ys_hbm, gate_ref, x1_ref, g_ref, b_ref, o_ref, buf, idx_smem, idx_sem, row_sem):
    per_step = COMBINE_TOKENS * TOP_K
    _gather_rows(dest_hbm, idx_smem, idx_sem, ys_hbm, buf, row_sem, per_step,
                 lambda r: buf.at[r % TOP_K, pl.ds(r // TOP_K, 1)])
    gate = gate_ref[...]
    ffn = buf[0] * gate[:, 0:1]
    for k in range(1, TOP_K):
        ffn = ffn + buf[k] * gate[:, k:k + 1]
    o_ref[...] = _layer_norm(DN_ALPHA * x1_ref[...] + ffn, g_ref[...], b_ref[...])


def _combine(dest, y_slots, gate, x1, ln_g, ln_b):
    steps = TOKENS // COMBINE_TOKENS
    rows = lambda w: pl.BlockSpec((COMBINE_TOKENS, w), lambda i: (i, 0))
    full = lambda a: pl.BlockSpec(a.shape, lambda i: (0, 0))
    return pl.pallas_call(
        _combine_kernel,
        out_shape=jax.ShapeDtypeStruct((TOKENS, D_MODEL), F32),
        grid=(steps,),
        in_specs=[pl.BlockSpec(memory_space=pl.ANY), pl.BlockSpec(memory_space=pl.ANY),
                  rows(HEAD_DIM), rows(D_MODEL), full(ln_g), full(ln_b)],
        out_specs=rows(D_MODEL),
        scratch_shapes=[pltpu.VMEM((TOP_K, COMBINE_TOKENS, D_MODEL), F32),
                        pltpu.SMEM((2, 1, COMBINE_TOKENS * TOP_K), I32), pltpu.SemaphoreType.DMA((2,)),
                        pltpu.SemaphoreType.DMA],
        compiler_params=_params("arbitrary"),
        name="moe_combine_ln",
    )(dest.reshape(steps, 1, COMBINE_TOKENS * TOP_K), y_slots, gate, x1, ln_g, ln_b)


def _moe_layout(top_e):
    e_flat = top_e.reshape(-1)
    onehot = (e_flat[:, None] == jnp.arange(N_EXPERTS, dtype=I32)[None, :]).astype(I32)
    running = jnp.cumsum(onehot, axis=0)
    counts = running[-1]
    rank = jnp.sum(running * onehot, axis=1) - 1
    padded = (counts + MOE_ROWS - 1) // MOE_ROWS * MOE_ROWS
    pad_end = jnp.cumsum(padded)
    dest = (pad_end - padded)[e_flat] + rank
    tok = jnp.arange(TOKENS * TOP_K, dtype=I32) // TOP_K
    slot_tok = jnp.zeros((MOE_SLOTS,), I32).at[dest].set(tok)
    n_used = pad_end[-1] // MOE_ROWS
    blk = jnp.minimum(jnp.arange(MOE_BLOCKS, dtype=I32), n_used - 1) * MOE_ROWS
    blk_expert = jnp.minimum(jnp.searchsorted(pad_end, blk, side='right'), N_EXPERTS - 1).astype(I32)
    return dest.astype(I32), slot_tok, blk_expert, n_used.reshape(1).astype(I32)


def kernel(x_prompt, x_sample, mem_prompt, cache_kv_w128, cache_kv_w512, cache_kv_w2048, cache_mem_kv, state_conv, state_lru_h, w_in, conv_w, conv_b, w_rg_a, b_rg_a, w_rg_x, b_rg_x, lru_lambda, w_mem_kv, p_att, p_lru, p_mem, w_out, ln1_g, ln1_b, w_router, b_router, w_gate_up, b_gate_up, w_down, b_down, ln2_g, ln2_b):
    row = lambda a: a[0].reshape(1, -1)
    x_p = x_prompt.reshape(N_PROMPT, D_MODEL)
    x_s = x_sample.reshape(DEC_BATCH, D_MODEL)
    xb = x_p.astype(BF16)
    w_in_b = w_in[0].astype(BF16)
    qkv_cols = 3 * ATT_WIDTH
    gate_col = qkv_cols + UG_WIDTH

    qkv = _matmul(xb, w_in_b[:, :qkv_cols], 1024, 1536, "in_proj_qkv")
    ug = _matmul(xb, w_in_b[:, qkv_cols:gate_col], 1024, 1024, "in_proj_lru_mem")
    gates = _matmul(xb, w_in_b[:, gate_col:], 1024, 1536, "in_proj_gates")

    c_p, sa_p, sb_p = _rope_tables(jnp.arange(SEQ, dtype=I32))
    y_att_p, k_rot = _prompt_attention(qkv, c_p, sa_p, sb_p)
    wa, wx = w_rg_a[0].astype(BF16), w_rg_x[0].astype(BF16)
    lru_args = (conv_w[0], row(conv_b), wa, row(b_rg_a), wx, row(b_rg_x), row(lru_lambda))
    y_lru_p, h_last_p = _prompt_lru(ug, *lru_args)
    mem_kv_p = _matmul(mem_prompt.reshape(BATCH * MEM_TOKENS, D_MODEL), w_mem_kv[0].astype(BF16), 512, 1024,
                       "mem_kv_proj")
    y_mem_p = _prompt_mem_attention(ug, mem_kv_p)

    merged = _merge(y_att_p, y_lru_p, y_mem_p, gates,
                    p_att[0].astype(BF16), p_lru[0].astype(BF16), p_mem[0].astype(BF16))
    x1_p, top_e_p, gate_p = _outproj_router(merged, x_p, w_out[0].astype(BF16), row(ln1_g), row(ln1_b),
                                            w_router[0].astype(BF16), row(b_router))

    h_s_in = _matmul(x_s, w_in[0], DEC_BATCH, 512, "sample_in_proj", full_precision=True)
    c_s, sa_s, sb_s = _rope_tables(jnp.full((1,), PAST_LEN, I32))
    caches = [cache[0].reshape(DEC_BATCH, ATT_BLOCK, -1)
              for cache in (cache_kv_w128, cache_kv_w512, cache_kv_w2048)]
    y_att_s, y_mem_s, k_rot_s = _sample_attention(
        h_s_in, c_s, sa_s, sb_s, caches, cache_mem_kv[0].reshape(DEC_BATCH, MEM_TOKENS, 2 * MEM_WIDTH))
    y_lru_s, h_s = _sample_lru(h_s_in, state_conv[0].reshape(DEC_BATCH, (CONV_W - 1) * LRU_WIDTH), state_lru_h[0],
                               conv_w[0], row(conv_b), w_rg_a[0], row(b_rg_a), w_rg_x[0], row(b_rg_x),
                               row(lru_lambda))
    merged_s = _sample_merge(h_s_in,
                             _matmul(y_att_s, p_att[0], DEC_BATCH, 512, "sample_p_att", full_precision=True),
                             _matmul(y_lru_s, p_lru[0], DEC_BATCH, 512, "sample_p_lru", full_precision=True),
                             _matmul(y_mem_s, p_mem[0], DEC_BATCH, 512, "sample_p_mem", full_precision=True))
    mix_s = _matmul(merged_s, w_out[0], DEC_BATCH, 512, "sample_out_proj", full_precision=True)
    x1_s, top_e_s, gate_s = _sample_ln_router(mix_s, x_s, row(ln1_g), row(ln1_b), w_router[0], row(b_router))

    pad = TOKENS - N_PROMPT - DEC_BATCH
    x1 = jnp.concatenate([x1_p, x1_s, jnp.zeros((pad, D_MODEL), F32)], axis=0)
    pad_e = (jnp.arange(pad, dtype=I32)[:, None] * TOP_K + jnp.arange(HEAD_DIM, dtype=I32)[None, :]) % N_EXPERTS
    top_e = jnp.concatenate([top_e_p, top_e_s, pad_e], axis=0)
    gate = jnp.concatenate([gate_p, gate_s, jnp.zeros((pad, HEAD_DIM), F32)], axis=0)
    dest, slot_tok, blk_expert, n_used = _moe_layout(top_e[:, :TOP_K])
    xs = _dispatch(slot_tok, x1)
    hdn = _expert_gate_up(blk_expert, n_used, xs, w_gate_up[0], b_gate_up[0].reshape(N_EXPERTS, 1, 2 * D_FF))
    y_slots = _expert_down(blk_expert, n_used, hdn, w_down[0], b_down[0].reshape(N_EXPERTS, 1, D_MODEL))
    y = _combine(dest, y_slots, gate, x1, row(ln2_g), row(ln2_b))

    y_prompt = y[:N_PROMPT].reshape(BATCH, SEQ, D_MODEL)
    y_sample = y[N_PROMPT:N_PROMPT + DEC_BATCH].reshape(DEC_BATCH, 1, D_MODEL)
    p_kv, s_kv = [], []
    for g, window in enumerate(WINDOWS):
        keep = min(window, SEQ)
        k_g = k_rot[g].reshape(BATCH, SEQ, HEADS, HEAD_DIM)[:, SEQ - keep:]
        v_cols = slice(2 * ATT_WIDTH + g * ATT_OUT, 2 * ATT_WIDTH + (g + 1) * ATT_OUT)
        v_g = qkv[:, v_cols].reshape(BATCH, SEQ, HEADS, HEAD_DIM)[:, SEQ - keep:]
        p_kv.append(jnp.stack([k_g, v_g], axis=2)[None])
        ks = k_rot_s[:, g * ATT_OUT:(g + 1) * ATT_OUT].reshape(DEC_BATCH, 1, HEADS, HEAD_DIM)
        vs = h_s_in[:, v_cols].reshape(DEC_BATCH, 1, HEADS, HEAD_DIM)
        s_kv.append(jnp.stack([ks, vs], axis=2)[None])
    p_mem_kv = mem_kv_p.reshape(1, BATCH, MEM_TOKENS, 2, MEM_HEADS, MEM_HEAD_DIM)
    u_p = ug[:, :LRU_WIDTH].reshape(BATCH, SEQ, LRU_WIDTH)
    p_conv = u_p[:, SEQ - (CONV_W - 1):][None]
    p_lru_h = h_last_p.reshape(1, BATCH, LRU_WIDTH)
    u_s = h_s_in[:, qkv_cols:qkv_cols + LRU_WIDTH]
    s_conv = jnp.concatenate([state_conv[0][:, 1:], u_s[:, None, :]], axis=1)[None]
    s_lru_h = h_s[None]
    return (y_prompt, y_sample, p_kv[0], p_kv[1], p_kv[2], p_mem_kv, p_conv, p_lru_h,
            s_kv[0], s_kv[1], s_kv[2], s_conv, s_lru_h)
```

```python
import jax
import jax.numpy as jnp
from jax import lax
from jax.experimental import pallas as pl
from jax.experimental.pallas import tpu as pltpu

F32 = jnp.float32
BF16 = jnp.bfloat16
I32 = jnp.int32

D_MODEL = 2048
BATCH = 8
SEQ = 2048
DEC_BATCH = 32
PAST_LEN = 8192
HEAD_DIM = 128
HEADS = 4
DILATIONS = (1, 4, 16)
WINDOWS = (128, 512, 2048)
ATT_BLOCK = 128
ATT_WIDTH = 1536
ATT_OUT = 512
ROT_DIM = 32
ROPE_THETA = 500000.0
LRU_WIDTH = 1536
LRU_BLOCK = 128
LRU_C = 8.0
CONV_W = 4
MEM_TOKENS = 256
MEM_HEADS = 4
MEM_HEAD_DIM = 256
MEM_WIDTH = 1024
N_EXPERTS = 32
TOP_K = 4
D_FF = 2048
SWIGLU_LIMIT = 7.0
SWIGLU_ALPHA = 1.702
LN_EPS = 1e-5
DN_ALPHA = 2.0 ** 0.25

N_PROMPT = BATCH * SEQ
OUTPROJ_ROWS = 256
TOKENS = N_PROMPT + OUTPROJ_ROWS
UG_WIDTH = 2 * LRU_WIDTH + MEM_WIDTH

V7X_VMEM_LIMIT = 56 * 1024 * 1024

LRU_COLS = 512
LRU_ROWS = 256
SUBLANES = 8
MOE_ROWS = 256
MOE_BLOCKS = TOKENS * TOP_K // MOE_ROWS + N_EXPERTS
MOE_SLOTS = MOE_BLOCKS * MOE_ROWS
FF_TILE = 1024
COMBINE_TOKENS = 128
LANES = 128
TOKEN_ROWS = D_MODEL // LANES
SAMPLE_STEP = 8


def _params(*sem):
    return pltpu.CompilerParams(dimension_semantics=sem, vmem_limit_bytes=V7X_VMEM_LIMIT)


def _bdot(a, b):
    return jnp.dot(a.astype(BF16), b.astype(BF16), preferred_element_type=F32)


def _hdot(a, b):
    return jnp.dot(a, b, precision=lax.Precision.HIGHEST, preferred_element_type=F32)


def _round_bf16(x):
    return x.astype(BF16).astype(F32)


def _keep_f32(x):
    return x


def _sigmoid(x):
    return 1.0 / (1.0 + jnp.exp(-x))


def _gelu_tanh(x):
    return 0.5 * x * (1.0 + jnp.tanh(0.7978845608028654 * (x + 0.044715 * (x * x * x))))


def _layer_norm(x, g, b):
    mu = jnp.mean(x, axis=-1, keepdims=True)
    xc = x - mu
    var = jnp.mean(xc * xc, axis=-1, keepdims=True)
    return xc * lax.rsqrt(var + LN_EPS) * g + b


def _mm_kernel(x_ref, w_ref, o_ref):
    o_ref[...] = _bdot(x_ref[...], w_ref[...]).astype(o_ref.dtype)


def _mm_hi_kernel(x_ref, w_ref, o_ref):
    o_ref[...] = _hdot(x_ref[...], w_ref[...])


def _matmul(x, w, tm, tn, name, full_precision=False):
    m, k = x.shape
    n = w.shape[1]
    return pl.pallas_call(
        _mm_hi_kernel if full_precision else _mm_kernel,
        out_shape=jax.ShapeDtypeStruct((m, n), F32),
        grid=(n // tn, m // tm),
        in_specs=[pl.BlockSpec((tm, k), lambda j, i: (i, 0)),
                  pl.BlockSpec((k, tn), lambda j, i: (0, j))],
        out_specs=pl.BlockSpec((tm, tn), lambda j, i: (i, j)),
        compiler_params=_params("arbitrary", "arbitrary"),
        name=name,
    )(x, w)


def _rope_tables(pos):
    half = ROT_DIM // 2
    inv_freq = ROPE_THETA ** (-jnp.arange(half, dtype=F32) / half)
    ang = pos.astype(F32)[:, None] * inv_freq[None, :]
    cos, sin = jnp.cos(ang), jnp.sin(ang)
    t = pos.shape[0]
    rest = HEAD_DIM - ROT_DIM
    c = jnp.concatenate([cos, cos, jnp.ones((t, rest), F32)], axis=1)
    sa = jnp.concatenate([-sin, jnp.zeros((t, half + rest), F32)], axis=1)
    sb = jnp.concatenate([jnp.zeros((t, half), F32), sin, jnp.zeros((t, rest), F32)], axis=1)
    return c, sa, sb


def _rope(x, c, sa, sb):
    half = ROT_DIM // 2
    return x * c + pltpu.roll(x, HEAD_DIM - half, 1) * sa + pltpu.roll(x, half, 1) * sb


def _group_merge(lses, outs, operand=_round_bf16):
    m = jnp.maximum(jnp.maximum(lses[0], lses[1]), lses[2])
    es = [jnp.exp(l - m) for l in lses]
    den = es[0] + es[1] + es[2]
    y = None
    for e, o in zip(es, outs):
        term = operand(e / den) * operand(o)
        y = term if y is None else y + term
    return y


def _attn_kernel(q0, q1, q2, k0, k1, k2, v0, v1, v2, c_ref, sa_ref, sb_ref,
                 y_ref, kr0, kr1, kr2, qs, os_, ls):
    c, sa, sb = c_ref[...], sa_ref[...], sb_ref[...]
    scale = HEAD_DIM ** -0.5
    ii = lax.broadcasted_iota(I32, (ATT_BLOCK, ATT_BLOCK), 0)
    jj = lax.broadcasted_iota(I32, (ATT_BLOCK, ATT_BLOCK), 1)
    cur_mask = jj <= ii
    prev_mask = jj >= ii
    dn = (((1,), (1,)), ((), ()))
    for g, (q_ref, k_ref, v_ref, kr_ref) in enumerate(((q0, k0, v0, kr0), (q1, k1, v1, kr1), (q2, k2, v2, kr2))):
        dil = DILATIONS[g]
        qs[...] = _rope(q_ref[...], c, sa, sb)
        kr_ref[...] = _rope(k_ref[...], c, sa, sb)
        o_g, l_g = os_.at[g], ls.at[g]

        def rows_of(start, dil=dil):
            return pl.ds(start, ATT_BLOCK, stride=dil) if dil > 1 else pl.ds(start, ATT_BLOCK)

        for cls in range(dil):
            for b in range(SEQ // (dil * ATT_BLOCK)):
                start = cls + dil * ATT_BLOCK * b
                rows = rows_of(start)
                q = qs[rows, :].astype(BF16)
                kc = kr_ref[rows, :].astype(BF16)
                vc = v_ref[rows, :].astype(BF16)
                s_c = lax.dot_general(q, kc, dn, preferred_element_type=F32) * scale
                s_c = jnp.where(cur_mask, s_c, -jnp.inf)
                m = jnp.max(s_c, axis=-1, keepdims=True)
                if b > 0:
                    prows = rows_of(start - dil * ATT_BLOCK)
                    kp = kr_ref[prows, :].astype(BF16)
                    vp = v_ref[prows, :].astype(BF16)
                    s_p = lax.dot_general(q, kp, dn, preferred_element_type=F32) * scale
                    s_p = jnp.where(prev_mask, s_p, -jnp.inf)
                    m = jnp.maximum(m, jnp.max(s_p, axis=-1, keepdims=True))
                    p_p = jnp.exp(s_p - m)
                p_c = jnp.exp(s_c - m)
                den = jnp.sum(p_c, axis=-1, keepdims=True)
                if b > 0:
                    den = den + jnp.sum(p_p, axis=-1, keepdims=True)
                o = jnp.dot((p_c / den).astype(BF16), vc, preferred_element_type=F32)
                if b > 0:
                    o = o + jnp.dot((p_p / den).astype(BF16), vp, preferred_element_type=F32)
                lse = m + jnp.log(den)
                o_g[rows, :] = o
                l_g[rows, :] = jnp.broadcast_to(lse, (ATT_BLOCK, HEAD_DIM))
    y = _group_merge([ls[0], ls[1], ls[2]], [os_[0], os_[1], os_[2]])
    y_ref[...] = y.astype(y_ref.dtype)


def _prompt_attention(qkv, c, sa, sb):
    blk = (SEQ, HEAD_DIM)
    head_cols = ATT_WIDTH // HEAD_DIM

    def col(base, g):
        return pl.BlockSpec(blk, lambda n, h: (n, base + g * HEADS + h))

    in_specs = ([col(0, g) for g in range(3)] + [col(head_cols, g) for g in range(3)]
                + [col(2 * head_cols, g) for g in range(3)] + [pl.BlockSpec(blk, lambda n, h: (0, 0))] * 3)
    res = pl.pallas_call(
        _attn_kernel,
        out_shape=[jax.ShapeDtypeStruct((N_PROMPT, ATT_OUT), BF16)]
        + [jax.ShapeDtypeStruct((N_PROMPT, ATT_OUT), F32)] * 3,
        grid=(BATCH, HEADS),
        in_specs=in_specs,
        out_specs=[pl.BlockSpec(blk, lambda n, h: (n, h))] * 4,
        scratch_shapes=[pltpu.VMEM(blk, F32), pltpu.VMEM((3,) + blk, F32), pltpu.VMEM((3,) + blk, F32)],
        compiler_params=_params("arbitrary", "arbitrary"),
        name="prompt_attention",
    )(*([qkv] * 9), c, sa, sb)
    return res[0], res[1:]


def _lru_gates(uc, wa_ref, ba, wx_ref, bx, lam, dot=_bdot):
    n_blk = uc.shape[1] // LRU_BLOCK
    r = jnp.concatenate([dot(uc[:, j * LRU_BLOCK:(j + 1) * LRU_BLOCK], wa_ref[j]) for j in range(n_blk)], axis=1)
    i = jnp.concatenate([dot(uc[:, j * LRU_BLOCK:(j + 1) * LRU_BLOCK], wx_ref[j]) for j in range(n_blk)], axis=1)
    r = _sigmoid(r + ba)
    i = _sigmoid(i + bx)
    neg = -lam
    softplus = jnp.maximum(neg, 0.0) + jnp.log1p(jnp.exp(-jnp.abs(neg)))
    log_a = (-LRU_C * r) * softplus
    a = jnp.exp(log_a)
    b = jnp.sqrt(-jnp.tanh(log_a) * (a * a + 1.0)) * (i * uc)
    return a, b


def _lru_kernel(u_ref, g_ref, cw_ref, cb_ref, wa_ref, ba_ref, wx_ref, bx_ref, lam_ref,
                y_ref, hl_ref, hist, carry):
    @pl.when(pl.program_id(2) == 0)
    def _():
        hist[...] = jnp.zeros_like(hist)
        carry[...] = jnp.zeros_like(carry)

    u = u_ref[...]
    cw = cw_ref[...]
    ext = jnp.concatenate([hist[...], u], axis=0)
    off = SUBLANES - (CONV_W - 1)
    conv = ext[off:off + LRU_ROWS] * cw[0:1]
    for t in range(1, CONV_W):
        conv = conv + ext[off + t:off + t + LRU_ROWS] * cw[t:t + 1]
    uc = cb_ref[...] + conv
    hist[...] = u[LRU_ROWS - SUBLANES:]

    a, b = _lru_gates(uc, wa_ref, ba_ref[...], wx_ref, bx_ref[...], lam_ref[...])

    step = lax.broadcasted_iota(I32, a.shape, 0) & (SUBLANES - 1)
    for s in (1, 2, 4):
        a_s = pltpu.roll(a, s, 0)
        b_s = pltpu.roll(b, s, 0)
        valid = step >= s
        b = jnp.where(valid, a * b_s + b, b)
        a = jnp.where(valid, a * a_s, a)
    h_prev = carry[0:1, :]
    hs = []
    for j in range(LRU_ROWS // SUBLANES):
        h_j = a[j * SUBLANES:(j + 1) * SUBLANES] * h_prev + b[j * SUBLANES:(j + 1) * SUBLANES]
        h_prev = h_j[SUBLANES - 1:SUBLANES]
        hs.append(h_j)
    h = jnp.concatenate(hs, axis=0)
    carry[...] = jnp.broadcast_to(h_prev, carry.shape)
    y_ref[...] = (h * _gelu_tanh(g_ref[...])).astype(y_ref.dtype)
    hl_ref[...] = h_prev


def _prompt_lru(ug, conv_w, conv_b, wa, ba, wx, bx, lam):
    t_blocks = SEQ // LRU_ROWS
    c_blocks = LRU_WIDTH // LRU_COLS
    row = pl.BlockSpec((1, LRU_COLS), lambda n, cb, tb: (0, cb))
    gate_w = pl.BlockSpec((LRU_COLS // LRU_BLOCK, LRU_BLOCK, LRU_BLOCK), lambda n, cb, tb: (cb, 0, 0))
    return pl.pallas_call(
        _lru_kernel,
        out_shape=[jax.ShapeDtypeStruct((N_PROMPT, LRU_WIDTH), BF16),
                   jax.ShapeDtypeStruct((BATCH, 1, LRU_WIDTH), F32)],
        grid=(BATCH, c_blocks, t_blocks),
        in_specs=[pl.BlockSpec((LRU_ROWS, LRU_COLS), lambda n, cb, tb: (n * t_blocks + tb, cb)),
                  pl.BlockSpec((LRU_ROWS, LRU_COLS), lambda n, cb, tb: (n * t_blocks + tb, c_blocks + cb)),
                  pl.BlockSpec((CONV_W, LRU_COLS), lambda n, cb, tb: (0, cb)),
                  row, gate_w, row, gate_w, row, row],
        out_specs=[pl.BlockSpec((LRU_ROWS, LRU_COLS), lambda n, cb, tb: (n * t_blocks + tb, cb)),
                   pl.BlockSpec((None, 1, LRU_COLS), lambda n, cb, tb: (n, 0, cb))],
        scratch_shapes=[pltpu.VMEM((SUBLANES, LRU_COLS), F32), pltpu.VMEM((SUBLANES, LRU_COLS), F32)],
        compiler_params=_params("arbitrary", "arbitrary", "arbitrary"),
        name="prompt_lru",
    )(ug, ug, conv_w, conv_b, wa, ba, wx, bx, lam)


def _sample_lru_kernel(h_ref_in, sc_ref, h0_ref, cw_ref, cb_ref, wa_ref, ba_ref, wx_ref, bx_ref, lam_ref,
                       y_ref, h_ref):
    u = h_ref_in[:, 3 * ATT_WIDTH:3 * ATT_WIDTH + LRU_WIDTH]
    g = h_ref_in[:, 3 * ATT_WIDTH + LRU_WIDTH:3 * ATT_WIDTH + 2 * LRU_WIDTH]
    cw = cw_ref[...]
    conv = sc_ref[:, 0:LRU_WIDTH] * cw[0:1]
    for t in range(1, CONV_W - 1):
        conv = conv + sc_ref[:, t * LRU_WIDTH:(t + 1) * LRU_WIDTH] * cw[t:t + 1]
    conv = conv + u * cw[CONV_W - 1:CONV_W]
    uc = cb_ref[...] + conv
    a, b = _lru_gates(uc, wa_ref, ba_ref[...], wx_ref, bx_ref[...], lam_ref[...], dot=_hdot)
    h = b + a * h0_ref[...]
    h_ref[...] = h
    y_ref[...] = h * _gelu_tanh(g)


def _sample_lru(h_in, state_conv, h0, conv_w, conv_b, wa, ba, wx, bx, lam):
    args = (h_in, state_conv, h0, conv_w, conv_b, wa, ba, wx, bx, lam)
    full = lambda shape: pl.BlockSpec(shape, lambda i: (0,) * len(shape))
    return pl.pallas_call(
        _sample_lru_kernel,
        out_shape=[jax.ShapeDtypeStruct((DEC_BATCH, LRU_WIDTH), F32)] * 2,
        grid=(1,),
        in_specs=[full(a.shape) for a in args],
        out_specs=[full((DEC_BATCH, LRU_WIDTH))] * 2,
        compiler_params=_params("arbitrary"),
        name="sample_lru",
    )(*args)


def _mem_attn_kernel(q_ref, k_ref, v_ref, y_ref):
    dn = (((1,), (1,)), ((), ()))
    s = lax.dot_general(q_ref[...].astype(BF16), k_ref[...].astype(BF16), dn,
                        preferred_element_type=F32) * (MEM_HEAD_DIM ** -0.5)
    m = jnp.max(s, axis=-1, keepdims=True)
    p = jnp.exp(s - m)
    p = p / jnp.sum(p, axis=-1, keepdims=True)
    y_ref[...] = _bdot(p, v_ref[...]).astype(y_ref.dtype)


def _prompt_mem_attention(ug, mem_kv):
    rows = 1024
    r_blocks = SEQ // rows
    q_col = 2 * LRU_WIDTH // MEM_HEAD_DIM
    return pl.pallas_call(
        _mem_attn_kernel,
        out_shape=jax.ShapeDtypeStruct((N_PROMPT, MEM_WIDTH), BF16),
        grid=(BATCH, MEM_HEADS, r_blocks),
        in_specs=[pl.BlockSpec((rows, MEM_HEAD_DIM), lambda n, h, r: (n * r_blocks + r, q_col + h)),
                  pl.BlockSpec((MEM_TOKENS, MEM_HEAD_DIM), lambda n, h, r: (n, h)),
                  pl.BlockSpec((MEM_TOKENS, MEM_HEAD_DIM), lambda n, h, r: (n, MEM_HEADS + h))],
        out_specs=pl.BlockSpec((rows, MEM_HEAD_DIM), lambda n, h, r: (n * r_blocks + r, h)),
        compiler_params=_params("arbitrary", "arbitrary", "arbitrary"),
        name="prompt_mem_attention",
    )(ug, mem_kv, mem_kv)


def _column_attend(q, keys, values, scale, extra=None):
    s = jnp.sum(keys * q, axis=1, keepdims=True) * scale
    m = jnp.max(s, axis=0, keepdims=True)
    if extra is not None:
        s_x = jnp.sum(extra[0] * q, axis=1, keepdims=True) * scale
        m = jnp.maximum(m, s_x)
        p_x = jnp.exp(s_x - m)
    p = jnp.exp(s - m)
    den = jnp.sum(p, axis=0, keepdims=True)
    if extra is not None:
        den = den + p_x
    o = jnp.sum((p / den) * values, axis=0, keepdims=True)
    if extra is not None:
        o = o + (p_x / den) * extra[1]
    return o, m + jnp.log(den)


def _sample_win_kernel(qkv_ref, c_ref, sa_ref, sb_ref, w0_ref, w1_ref, w2_ref, ya_ref, kr_ref):
    c, sa, sb = c_ref[...], sa_ref[...], sb_ref[...]
    scale = HEAD_DIM ** -0.5
    kv_cols = HEADS * HEAD_DIM
    for s in range(SAMPLE_STEP):
        row = slice(s, s + 1)
        for h in range(HEADS):
            outs, lses = [], []
            for g, w_ref in enumerate((w0_ref, w1_ref, w2_ref)):
                col = (g * HEADS + h) * HEAD_DIM
                q = _rope(qkv_ref[row, col:col + HEAD_DIM], c, sa, sb)
                k_new = _rope(qkv_ref[row, ATT_WIDTH + col:ATT_WIDTH + col + HEAD_DIM], c, sa, sb)
                v_new = qkv_ref[row, 2 * ATT_WIDTH + col:2 * ATT_WIDTH + col + HEAD_DIM]
                kr_ref[row, col:col + HEAD_DIM] = k_new
                keys = w_ref[s, :, h * HEAD_DIM:(h + 1) * HEAD_DIM]
                values = w_ref[s, :, kv_cols + h * HEAD_DIM:kv_cols + (h + 1) * HEAD_DIM]
                o, lse = _column_attend(q, keys, values, scale, extra=(k_new, v_new))
                outs.append(o)
                lses.append(lse)
            ya_ref[row, h * HEAD_DIM:(h + 1) * HEAD_DIM] = _group_merge(lses, outs, operand=_keep_f32)


def _sample_mem_kernel(h_ref, mem_ref, ym_ref):
    q_col = 3 * ATT_WIDTH + 2 * LRU_WIDTH
    for s in range(SAMPLE_STEP):
        row = slice(s, s + 1)
        for h in range(MEM_HEADS):
            q = h_ref[row, q_col + h * MEM_HEAD_DIM:q_col + (h + 1) * MEM_HEAD_DIM]
            keys = mem_ref[s, :, h * MEM_HEAD_DIM:(h + 1) * MEM_HEAD_DIM]
            values = mem_ref[s, :, MEM_WIDTH + h * MEM_HEAD_DIM:MEM_WIDTH + (h + 1) * MEM_HEAD_DIM]
            o, _ = _column_attend(q, keys, values, MEM_HEAD_DIM ** -0.5)
            ym_ref[row, h * MEM_HEAD_DIM:(h + 1) * MEM_HEAD_DIM] = o


def _sample_attention(h_in, c, sa, sb, caches, cache_mem):
    kv_cols = 2 * HEADS * HEAD_DIM
    full = lambda shape: pl.BlockSpec(shape, lambda i: (0,) * len(shape))
    rows = lambda w: pl.BlockSpec((SAMPLE_STEP, w), lambda i: (i, 0))
    y_att, k_rot = pl.pallas_call(
        _sample_win_kernel,
        out_shape=[jax.ShapeDtypeStruct((DEC_BATCH, ATT_OUT), F32),
                   jax.ShapeDtypeStruct((DEC_BATCH, ATT_WIDTH), F32)],
        grid=(DEC_BATCH // SAMPLE_STEP,),
        in_specs=[rows(h_in.shape[1]), full(c.shape), full(sa.shape), full(sb.shape)]
        + [pl.BlockSpec((SAMPLE_STEP, ATT_BLOCK, kv_cols), lambda i: (i, 0, 0))] * 3,
        out_specs=[rows(ATT_OUT), rows(ATT_WIDTH)],
        compiler_params=_params("arbitrary"),
        name="sample_window_attention",
    )(h_in, c, sa, sb, *caches)
    y_mem = pl.pallas_call(
        _sample_mem_kernel,
        out_shape=jax.ShapeDtypeStruct((DEC_BATCH, MEM_WIDTH), F32),
        grid=(DEC_BATCH // SAMPLE_STEP,),
        in_specs=[rows(h_in.shape[1]),
                  pl.BlockSpec((SAMPLE_STEP, MEM_TOKENS, 2 * MEM_WIDTH), lambda i: (i, 0, 0))],
        out_specs=rows(MEM_WIDTH),
        compiler_params=_params("arbitrary"),
        name="sample_mem_attention",
    )(h_in, cache_mem)
    return y_att, y_mem, k_rot


def _merge_kernel(ya_ref, yl_ref, ym_ref, ga_ref, gl_ref, gm_ref, pa_ref, pl_ref, pm_ref, o_ref):
    merged = _sigmoid(ga_ref[...]) * _bdot(ya_ref[...], pa_ref[...])
    merged = merged + _sigmoid(gl_ref[...]) * _bdot(yl_ref[...], pl_ref[...])
    merged = merged + _sigmoid(gm_ref[...]) * _bdot(ym_ref[...], pm_ref[...])
    o_ref[...] = merged.astype(o_ref.dtype)


def _merge(y_att, y_lru, y_mem, gates, p_att, p_lru, p_mem):
    tm = 256
    rows = lambda w: pl.BlockSpec((tm, w), lambda i: (i, 0))
    full = lambda a: pl.BlockSpec(a.shape, lambda i: (0, 0))
    return pl.pallas_call(
        _merge_kernel,
        out_shape=jax.ShapeDtypeStruct((N_PROMPT, D_MODEL), BF16),
        grid=(N_PROMPT // tm,),
        in_specs=[rows(ATT_OUT), rows(LRU_WIDTH), rows(MEM_WIDTH)]
        + [pl.BlockSpec((tm, D_MODEL), lambda i, b=b: (i, b)) for b in range(3)]
        + [full(p_att), full(p_lru), full(p_mem)],
        out_specs=rows(D_MODEL),
        compiler_params=_params("arbitrary"),
        name="branch_merge",
    )(y_att, y_lru, y_mem, gates, gates, gates, p_att, p_lru, p_mem)


def _sample_merge_kernel(h_ref, pa_ref, pl_ref, pm_ref, o_ref):
    gate_col = 3 * ATT_WIDTH + UG_WIDTH
    gate = lambda b: _sigmoid(h_ref[:, gate_col + b * D_MODEL:gate_col + (b + 1) * D_MODEL])
    o_ref[...] = (gate(0) * pa_ref[...] + gate(1) * pl_ref[...]) + gate(2) * pm_ref[...]


def _sample_merge(h_in, pa, pl_, pm):
    args = (h_in, pa, pl_, pm)
    full = lambda a: pl.BlockSpec(a.shape, lambda i: (0, 0))
    return pl.pallas_call(
        _sample_merge_kernel,
        out_shape=jax.ShapeDtypeStruct((DEC_BATCH, D_MODEL), F32),
        grid=(1,),
        in_specs=[full(a) for a in args],
        out_specs=pl.BlockSpec((DEC_BATCH, D_MODEL), lambda i: (0, 0)),
        compiler_params=_params("arbitrary"),
        name="sample_merge",
    )(*args)


def _ln_route(mix, x_ref, g_ref, b_ref, wr_ref, br_ref, x1_ref, e_ref, gt_ref, dot):
    x1 = _layer_norm(DN_ALPHA * x_ref[...] + mix, g_ref[...], b_ref[...])
    x1_ref[...] = x1
    logits = dot(x1, wr_ref[...]) + br_ref[...]
    lane = lax.broadcasted_iota(I32, logits.shape, 1).astype(F32)
    out_lane = lax.broadcasted_iota(I32, e_ref.shape, 1)
    top_v = []
    e_out = jnp.zeros(e_ref.shape, I32)
    for k in range(TOP_K):
        v = jnp.max(logits, axis=-1, keepdims=True)
        e = jnp.min(jnp.where(logits == v, lane, float(N_EXPERTS)), axis=-1, keepdims=True)
        logits = jnp.where(lane == e, -jnp.inf, logits)
        top_v.append(v)
        e_out = jnp.where(out_lane == k, e.astype(I32), e_out)
    ps = [jnp.exp(v - top_v[0]) for v in top_v]
    den = ps[0] + ps[1] + ps[2] + ps[3]
    g_out = jnp.zeros(gt_ref.shape, F32)
    for k in range(TOP_K):
        g_out = jnp.where(out_lane == k, ps[k] / den, g_out)
    e_ref[...] = e_out
    gt_ref[...] = g_out


def _outproj_kernel(m_ref, x_ref, w_ref, g_ref, b_ref, wr_ref, br_ref, x1_ref, x1r_ref, e_ref, gt_ref):
    @pl.when(pl.program_id(0) < N_PROMPT // OUTPROJ_ROWS)
    def _():
        mix = jnp.dot(m_ref[...], w_ref[...], preferred_element_type=F32)
        _ln_route(mix, x_ref, g_ref, b_ref, wr_ref, br_ref, x1_ref, e_ref, gt_ref, _bdot)
        _to_token_rows(x1_ref[...], x1r_ref, OUTPROJ_ROWS)

    @pl.when(pl.program_id(0) >= N_PROMPT // OUTPROJ_ROWS)
    def _():
        for ref in (x1_ref, x1r_ref, e_ref, gt_ref):
            ref[...] = jnp.zeros_like(ref)


def _sample_ln_router_kernel(mix_ref, x_ref, g_ref, b_ref, wr_ref, br_ref, x1_ref, e_ref, gt_ref):
    _ln_route(mix_ref[...], x_ref, g_ref, b_ref, wr_ref, br_ref, x1_ref, e_ref, gt_ref, _hdot)


def _sample_ln_router(mix, x, ln_g, ln_b, w_router, b_router):
    args = (mix, x, ln_g, ln_b, w_router, b_router)
    full = lambda shape: pl.BlockSpec(shape, lambda i: (0, 0))
    out_shapes = [jax.ShapeDtypeStruct((DEC_BATCH, D_MODEL), F32), jax.ShapeDtypeStruct((DEC_BATCH, HEAD_DIM), I32),
                  jax.ShapeDtypeStruct((DEC_BATCH, HEAD_DIM), F32)]
    return pl.pallas_call(
        _sample_ln_router_kernel,
        out_shape=out_shapes,
        grid=(1,),
        in_specs=[full(a.shape) for a in args],
        out_specs=[full(s.shape) for s in out_shapes],
        compiler_params=_params("arbitrary"),
        name="sample_ln_router",
    )(*args)


def _outproj_router(merged, x, w_out, ln_g, ln_b, w_router, b_router):
    tm = OUTPROJ_ROWS
    last = N_PROMPT // tm - 1
    in_rows = lambda w: pl.BlockSpec((tm, w), lambda i: (jnp.minimum(i, last), 0))
    rows = lambda w: pl.BlockSpec((tm, w), lambda i: (i, 0))
    full = lambda a: pl.BlockSpec(a.shape, lambda i: (0, 0))
    return pl.pallas_call(
        _outproj_kernel,
        out_shape=[jax.ShapeDtypeStruct((TOKENS, D_MODEL), F32),
                   jax.ShapeDtypeStruct((TOKENS * TOKEN_ROWS, LANES), F32),
                   jax.ShapeDtypeStruct((TOKENS, HEAD_DIM), I32),
                   jax.ShapeDtypeStruct((TOKENS, HEAD_DIM), F32)],
        grid=(TOKENS // tm,),
        in_specs=[in_rows(D_MODEL), in_rows(D_MODEL), full(w_out), full(ln_g), full(ln_b), full(w_router),
                  full(b_router)],
        out_specs=[rows(D_MODEL), pl.BlockSpec((tm * TOKEN_ROWS, LANES), lambda i: (i, 0)), rows(HEAD_DIM),
                   rows(HEAD_DIM)],
        compiler_params=_params("arbitrary"),
        name="outproj_ln_router",
    )(merged, x, w_out, ln_g, ln_b, w_router, b_router)


def _to_token_rows(x, ref, n):
    for j in range(TOKEN_ROWS):
        ref[pl.ds(j, n, stride=TOKEN_ROWS), :] = x[:, j * LANES:(j + 1) * LANES]


def _from_token_rows(ref, first, n):
    return jnp.concatenate([ref[pl.ds(first * TOKEN_ROWS + j, n, stride=TOKEN_ROWS), :] for j in range(TOKEN_ROWS)],
                           axis=1)


def _gather_tokens(idx_hbm, idx_smem, idx_sem, src_hbm, buf, row_sem, per_step):
    i = pl.program_id(0)
    n = pl.num_programs(0)

    def idx_copy(step):
        return pltpu.make_async_copy(idx_hbm.at[step], idx_smem.at[step % 2], idx_sem.at[step % 2])

    def issue(step):
        slot = step % 2

        def body(r, carry):
            tok = idx_smem[slot, 0, r]
            pltpu.make_async_copy(src_hbm.at[pl.ds(tok * TOKEN_ROWS, TOKEN_ROWS)],
                                  buf.at[slot, pl.ds(r * TOKEN_ROWS, TOKEN_ROWS)], row_sem.at[slot]).start()
            return carry

        lax.fori_loop(0, per_step, body, 0, unroll=8)

    @pl.when(i == 0)
    def _():
        idx_copy(0).start()
        idx_copy(0).wait()
        issue(0)

        @pl.when(n > 1)
        def _():
            idx_copy(1).start()

    @pl.when(i + 1 < n)
    def _():
        idx_copy(i + 1).wait()
        issue(i + 1)

        @pl.when(i + 2 < n)
        def _():
            idx_copy(i + 2).start()

    slot = i % 2
    pltpu.make_async_copy(src_hbm.at[pl.ds(0, per_step * TOKEN_ROWS)], buf.at[slot], row_sem.at[slot]).wait()
    return slot


def _dispatch_kernel(tok_hbm, x_hbm, o_ref, buf, idx_smem, idx_sem, row_sem):
    slot = _gather_tokens(tok_hbm, idx_smem, idx_sem, x_hbm, buf, row_sem, MOE_ROWS)
    o_ref[...] = _from_token_rows(buf.at[slot], 0, MOE_ROWS).astype(o_ref.dtype)


def _dispatch(slot_tok, x1_rows):
    return pl.pallas_call(
        _dispatch_kernel,
        out_shape=jax.ShapeDtypeStruct((MOE_SLOTS, D_MODEL), BF16),
        grid=(MOE_BLOCKS,),
        in_specs=[pl.BlockSpec(memory_space=pl.ANY), pl.BlockSpec(memory_space=pl.ANY)],
        out_specs=pl.BlockSpec((MOE_ROWS, D_MODEL), lambda i: (i, 0)),
        scratch_shapes=[pltpu.VMEM((2, MOE_ROWS * TOKEN_ROWS, LANES), F32), pltpu.SMEM((2, 1, MOE_ROWS), I32),
                        pltpu.SemaphoreType.DMA((2,)), pltpu.SemaphoreType.DMA((2,))],
        compiler_params=_params("arbitrary"),
        name="moe_dispatch",
    )(slot_tok.reshape(MOE_BLOCKS, 1, MOE_ROWS), x1_rows)


def _expert_changed(be_ref, i):
    return jnp.logical_or(i == 0, be_ref[i] != be_ref[jnp.maximum(i - 1, 0)])


def _gate_up_kernel(be_ref, nu_ref, x_ref, wg_ref, wu_ref, bg_ref, bu_ref, o_ref, wg_b, wu_b):
    i = pl.program_id(1)

    @pl.when(_expert_changed(be_ref, i))
    def _():
        wg_b[...] = wg_ref[...].astype(BF16)
        wu_b[...] = wu_ref[...].astype(BF16)

    @pl.when(i < nu_ref[0])
    def _():
        x = x_ref[...].astype(BF16)
        g = jnp.dot(x, wg_b[...], preferred_element_type=F32) + bg_ref[...]
        u = jnp.dot(x, wu_b[...], preferred_element_type=F32) + bu_ref[...]
        g = jnp.minimum(g, SWIGLU_LIMIT)
        u = jnp.clip(u, -SWIGLU_LIMIT, SWIGLU_LIMIT)
        o_ref[...] = ((u + 1.0) * (g * _sigmoid(SWIGLU_ALPHA * g))).astype(o_ref.dtype)

    @pl.when(i >= nu_ref[0])
    def _():
        o_ref[...] = jnp.zeros_like(o_ref)


def _expert_gate_up(blk_expert, n_used, xs, w_gate_up, b_gate_up):
    ff_tiles = D_FF // FF_TILE
    row_blk = lambda j, i, be, nu: jnp.minimum(i, nu[0] - 1)
    return pl.pallas_call(
        _gate_up_kernel,
        out_shape=jax.ShapeDtypeStruct((MOE_SLOTS, D_FF), BF16),
        grid_spec=pltpu.PrefetchScalarGridSpec(
            num_scalar_prefetch=2,
            grid=(ff_tiles, MOE_BLOCKS),
            in_specs=[pl.BlockSpec((MOE_ROWS, D_MODEL), lambda j, i, be, nu: (row_blk(j, i, be, nu), 0)),
                      pl.BlockSpec((None, D_MODEL, FF_TILE), lambda j, i, be, nu: (be[i], 0, j)),
                      pl.BlockSpec((None, D_MODEL, FF_TILE), lambda j, i, be, nu: (be[i], 0, ff_tiles + j)),
                      pl.BlockSpec((None, 1, FF_TILE), lambda j, i, be, nu: (be[i], 0, j)),
                      pl.BlockSpec((None, 1, FF_TILE), lambda j, i, be, nu: (be[i], 0, ff_tiles + j))],
            out_specs=pl.BlockSpec((MOE_ROWS, FF_TILE), lambda j, i, be, nu: (i, j)),
            scratch_shapes=[pltpu.VMEM((D_MODEL, FF_TILE), BF16), pltpu.VMEM((D_MODEL, FF_TILE), BF16)]),
        compiler_params=_params("arbitrary", "arbitrary"),
        name="moe_gate_up",
    )(blk_expert, n_used, xs, w_gate_up, w_gate_up, b_gate_up, b_gate_up)


def _down_kernel(be_ref, nu_ref, h_ref, w_ref, b_ref, o_ref, w_b):
    i = pl.program_id(1)

    @pl.when(_expert_changed(be_ref, i))
    def _():
        w_b[...] = w_ref[...].astype(BF16)

    @pl.when(i < nu_ref[0])
    def _():
        y = jnp.dot(h_ref[...], w_b[...], preferred_element_type=F32) + b_ref[...]
        _to_token_rows(y, o_ref, MOE_ROWS)

    @pl.when(i >= nu_ref[0])
    def _():
        o_ref[...] = jnp.zeros_like(o_ref)


def _expert_down(blk_expert, n_used, hdn, w_down, b_down):
    row_blk = lambda j, i, be, nu: jnp.minimum(i, nu[0] - 1)
    return pl.pallas_call(
        _down_kernel,
        out_shape=jax.ShapeDtypeStruct((MOE_SLOTS * TOKEN_ROWS, LANES), F32),
        grid_spec=pltpu.PrefetchScalarGridSpec(
            num_scalar_prefetch=2,
            grid=(1, MOE_BLOCKS),
            in_specs=[pl.BlockSpec((MOE_ROWS, D_FF), lambda j, i, be, nu: (row_blk(j, i, be, nu), 0)),
                      pl.BlockSpec((None, D_FF, D_MODEL), lambda j, i, be, nu: (be[i], 0, 0)),
                      pl.BlockSpec((None, 1, D_MODEL), lambda j, i, be, nu: (be[i], 0, 0))],
            out_specs=pl.BlockSpec((MOE_ROWS * TOKEN_ROWS, LANES), lambda j, i, be, nu: (i, 0)),
            scratch_shapes=[pltpu.VMEM((D_FF, D_MODEL), BF16)]),
        compiler_params=_params("arbitrary", "arbitrary"),
        name="moe_down",
    )(blk_expert, n_used, hdn, w_down, b_down)


def _combine_kernel(dest_hbm, ys_hbm, gate_ref, x1_ref, g_ref, b_ref, o_ref, buf, idx_smem, idx_sem, row_sem):
    per_step = COMBINE_TOKENS * TOP_K
    slot = _gather_tokens(dest_hbm, idx_smem, idx_sem, ys_hbm, buf, row_sem, per_step)
    gate = gate_ref[...]
    ffn = None
    for k in range(TOP_K):
        term = _from_token_rows(buf.at[slot], k * COMBINE_TOKENS, COMBINE_TOKENS) * gate[:, k:k + 1]
        ffn = term if ffn is None else ffn + term
    o_ref[...] = _layer_norm(DN_ALPHA * x1_ref[...] + ffn, g_ref[...], b_ref[...])


def _combine(dest, y_slots, gate, x1, ln_g, ln_b):
    steps = TOKENS // COMBINE_TOKENS
    per_step = COMBINE_TOKENS * TOP_K
    rows = lambda w: pl.BlockSpec((COMBINE_TOKENS, w), lambda i: (i, 0))
    full = lambda a: pl.BlockSpec(a.shape, lambda i: (0, 0))
    order = dest.reshape(steps, COMBINE_TOKENS, TOP_K).transpose(0, 2, 1).reshape(steps, 1, per_step)
    return pl.pallas_call(
        _combine_kernel,
        out_shape=jax.ShapeDtypeStruct((TOKENS, D_MODEL), F32),
        grid=(steps,),
        in_specs=[pl.BlockSpec(memory_space=pl.ANY), pl.BlockSpec(memory_space=pl.ANY),
                  rows(HEAD_DIM), rows(D_MODEL), full(ln_g), full(ln_b)],
        out_specs=rows(D_MODEL),
        scratch_shapes=[pltpu.VMEM((2, per_step * TOKEN_ROWS, LANES), F32), pltpu.SMEM((2, 1, per_step), I32),
                        pltpu.SemaphoreType.DMA((2,)), pltpu.SemaphoreType.DMA((2,))],
        compiler_params=_params("arbitrary"),
        name="moe_combine_ln",
    )(order, y_slots, gate, x1, ln_g, ln_b)


def _moe_layout(top_e):
    e_flat = top_e.reshape(-1)
    onehot = (e_flat[:, None] == jnp.arange(N_EXPERTS, dtype=I32)[None, :]).astype(I32)
    running = jnp.cumsum(onehot, axis=0)
    counts = running[-1]
    rank = jnp.sum(running * onehot, axis=1) - 1
    padded = (counts + MOE_ROWS - 1) // MOE_ROWS * MOE_ROWS
    pad_end = jnp.cumsum(padded)
    dest = (pad_end - padded)[e_flat] + rank
    tok = jnp.arange(TOKENS * TOP_K, dtype=I32) // TOP_K
    slot_tok = jnp.zeros((MOE_SLOTS,), I32).at[dest].set(tok)
    n_used = pad_end[-1] // MOE_ROWS
    blk = jnp.minimum(jnp.arange(MOE_BLOCKS, dtype=I32), n_used - 1) * MOE_ROWS
    blk_expert = jnp.minimum(jnp.searchsorted(pad_end, blk, side='right'), N_EXPERTS - 1).astype(I32)
    return dest.astype(I32), slot_tok, blk_expert, n_used.reshape(1).astype(I32)


def kernel(x_prompt, x_sample, mem_prompt, cache_kv_w128, cache_kv_w512, cache_kv_w2048, cache_mem_kv, state_conv, state_lru_h, w_in, conv_w, conv_b, w_rg_a, b_rg_a, w_rg_x, b_rg_x, lru_lambda, w_mem_kv, p_att, p_lru, p_mem, w_out, ln1_g, ln1_b, w_router, b_router, w_gate_up, b_gate_up, w_down, b_down, ln2_g, ln2_b):
    row = lambda a: a.reshape(1, -1)
    layer = lambda a: a.reshape(a.shape[1:])
    x_p = x_prompt.reshape(N_PROMPT, D_MODEL)
    x_s = x_sample.reshape(DEC_BATCH, D_MODEL)
    xb = x_p.astype(BF16)
    w_in_b = layer(w_in).astype(BF16)
    qkv_cols = 3 * ATT_WIDTH
    gate_col = qkv_cols + UG_WIDTH

    qkv = _matmul(xb, w_in_b[:, :qkv_cols], 1024, 1536, "in_proj_qkv")
    ug = _matmul(xb, w_in_b[:, qkv_cols:gate_col], 1024, 1024, "in_proj_lru_mem")
    gates = _matmul(xb, w_in_b[:, gate_col:], 1024, 1536, "in_proj_gates")

    c_p, sa_p, sb_p = _rope_tables(jnp.arange(SEQ, dtype=I32))
    y_att_p, k_rot = _prompt_attention(qkv, c_p, sa_p, sb_p)
    wa, wx = layer(w_rg_a).astype(BF16), layer(w_rg_x).astype(BF16)
    lru_args = (layer(conv_w), row(conv_b), wa, row(b_rg_a), wx, row(b_rg_x), row(lru_lambda))
    y_lru_p, h_last_p = _prompt_lru(ug, *lru_args)
    mem_kv_p = _matmul(mem_prompt.reshape(BATCH * MEM_TOKENS, D_MODEL), layer(w_mem_kv).astype(BF16), 512, 1024,
                       "mem_kv_proj")
    y_mem_p = _prompt_mem_attention(ug, mem_kv_p)

    merged = _merge(y_att_p, y_lru_p, y_mem_p, gates,
                    layer(p_att).astype(BF16), layer(p_lru).astype(BF16), layer(p_mem).astype(BF16))
    x1, x1_rows, top_e, gate = _outproj_router(merged, x_p, layer(w_out).astype(BF16), row(ln1_g), row(ln1_b),
                                               layer(w_router).astype(BF16), row(b_router))

    h_s_in = _matmul(x_s, layer(w_in), DEC_BATCH, 512, "sample_in_proj", full_precision=True)
    c_s, sa_s, sb_s = _rope_tables(jnp.full((1,), PAST_LEN, I32))
    caches = [cache.reshape(DEC_BATCH, ATT_BLOCK, -1) for cache in (cache_kv_w128, cache_kv_w512, cache_kv_w2048)]
    y_att_s, y_mem_s, k_rot_s = _sample_attention(
        h_s_in, c_s, sa_s, sb_s, caches, cache_mem_kv.reshape(DEC_BATCH, MEM_TOKENS, 2 * MEM_WIDTH))
    y_lru_s, h_s = _sample_lru(h_s_in, state_conv.reshape(DEC_BATCH, (CONV_W - 1) * LRU_WIDTH), layer(state_lru_h),
                               layer(conv_w), row(conv_b), layer(w_rg_a), row(b_rg_a), layer(w_rg_x), row(b_rg_x),
                               row(lru_lambda))
    hi_mm = lambda a, w, name: _matmul(a, layer(w), DEC_BATCH, 512, name, full_precision=True)
    merged_s = _sample_merge(h_s_in, hi_mm(y_att_s, p_att, "sample_p_att"), hi_mm(y_lru_s, p_lru, "sample_p_lru"),
                             hi_mm(y_mem_s, p_mem, "sample_p_mem"))
    mix_s = hi_mm(merged_s, w_out, "sample_out_proj")
    x1_s, top_e_s, gate_s = _sample_ln_router(mix_s, x_s, row(ln1_g), row(ln1_b), layer(w_router), row(b_router))

    pad = TOKENS - N_PROMPT - DEC_BATCH
    tail = lambda s, p: jnp.concatenate([s, p], axis=0)
    pad_e = (jnp.arange(pad, dtype=I32)[:, None] * TOP_K + jnp.arange(HEAD_DIM, dtype=I32)[None, :]) % N_EXPERTS
    x1_tail = tail(x1_s, jnp.zeros((pad, D_MODEL), F32))
    x1 = lax.dynamic_update_slice(x1, x1_tail, (N_PROMPT, 0))
    x1_rows = lax.dynamic_update_slice(x1_rows, x1_tail.reshape(-1, LANES), (N_PROMPT * TOKEN_ROWS, 0))
    top_e = lax.dynamic_update_slice(top_e, tail(top_e_s, pad_e), (N_PROMPT, 0))
    gate = lax.dynamic_update_slice(gate, tail(gate_s, jnp.zeros((pad, HEAD_DIM), F32)), (N_PROMPT, 0))

    dest, slot_tok, blk_expert, n_used = _moe_layout(top_e[:, :TOP_K])
    xs = _dispatch(slot_tok, x1_rows)
    hdn = _expert_gate_up(blk_expert, n_used, xs, layer(w_gate_up), b_gate_up.reshape(N_EXPERTS, 1, 2 * D_FF))
    y_slots = _expert_down(blk_expert, n_used, hdn, layer(w_down), b_down.reshape(N_EXPERTS, 1, D_MODEL))
    y = _combine(dest, y_slots, gate, x1, row(ln2_g), row(ln2_b))

    y_prompt = y[:N_PROMPT].reshape(BATCH, SEQ, D_MODEL)
    y_sample = y[N_PROMPT:N_PROMPT + DEC_BATCH].reshape(DEC_BATCH, 1, D_MODEL)
    p_kv, s_kv = [], []
    for g, window in enumerate(WINDOWS):
        keep = min(window, SEQ)
        k_g = k_rot[g].reshape(BATCH, SEQ, HEADS, HEAD_DIM)[:, SEQ - keep:]
        v_cols = slice(2 * ATT_WIDTH + g * ATT_OUT, 2 * ATT_WIDTH + (g + 1) * ATT_OUT)
        v_g = qkv[:, v_cols].reshape(BATCH, SEQ, HEADS, HEAD_DIM)[:, SEQ - keep:]
        p_kv.append(jnp.stack([k_g, v_g], axis=2)[None])
        ks = k_rot_s[:, g * ATT_OUT:(g + 1) * ATT_OUT].reshape(DEC_BATCH, 1, HEADS, HEAD_DIM)
        vs = h_s_in[:, v_cols].reshape(DEC_BATCH, 1, HEADS, HEAD_DIM)
        s_kv.append(jnp.stack([ks, vs], axis=2)[None])
    p_mem_kv = mem_kv_p.reshape(1, BATCH, MEM_TOKENS, 2, MEM_HEADS, MEM_HEAD_DIM)
    u_p = ug[:, :LRU_WIDTH].reshape(BATCH, SEQ, LRU_WIDTH)
    p_conv = u_p[:, SEQ - (CONV_W - 1):][None]
    p_lru_h = h_last_p.reshape(1, BATCH, LRU_WIDTH)
    u_s = h_s_in[:, qkv_cols:qkv_cols + LRU_WIDTH]
    s_conv = jnp.concatenate([layer(state_conv)[:, 1:], u_s[:, None, :]], axis=1)[None]
    s_lru_h = h_s[None]
    return (y_prompt, y_sample, p_kv[0], p_kv[1], p_kv[2], p_mem_kv, p_conv, p_lru_h,
            s_kv[0], s_kv[1], s_kv[2], s_conv, s_lru_h)
```

```python
import jax
import jax.numpy as jnp
from jax import lax
from jax.experimental import pallas as pl
from jax.experimental.pallas import tpu as pltpu

F32 = jnp.float32
BF16 = jnp.bfloat16
I32 = jnp.int32

D_MODEL = 2048
BATCH = 8
SEQ = 2048
DEC_BATCH = 32
PAST_LEN = 8192
HEAD_DIM = 128
HEADS = 4
DILATIONS = (1, 4, 16)
WINDOWS = (128, 512, 2048)
ATT_BLOCK = 128
ATT_WIDTH = 1536
ATT_OUT = 512
ROT_DIM = 32
ROPE_THETA = 500000.0
LRU_WIDTH = 1536
LRU_BLOCK = 128
LRU_C = 8.0
CONV_W = 4
MEM_TOKENS = 256
MEM_HEADS = 4
MEM_HEAD_DIM = 256
MEM_WIDTH = 1024
N_EXPERTS = 32
TOP_K = 4
D_FF = 2048
SWIGLU_LIMIT = 7.0
SWIGLU_ALPHA = 1.702
LN_EPS = 1e-5
DN_ALPHA = 2.0 ** 0.25

N_PROMPT = BATCH * SEQ
OUTPROJ_ROWS = 256
TOKENS = N_PROMPT + OUTPROJ_ROWS
UG_WIDTH = 2 * LRU_WIDTH + MEM_WIDTH

V7X_VMEM_LIMIT = 56 * 1024 * 1024

LRU_COLS = 512
LRU_ROWS = 256
SUBLANES = 8
MOE_ROWS = 256
MOE_BLOCKS = TOKENS * TOP_K // MOE_ROWS + N_EXPERTS
MOE_SLOTS = MOE_BLOCKS * MOE_ROWS
FF_TILE = 1024
COMBINE_TOKENS = 128
LANES = 128
TOKEN_ROWS = D_MODEL // LANES


def _params(*sem):
    return pltpu.CompilerParams(dimension_semantics=sem, vmem_limit_bytes=V7X_VMEM_LIMIT)


def _bdot(a, b):
    return jnp.dot(a.astype(BF16), b.astype(BF16), preferred_element_type=F32)


def _hdot(a, b):
    return jnp.dot(a, b, precision=lax.Precision.HIGHEST, preferred_element_type=F32)


def _round_bf16(x):
    return x.astype(BF16).astype(F32)


def _keep_f32(x):
    return x


def _sigmoid(x):
    return 1.0 / (1.0 + jnp.exp(-x))


def _gelu_tanh(x):
    return 0.5 * x * (1.0 + jnp.tanh(0.7978845608028654 * (x + 0.044715 * (x * x * x))))


def _layer_norm(x, g, b):
    mu = jnp.mean(x, axis=-1, keepdims=True)
    xc = x - mu
    var = jnp.mean(xc * xc, axis=-1, keepdims=True)
    return xc * lax.rsqrt(var + LN_EPS) * g + b


def _mm_kernel(x_ref, w_ref, o_ref):
    o_ref[...] = _bdot(x_ref[...], w_ref[...]).astype(o_ref.dtype)


def _mm_hi_kernel(x_ref, w_ref, o_ref):
    o_ref[...] = _hdot(x_ref[...], w_ref[...])


def _matmul(x, w, tm, tn, name, full_precision=False):
    m, k = x.shape
    n = w.shape[1]
    return pl.pallas_call(
        _mm_hi_kernel if full_precision else _mm_kernel,
        out_shape=jax.ShapeDtypeStruct((m, n), F32),
        grid=(n // tn, m // tm),
        in_specs=[pl.BlockSpec((tm, k), lambda j, i: (i, 0)),
                  pl.BlockSpec((k, tn), lambda j, i: (0, j))],
        out_specs=pl.BlockSpec((tm, tn), lambda j, i: (i, j)),
        compiler_params=_params("arbitrary", "arbitrary"),
        name=name,
    )(x, w)


def _rope_tables(pos):
    half = ROT_DIM // 2
    inv_freq = ROPE_THETA ** (-jnp.arange(half, dtype=F32) / half)
    ang = pos.astype(F32)[:, None] * inv_freq[None, :]
    cos, sin = jnp.cos(ang), jnp.sin(ang)
    t = pos.shape[0]
    rest = HEAD_DIM - ROT_DIM
    c = jnp.concatenate([cos, cos, jnp.ones((t, rest), F32)], axis=1)
    sa = jnp.concatenate([-sin, jnp.zeros((t, half + rest), F32)], axis=1)
    sb = jnp.concatenate([jnp.zeros((t, half), F32), sin, jnp.zeros((t, rest), F32)], axis=1)
    return c, sa, sb


def _rope(x, c, sa, sb):
    half = ROT_DIM // 2
    return x * c + pltpu.roll(x, HEAD_DIM - half, 1) * sa + pltpu.roll(x, half, 1) * sb


def _group_merge(lses, outs, operand=_round_bf16):
    m = jnp.maximum(jnp.maximum(lses[0], lses[1]), lses[2])
    es = [jnp.exp(l - m) for l in lses]
    den = es[0] + es[1] + es[2]
    y = None
    for e, o in zip(es, outs):
        term = operand(e / den) * operand(o)
        y = term if y is None else y + term
    return y


def _attn_kernel(q0, q1, q2, k0, k1, k2, v0, v1, v2, c_ref, sa_ref, sb_ref,
                 y_ref, kr0, kr1, kr2, qs, os_, ls):
    c, sa, sb = c_ref[...], sa_ref[...], sb_ref[...]
    scale = HEAD_DIM ** -0.5
    ii = lax.broadcasted_iota(I32, (ATT_BLOCK, ATT_BLOCK), 0)
    jj = lax.broadcasted_iota(I32, (ATT_BLOCK, ATT_BLOCK), 1)
    cur_mask = jj <= ii
    prev_mask = jj >= ii
    dn = (((1,), (1,)), ((), ()))
    for g, (q_ref, k_ref, v_ref, kr_ref) in enumerate(((q0, k0, v0, kr0), (q1, k1, v1, kr1), (q2, k2, v2, kr2))):
        dil = DILATIONS[g]
        qs[...] = _rope(q_ref[...], c, sa, sb)
        kr_ref[...] = _rope(k_ref[...], c, sa, sb)
        o_g, l_g = os_.at[g], ls.at[g]

        def rows_of(start, dil=dil):
            return pl.ds(start, ATT_BLOCK, stride=dil) if dil > 1 else pl.ds(start, ATT_BLOCK)

        for cls in range(dil):
            for b in range(SEQ // (dil * ATT_BLOCK)):
                start = cls + dil * ATT_BLOCK * b
                rows = rows_of(start)
                q = qs[rows, :].astype(BF16)
                kc = kr_ref[rows, :].astype(BF16)
                vc = v_ref[rows, :].astype(BF16)
                s_c = lax.dot_general(q, kc, dn, preferred_element_type=F32) * scale
                s_c = jnp.where(cur_mask, s_c, -jnp.inf)
                m = jnp.max(s_c, axis=-1, keepdims=True)
                if b > 0:
                    prows = rows_of(start - dil * ATT_BLOCK)
                    kp = kr_ref[prows, :].astype(BF16)
                    vp = v_ref[prows, :].astype(BF16)
                    s_p = lax.dot_general(q, kp, dn, preferred_element_type=F32) * scale
                    s_p = jnp.where(prev_mask, s_p, -jnp.inf)
                    m = jnp.maximum(m, jnp.max(s_p, axis=-1, keepdims=True))
                    p_p = jnp.exp(s_p - m)
                p_c = jnp.exp(s_c - m)
                den = jnp.sum(p_c, axis=-1, keepdims=True)
                if b > 0:
                    den = den + jnp.sum(p_p, axis=-1, keepdims=True)
                o = jnp.dot((p_c / den).astype(BF16), vc, preferred_element_type=F32)
                if b > 0:
                    o = o + jnp.dot((p_p / den).astype(BF16), vp, preferred_element_type=F32)
                lse = m + jnp.log(den)
                o_g[rows, :] = o
                l_g[rows, :] = jnp.broadcast_to(lse, (ATT_BLOCK, HEAD_DIM))
    y = _group_merge([ls[0], ls[1], ls[2]], [os_[0], os_[1], os_[2]])
    y_ref[...] = y.astype(y_ref.dtype)


def _prompt_attention(qkv, c, sa, sb):
    blk = (SEQ, HEAD_DIM)
    head_cols = ATT_WIDTH // HEAD_DIM

    def col(base, g):
        return pl.BlockSpec(blk, lambda n, h: (n, base + g * HEADS + h))

    in_specs = ([col(0, g) for g in range(3)] + [col(head_cols, g) for g in range(3)]
                + [col(2 * head_cols, g) for g in range(3)] + [pl.BlockSpec(blk, lambda n, h: (0, 0))] * 3)
    res = pl.pallas_call(
        _attn_kernel,
        out_shape=[jax.ShapeDtypeStruct((N_PROMPT, ATT_OUT), BF16)]
        + [jax.ShapeDtypeStruct((N_PROMPT, ATT_OUT), F32)] * 3,
        grid=(BATCH, HEADS),
        in_specs=in_specs,
        out_specs=[pl.BlockSpec(blk, lambda n, h: (n, h))] * 4,
        scratch_shapes=[pltpu.VMEM(blk, F32), pltpu.VMEM((3,) + blk, F32), pltpu.VMEM((3,) + blk, F32)],
        compiler_params=_params("arbitrary", "arbitrary"),
        name="prompt_attention",
    )(*([qkv] * 9), c, sa, sb)
    return res[0], res[1:]


def _lru_gates(uc, wa_ref, ba, wx_ref, bx, lam, dot=_bdot):
    n_blk = uc.shape[1] // LRU_BLOCK
    r = jnp.concatenate([dot(uc[:, j * LRU_BLOCK:(j + 1) * LRU_BLOCK], wa_ref[j]) for j in range(n_blk)], axis=1)
    i = jnp.concatenate([dot(uc[:, j * LRU_BLOCK:(j + 1) * LRU_BLOCK], wx_ref[j]) for j in range(n_blk)], axis=1)
    r = _sigmoid(r + ba)
    i = _sigmoid(i + bx)
    neg = -lam
    softplus = jnp.maximum(neg, 0.0) + jnp.log1p(jnp.exp(-jnp.abs(neg)))
    log_a = (-LRU_C * r) * softplus
    a = jnp.exp(log_a)
    b = jnp.sqrt(-jnp.tanh(log_a) * (a * a + 1.0)) * (i * uc)
    return a, b


def _lru_kernel(u_ref, g_ref, cw_ref, cb_ref, wa_ref, ba_ref, wx_ref, bx_ref, lam_ref,
                y_ref, hl_ref, hist, carry):
    @pl.when(pl.program_id(2) == 0)
    def _():
        hist[...] = jnp.zeros_like(hist)
        carry[...] = jnp.zeros_like(carry)

    u = u_ref[...]
    cw = cw_ref[...]
    ext = jnp.concatenate([hist[...], u], axis=0)
    off = SUBLANES - (CONV_W - 1)
    conv = ext[off:off + LRU_ROWS] * cw[0:1]
    for t in range(1, CONV_W):
        conv = conv + ext[off + t:off + t + LRU_ROWS] * cw[t:t + 1]
    uc = cb_ref[...] + conv
    hist[...] = u[LRU_ROWS - SUBLANES:]

    a, b = _lru_gates(uc, wa_ref, ba_ref[...], wx_ref, bx_ref[...], lam_ref[...])

    step = lax.broadcasted_iota(I32, a.shape, 0) & (SUBLANES - 1)
    for s in (1, 2, 4):
        a_s = pltpu.roll(a, s, 0)
        b_s = pltpu.roll(b, s, 0)
        valid = step >= s
        b = jnp.where(valid, a * b_s + b, b)
        a = jnp.where(valid, a * a_s, a)
    h_prev = carry[0:1, :]
    hs = []
    for j in range(LRU_ROWS // SUBLANES):
        h_j = a[j * SUBLANES:(j + 1) * SUBLANES] * h_prev + b[j * SUBLANES:(j + 1) * SUBLANES]
        h_prev = h_j[SUBLANES - 1:SUBLANES]
        hs.append(h_j)
    h = jnp.concatenate(hs, axis=0)
    carry[...] = jnp.broadcast_to(h_prev, carry.shape)
    y_ref[...] = (h * _gelu_tanh(g_ref[...])).astype(y_ref.dtype)
    hl_ref[...] = h_prev


def _prompt_lru(ug, conv_w, conv_b, wa, ba, wx, bx, lam):
    t_blocks = SEQ // LRU_ROWS
    c_blocks = LRU_WIDTH // LRU_COLS
    row = pl.BlockSpec((1, LRU_COLS), lambda n, cb, tb: (0, cb))
    gate_w = pl.BlockSpec((LRU_COLS // LRU_BLOCK, LRU_BLOCK, LRU_BLOCK), lambda n, cb, tb: (cb, 0, 0))
    return pl.pallas_call(
        _lru_kernel,
        out_shape=[jax.ShapeDtypeStruct((N_PROMPT, LRU_WIDTH), BF16),
                   jax.ShapeDtypeStruct((BATCH, 1, LRU_WIDTH), F32)],
        grid=(BATCH, c_blocks, t_blocks),
        in_specs=[pl.BlockSpec((LRU_ROWS, LRU_COLS), lambda n, cb, tb: (n * t_blocks + tb, cb)),
                  pl.BlockSpec((LRU_ROWS, LRU_COLS), lambda n, cb, tb: (n * t_blocks + tb, c_blocks + cb)),
                  pl.BlockSpec((CONV_W, LRU_COLS), lambda n, cb, tb: (0, cb)),
                  row, gate_w, row, gate_w, row, row],
        out_specs=[pl.BlockSpec((LRU_ROWS, LRU_COLS), lambda n, cb, tb: (n * t_blocks + tb, cb)),
                   pl.BlockSpec((None, 1, LRU_COLS), lambda n, cb, tb: (n, 0, cb))],
        scratch_shapes=[pltpu.VMEM((SUBLANES, LRU_COLS), F32), pltpu.VMEM((SUBLANES, LRU_COLS), F32)],
        compiler_params=_params("arbitrary", "arbitrary", "arbitrary"),
        name="prompt_lru",
    )(ug, ug, conv_w, conv_b, wa, ba, wx, bx, lam)


def _sample_lru_kernel(h_ref_in, sc_ref, h0_ref, cw_ref, cb_ref, wa_ref, ba_ref, wx_ref, bx_ref, lam_ref,
                       y_ref, h_ref):
    u = h_ref_in[:, 3 * ATT_WIDTH:3 * ATT_WIDTH + LRU_WIDTH]
    g = h_ref_in[:, 3 * ATT_WIDTH + LRU_WIDTH:3 * ATT_WIDTH + 2 * LRU_WIDTH]
    cw = cw_ref[...]
    conv = sc_ref[:, 0:LRU_WIDTH] * cw[0:1]
    for t in range(1, CONV_W - 1):
        conv = conv + sc_ref[:, t * LRU_WIDTH:(t + 1) * LRU_WIDTH] * cw[t:t + 1]
    conv = conv + u * cw[CONV_W - 1:CONV_W]
    uc = cb_ref[...] + conv
    a, b = _lru_gates(uc, wa_ref, ba_ref[...], wx_ref, bx_ref[...], lam_ref[...], dot=_hdot)
    h = b + a * h0_ref[...]
    h_ref[...] = h
    y_ref[...] = h * _gelu_tanh(g)


def _sample_lru(h_in, state_conv, h0, conv_w, conv_b, wa, ba, wx, bx, lam):
    args = (h_in, state_conv, h0, conv_w, conv_b, wa, ba, wx, bx, lam)
    full = lambda shape: pl.BlockSpec(shape, lambda i: (0,) * len(shape))
    return pl.pallas_call(
        _sample_lru_kernel,
        out_shape=[jax.ShapeDtypeStruct((DEC_BATCH, LRU_WIDTH), F32)] * 2,
        grid=(1,),
        in_specs=[full(a.shape) for a in args],
        out_specs=[full((DEC_BATCH, LRU_WIDTH))] * 2,
        compiler_params=_params("arbitrary"),
        name="sample_lru",
    )(*args)


def _mem_attn_kernel(q_ref, k_ref, v_ref, y_ref):
    dn = (((1,), (1,)), ((), ()))
    s = lax.dot_general(q_ref[...].astype(BF16), k_ref[...].astype(BF16), dn,
                        preferred_element_type=F32) * (MEM_HEAD_DIM ** -0.5)
    m = jnp.max(s, axis=-1, keepdims=True)
    p = jnp.exp(s - m)
    p = p / jnp.sum(p, axis=-1, keepdims=True)
    y_ref[...] = _bdot(p, v_ref[...]).astype(y_ref.dtype)


def _prompt_mem_attention(ug, mem_kv):
    rows = 1024
    r_blocks = SEQ // rows
    q_col = 2 * LRU_WIDTH // MEM_HEAD_DIM
    return pl.pallas_call(
        _mem_attn_kernel,
        out_shape=jax.ShapeDtypeStruct((N_PROMPT, MEM_WIDTH), BF16),
        grid=(BATCH, MEM_HEADS, r_blocks),
        in_specs=[pl.BlockSpec((rows, MEM_HEAD_DIM), lambda n, h, r: (n * r_blocks + r, q_col + h)),
                  pl.BlockSpec((MEM_TOKENS, MEM_HEAD_DIM), lambda n, h, r: (n, h)),
                  pl.BlockSpec((MEM_TOKENS, MEM_HEAD_DIM), lambda n, h, r: (n, MEM_HEADS + h))],
        out_specs=pl.BlockSpec((rows, MEM_HEAD_DIM), lambda n, h, r: (n * r_blocks + r, h)),
        compiler_params=_params("arbitrary", "arbitrary", "arbitrary"),
        name="prompt_mem_attention",
    )(ug, mem_kv, mem_kv)


def _heads_attend(q, keys, values, scale, extra=None):
    s = jnp.sum(keys * q[None], axis=-1, keepdims=True) * scale
    m = jnp.max(s, axis=0)
    if extra is not None:
        s_x = jnp.sum(extra[0] * q, axis=-1, keepdims=True) * scale
        m = jnp.maximum(m, s_x)
        p_x = jnp.exp(s_x - m)
    p = jnp.exp(s - m[None])
    den = jnp.sum(p, axis=0)
    if extra is not None:
        den = den + p_x
    o = jnp.sum((p / den[None]) * values, axis=0)
    if extra is not None:
        o = o + (p_x / den) * extra[1]
    return o, m + jnp.log(den)


def _sample_attn_kernel(h_ref, c_ref, sa_ref, sb_ref, w0_ref, w1_ref, w2_ref, mem_ref, ya_ref, ym_ref, kr_ref):
    c, sa, sb = c_ref[...], sa_ref[...], sb_ref[...]
    outs, lses = [], []
    for g, w_ref in enumerate((w0_ref, w1_ref, w2_ref)):
        q, k_new, v_new = [], [], []
        for h in range(HEADS):
            col = (g * HEADS + h) * HEAD_DIM
            q.append(_rope(h_ref[:, col:col + HEAD_DIM], c, sa, sb))
            k_new.append(_rope(h_ref[:, ATT_WIDTH + col:ATT_WIDTH + col + HEAD_DIM], c, sa, sb))
            v_new.append(h_ref[:, 2 * ATT_WIDTH + col:2 * ATT_WIDTH + col + HEAD_DIM])
            kr_ref[:, col:col + HEAD_DIM] = k_new[h]
        stack = lambda rows: jnp.concatenate(rows, axis=0)
        o, lse = _heads_attend(stack(q), w_ref[:, 0], w_ref[:, 1], HEAD_DIM ** -0.5,
                               extra=(stack(k_new), stack(v_new)))
        outs.append(o)
        lses.append(lse)
    y = _group_merge(lses, outs, operand=_keep_f32)
    for h in range(HEADS):
        ya_ref[:, h * HEAD_DIM:(h + 1) * HEAD_DIM] = y[h:h + 1]
    q_col = 3 * ATT_WIDTH + 2 * LRU_WIDTH
    qm = jnp.concatenate([h_ref[:, q_col + h * MEM_HEAD_DIM:q_col + (h + 1) * MEM_HEAD_DIM]
                          for h in range(MEM_HEADS)], axis=0)
    om, _ = _heads_attend(qm, mem_ref[:, 0], mem_ref[:, 1], MEM_HEAD_DIM ** -0.5)
    for h in range(MEM_HEADS):
        ym_ref[:, h * MEM_HEAD_DIM:(h + 1) * MEM_HEAD_DIM] = om[h:h + 1]


def _sample_attention(h_in, c, sa, sb, caches, cache_mem):
    full = lambda shape: pl.BlockSpec(shape, lambda i: (0,) * len(shape))
    row = lambda w: pl.BlockSpec((None, 1, w), lambda i: (i, 0, 0))
    window = pl.BlockSpec((None, ATT_BLOCK, None, 2, HEADS, HEAD_DIM), lambda i: (i, 0, 0, 0, 0, 0))
    widths = (ATT_OUT, MEM_WIDTH, ATT_WIDTH)
    y_att, y_mem, k_rot = pl.pallas_call(
        _sample_attn_kernel,
        out_shape=[jax.ShapeDtypeStruct((DEC_BATCH, 1, w), F32) for w in widths],
        grid=(DEC_BATCH,),
        in_specs=[row(h_in.shape[1]), full(c.shape), full(sa.shape), full(sb.shape), window, window, window,
                  pl.BlockSpec((None, MEM_TOKENS, 2, MEM_HEADS, MEM_HEAD_DIM), lambda i: (i, 0, 0, 0, 0))],
        out_specs=[row(w) for w in widths],
        compiler_params=_params("arbitrary"),
        name="sample_attention",
    )(h_in.reshape(DEC_BATCH, 1, -1), c, sa, sb, *caches, cache_mem)
    return [a.reshape(DEC_BATCH, -1) for a in (y_att, y_mem, k_rot)]


def _merge_kernel(ya_ref, yl_ref, ym_ref, ga_ref, gl_ref, gm_ref, pa_ref, pl_ref, pm_ref, o_ref):
    merged = _sigmoid(ga_ref[...]) * _bdot(ya_ref[...], pa_ref[...])
    merged = merged + _sigmoid(gl_ref[...]) * _bdot(yl_ref[...], pl_ref[...])
    merged = merged + _sigmoid(gm_ref[...]) * _bdot(ym_ref[...], pm_ref[...])
    o_ref[...] = merged.astype(o_ref.dtype)


def _merge(y_att, y_lru, y_mem, gates, p_att, p_lru, p_mem):
    tm = 256
    rows = lambda w: pl.BlockSpec((tm, w), lambda i: (i, 0))
    full = lambda a: pl.BlockSpec(a.shape, lambda i: (0, 0))
    return pl.pallas_call(
        _merge_kernel,
        out_shape=jax.ShapeDtypeStruct((N_PROMPT, D_MODEL), BF16),
        grid=(N_PROMPT // tm,),
        in_specs=[rows(ATT_OUT), rows(LRU_WIDTH), rows(MEM_WIDTH)]
        + [pl.BlockSpec((tm, D_MODEL), lambda i, b=b: (i, b)) for b in range(3)]
        + [full(p_att), full(p_lru), full(p_mem)],
        out_specs=rows(D_MODEL),
        compiler_params=_params("arbitrary"),
        name="branch_merge",
    )(y_att, y_lru, y_mem, gates, gates, gates, p_att, p_lru, p_mem)


def _sample_merge_kernel(h_ref, pa_ref, pl_ref, pm_ref, o_ref):
    gate_col = 3 * ATT_WIDTH + UG_WIDTH
    gate = lambda b: _sigmoid(h_ref[:, gate_col + b * D_MODEL:gate_col + (b + 1) * D_MODEL])
    o_ref[...] = (gate(0) * pa_ref[...] + gate(1) * pl_ref[...]) + gate(2) * pm_ref[...]


def _sample_merge(h_in, pa, pl_, pm):
    args = (h_in, pa, pl_, pm)
    full = lambda a: pl.BlockSpec(a.shape, lambda i: (0, 0))
    return pl.pallas_call(
        _sample_merge_kernel,
        out_shape=jax.ShapeDtypeStruct((DEC_BATCH, D_MODEL), F32),
        grid=(1,),
        in_specs=[full(a) for a in args],
        out_specs=pl.BlockSpec((DEC_BATCH, D_MODEL), lambda i: (0, 0)),
        compiler_params=_params("arbitrary"),
        name="sample_merge",
    )(*args)


def _ln_route(mix, x_ref, g_ref, b_ref, wr_ref, br_ref, x1_ref, e_ref, gt_ref, dot):
    x1 = _layer_norm(DN_ALPHA * x_ref[...] + mix, g_ref[...], b_ref[...])
    x1_ref[...] = x1
    logits = dot(x1, wr_ref[...]) + br_ref[...]
    lane = lax.broadcasted_iota(I32, logits.shape, 1).astype(F32)
    out_lane = lax.broadcasted_iota(I32, e_ref.shape, 1)
    top_v = []
    e_out = jnp.zeros(e_ref.shape, I32)
    for k in range(TOP_K):
        v = jnp.max(logits, axis=-1, keepdims=True)
        e = jnp.min(jnp.where(logits == v, lane, float(N_EXPERTS)), axis=-1, keepdims=True)
        logits = jnp.where(lane == e, -jnp.inf, logits)
        top_v.append(v)
        e_out = jnp.where(out_lane == k, e.astype(I32), e_out)
    ps = [jnp.exp(v - top_v[0]) for v in top_v]
    den = ps[0] + ps[1] + ps[2] + ps[3]
    g_out = jnp.zeros(gt_ref.shape, F32)
    for k in range(TOP_K):
        g_out = jnp.where(out_lane == k, ps[k] / den, g_out)
    e_ref[...] = e_out
    gt_ref[...] = g_out


def _outproj_kernel(m_ref, x_ref, w_ref, g_ref, b_ref, wr_ref, br_ref, x1_ref, x1r_ref, e_ref, gt_ref):
    @pl.when(pl.program_id(0) < N_PROMPT // OUTPROJ_ROWS)
    def _():
        mix = jnp.dot(m_ref[...], w_ref[...], preferred_element_type=F32)
        _ln_route(mix, x_ref, g_ref, b_ref, wr_ref, br_ref, x1_ref, e_ref, gt_ref, _bdot)
        _to_token_rows(x1_ref[...], x1r_ref, OUTPROJ_ROWS)

    @pl.when(pl.program_id(0) >= N_PROMPT // OUTPROJ_ROWS)
    def _():
        for ref in (x1_ref, x1r_ref, e_ref, gt_ref):
            ref[...] = jnp.zeros_like(ref)


def _sample_ln_router_kernel(mix_ref, x_ref, g_ref, b_ref, wr_ref, br_ref, x1_ref, e_ref, gt_ref):
    _ln_route(mix_ref[...], x_ref, g_ref, b_ref, wr_ref, br_ref, x1_ref, e_ref, gt_ref, _hdot)


def _sample_ln_router(mix, x, ln_g, ln_b, w_router, b_router):
    args = (mix, x, ln_g, ln_b, w_router, b_router)
    full = lambda shape: pl.BlockSpec(shape, lambda i: (0, 0))
    out_shapes = [jax.ShapeDtypeStruct((DEC_BATCH, D_MODEL), F32), jax.ShapeDtypeStruct((DEC_BATCH, HEAD_DIM), I32),
                  jax.ShapeDtypeStruct((DEC_BATCH, HEAD_DIM), F32)]
    return pl.pallas_call(
        _sample_ln_router_kernel,
        out_shape=out_shapes,
        grid=(1,),
        in_specs=[full(a.shape) for a in args],
        out_specs=[full(s.shape) for s in out_shapes],
        compiler_params=_params("arbitrary"),
        name="sample_ln_router",
    )(*args)


def _outproj_router(merged, x, w_out, ln_g, ln_b, w_router, b_router):
    tm = OUTPROJ_ROWS
    last = N_PROMPT // tm - 1
    in_rows = lambda w: pl.BlockSpec((tm, w), lambda i: (jnp.minimum(i, last), 0))
    rows = lambda w: pl.BlockSpec((tm, w), lambda i: (i, 0))
    full = lambda a: pl.BlockSpec(a.shape, lambda i: (0, 0))
    return pl.pallas_call(
        _outproj_kernel,
        out_shape=[jax.ShapeDtypeStruct((TOKENS, D_MODEL), F32),
                   jax.ShapeDtypeStruct((TOKENS * TOKEN_ROWS, LANES), F32),
                   jax.ShapeDtypeStruct((TOKENS, HEAD_DIM), I32),
                   jax.ShapeDtypeStruct((TOKENS, HEAD_DIM), F32)],
        grid=(TOKENS // tm,),
        in_specs=[in_rows(D_MODEL), in_rows(D_MODEL), full(w_out), full(ln_g), full(ln_b), full(w_router),
                  full(b_router)],
        out_specs=[rows(D_MODEL), pl.BlockSpec((tm * TOKEN_ROWS, LANES), lambda i: (i, 0)), rows(HEAD_DIM),
                   rows(HEAD_DIM)],
        compiler_params=_params("arbitrary"),
        name="outproj_ln_router",
    )(merged, x, w_out, ln_g, ln_b, w_router, b_router)


def _to_token_rows(x, ref, n):
    for j in range(TOKEN_ROWS):
        ref[pl.ds(j, n, stride=TOKEN_ROWS), :] = x[:, j * LANES:(j + 1) * LANES]


def _from_token_rows(ref, first, n):
    return jnp.concatenate([ref[pl.ds(first * TOKEN_ROWS + j, n, stride=TOKEN_ROWS), :] for j in range(TOKEN_ROWS)],
                           axis=1)


def _gather_pipeline(t, last, idx_hbm, src_hbm, bufs, idx_smem, idx_sem, row_sem, per_step, variants):
    def idx_copy(step, slot):
        return pltpu.make_async_copy(idx_hbm.at[step], idx_smem.at[slot], idx_sem.at[slot])

    def row_copy(slot, r):
        tok = idx_smem[slot, 0, r]
        return pltpu.make_async_copy(src_hbm.at[pl.ds(tok * TOKEN_ROWS, TOKEN_ROWS)],
                                     bufs[slot].at[pl.ds(r * TOKEN_ROWS, TOKEN_ROWS)], row_sem.at[slot])

    def wait_rows(slot):
        pltpu.make_async_copy(src_hbm.at[pl.ds(0, per_step * TOKEN_ROWS)], bufs[slot], row_sem.at[slot]).wait()

    @pl.when(t == 0)
    def _():
        idx_copy(0, 0).start()
        idx_copy(0, 0).wait()

        def first(r, carry):
            row_copy(0, r).start()
            return carry

        lax.fori_loop(0, per_step, first, 0)
        idx_copy(1, 1).start()

    def step(slot, body):
        wait_rows(slot)
        idx_copy(t + 1, 1 - slot).wait()
        for r in range(per_step):
            row_copy(1 - slot, r).start(priority=r % 2)
        idx_copy(t + 2, slot).start()
        body(bufs[slot])

    for slot in (0, 1):
        parity = t % 2 == slot
        for cond, body in variants:
            pl.when(parity if cond is None else jnp.logical_and(parity, cond))(
                lambda slot=slot, body=body: step(slot, body))

    @pl.when(t == last)
    def _():
        idx_copy(0, last % 2).wait()
        wait_rows(1 - last % 2)


def _expert_changed(be_ref, i):
    return jnp.logical_or(i == 0, be_ref[i] != be_ref[jnp.maximum(i - 1, 0)])


def _gate_up_kernel(be_ref, nu_ref, tok_hbm, x_hbm, wg_ref, wu_ref, bg_ref, bu_ref, o_ref,
                    wg_b, wu_b, buf0, buf1, idx_smem, idx_sem, row_sem):
    i = pl.program_id(1)
    t = pl.program_id(0) * MOE_BLOCKS + i
    last = D_FF // FF_TILE * MOE_BLOCKS - 1

    @pl.when(_expert_changed(be_ref, i))
    def _():
        wg_b[...] = wg_ref[...].astype(BF16)
        wu_b[...] = wu_ref[...].astype(BF16)

    def compute(buf):
        x = _from_token_rows(buf, 0, MOE_ROWS).astype(BF16)
        g = jnp.dot(x, wg_b[...], preferred_element_type=F32) + bg_ref[...]
        u = jnp.dot(x, wu_b[...], preferred_element_type=F32) + bu_ref[...]
        g = jnp.minimum(g, SWIGLU_LIMIT)
        u = jnp.clip(u, -SWIGLU_LIMIT, SWIGLU_LIMIT)
        o_ref[...] = ((u + 1.0) * (g * _sigmoid(SWIGLU_ALPHA * g))).astype(o_ref.dtype)

    def unused(buf):
        o_ref[...] = jnp.zeros_like(o_ref)

    used = i < nu_ref[0]
    _gather_pipeline(t, last, tok_hbm, x_hbm, (buf0, buf1), idx_smem, idx_sem, row_sem, MOE_ROWS,
                     [(used, compute), (jnp.logical_not(used), unused)])


def _expert_gate_up(blk_expert, n_used, slot_tok, x1_rows, w_gate_up, b_gate_up):
    ff_tiles = D_FF // FF_TILE
    blocks = slot_tok.reshape(MOE_BLOCKS, 1, MOE_ROWS)
    tok_steps = jnp.concatenate([blocks] * ff_tiles + [blocks[:2]], axis=0)
    return pl.pallas_call(
        _gate_up_kernel,
        out_shape=jax.ShapeDtypeStruct((MOE_SLOTS, D_FF), BF16),
        grid_spec=pltpu.PrefetchScalarGridSpec(
            num_scalar_prefetch=2,
            grid=(ff_tiles, MOE_BLOCKS),
            in_specs=[pl.BlockSpec(memory_space=pl.ANY), pl.BlockSpec(memory_space=pl.ANY),
                      pl.BlockSpec((None, D_MODEL, FF_TILE), lambda j, i, be, nu: (be[i], 0, j)),
                      pl.BlockSpec((None, D_MODEL, FF_TILE), lambda j, i, be, nu: (be[i], 0, ff_tiles + j)),
                      pl.BlockSpec((None, 1, FF_TILE), lambda j, i, be, nu: (be[i], 0, j)),
                      pl.BlockSpec((None, 1, FF_TILE), lambda j, i, be, nu: (be[i], 0, ff_tiles + j))],
            out_specs=pl.BlockSpec((MOE_ROWS, FF_TILE), lambda j, i, be, nu: (i, j)),
            scratch_shapes=[pltpu.VMEM((D_MODEL, FF_TILE), BF16), pltpu.VMEM((D_MODEL, FF_TILE), BF16),
                            pltpu.VMEM((MOE_ROWS * TOKEN_ROWS, LANES), F32),
                            pltpu.VMEM((MOE_ROWS * TOKEN_ROWS, LANES), F32),
                            pltpu.SMEM((2, 1, MOE_ROWS), I32), pltpu.SemaphoreType.DMA((2,)),
                            pltpu.SemaphoreType.DMA((2,))]),
        compiler_params=_params("arbitrary", "arbitrary"),
        name="moe_gate_up",
    )(blk_expert, n_used, tok_steps, x1_rows, w_gate_up, w_gate_up, b_gate_up, b_gate_up)


def _down_kernel(be_ref, nu_ref, h_ref, w_ref, b_ref, o_ref, w_b):
    i = pl.program_id(1)

    @pl.when(_expert_changed(be_ref, i))
    def _():
        w_b[...] = w_ref[...].astype(BF16)

    @pl.when(i < nu_ref[0])
    def _():
        y = jnp.dot(h_ref[...], w_b[...], preferred_element_type=F32) + b_ref[...]
        _to_token_rows(y, o_ref, MOE_ROWS)

    @pl.when(i >= nu_ref[0])
    def _():
        o_ref[...] = jnp.zeros_like(o_ref)


def _expert_down(blk_expert, n_used, hdn, w_down, b_down):
    row_blk = lambda j, i, be, nu: jnp.minimum(i, nu[0] - 1)
    return pl.pallas_call(
        _down_kernel,
        out_shape=jax.ShapeDtypeStruct((MOE_SLOTS * TOKEN_ROWS, LANES), F32),
        grid_spec=pltpu.PrefetchScalarGridSpec(
            num_scalar_prefetch=2,
            grid=(1, MOE_BLOCKS),
            in_specs=[pl.BlockSpec((MOE_ROWS, D_FF), lambda j, i, be, nu: (row_blk(j, i, be, nu), 0)),
                      pl.BlockSpec((None, D_FF, D_MODEL), lambda j, i, be, nu: (be[i], 0, 0)),
                      pl.BlockSpec((None, 1, D_MODEL), lambda j, i, be, nu: (be[i], 0, 0))],
            out_specs=pl.BlockSpec((MOE_ROWS * TOKEN_ROWS, LANES), lambda j, i, be, nu: (i, 0)),
            scratch_shapes=[pltpu.VMEM((D_FF, D_MODEL), BF16)]),
        compiler_params=_params("arbitrary", "arbitrary"),
        name="moe_down",
    )(blk_expert, n_used, hdn, w_down, b_down)


def _combine_kernel(dest_hbm, ys_hbm, gate_ref, x1_ref, g_ref, b_ref, o_ref, buf0, buf1, idx_smem, idx_sem,
                    row_sem):
    def compute(buf):
        gate = gate_ref[...]
        ffn = None
        for k in range(TOP_K):
            term = _from_token_rows(buf, k * COMBINE_TOKENS, COMBINE_TOKENS) * gate[:, k:k + 1]
            ffn = term if ffn is None else ffn + term
        o_ref[...] = _layer_norm(DN_ALPHA * x1_ref[...] + ffn, g_ref[...], b_ref[...])

    _gather_pipeline(pl.program_id(0), TOKENS // COMBINE_TOKENS - 1, dest_hbm, ys_hbm, (buf0, buf1), idx_smem,
                     idx_sem, row_sem, COMBINE_TOKENS * TOP_K, [(None, compute)])


def _combine(dest, y_slots, gate, x1, ln_g, ln_b):
    steps = TOKENS // COMBINE_TOKENS
    per_step = COMBINE_TOKENS * TOP_K
    rows = lambda w: pl.BlockSpec((COMBINE_TOKENS, w), lambda i: (i, 0))
    full = lambda a: pl.BlockSpec(a.shape, lambda i: (0, 0))
    order = dest.reshape(steps, COMBINE_TOKENS, TOP_K).transpose(0, 2, 1).reshape(steps, 1, per_step)
    order = jnp.concatenate([order, order[:2]], axis=0)
    return pl.pallas_call(
        _combine_kernel,
        out_shape=jax.ShapeDtypeStruct((TOKENS, D_MODEL), F32),
        grid=(steps,),
        in_specs=[pl.BlockSpec(memory_space=pl.ANY), pl.BlockSpec(memory_space=pl.ANY),
                  rows(HEAD_DIM), rows(D_MODEL), full(ln_g), full(ln_b)],
        out_specs=rows(D_MODEL),
        scratch_shapes=[pltpu.VMEM((per_step * TOKEN_ROWS, LANES), F32), pltpu.VMEM((per_step * TOKEN_ROWS, LANES), F32),
                        pltpu.SMEM((2, 1, per_step), I32), pltpu.SemaphoreType.DMA((2,)),
                        pltpu.SemaphoreType.DMA((2,))],
        compiler_params=_params("arbitrary"),
        name="moe_combine_ln",
    )(order, y_slots, gate, x1, ln_g, ln_b)


def _moe_layout(top_e):
    e_flat = top_e.reshape(-1)
    onehot = (e_flat[:, None] == jnp.arange(N_EXPERTS, dtype=I32)[None, :]).astype(I32)
    running = jnp.cumsum(onehot, axis=0)
    counts = running[-1]
    rank = jnp.sum(running * onehot, axis=1) - 1
    padded = (counts + MOE_ROWS - 1) // MOE_ROWS * MOE_ROWS
    pad_end = jnp.cumsum(padded)
    dest = (pad_end - padded)[e_flat] + rank
    tok = jnp.arange(TOKENS * TOP_K, dtype=I32) // TOP_K
    slot_tok = jnp.zeros((MOE_SLOTS,), I32).at[dest].set(tok)
    n_used = pad_end[-1] // MOE_ROWS
    blk = jnp.minimum(jnp.arange(MOE_BLOCKS, dtype=I32), n_used - 1) * MOE_ROWS
    blk_expert = jnp.minimum(jnp.searchsorted(pad_end, blk, side='right'), N_EXPERTS - 1).astype(I32)
    return dest.astype(I32), slot_tok, blk_expert, n_used.reshape(1).astype(I32)


def kernel(x_prompt, x_sample, mem_prompt, cache_kv_w128, cache_kv_w512, cache_kv_w2048, cache_mem_kv, state_conv, state_lru_h, w_in, conv_w, conv_b, w_rg_a, b_rg_a, w_rg_x, b_rg_x, lru_lambda, w_mem_kv, p_att, p_lru, p_mem, w_out, ln1_g, ln1_b, w_router, b_router, w_gate_up, b_gate_up, w_down, b_down, ln2_g, ln2_b):
    row = lambda a: a.reshape(1, -1)
    layer = lambda a: a.reshape(a.shape[1:])
    x_p = x_prompt.reshape(N_PROMPT, D_MODEL)
    x_s = x_sample.reshape(DEC_BATCH, D_MODEL)
    xb = x_p.astype(BF16)
    w_in_b = layer(w_in).astype(BF16)
    qkv_cols = 3 * ATT_WIDTH
    gate_col = qkv_cols + UG_WIDTH

    qkv = _matmul(xb, w_in_b[:, :qkv_cols], 1024, 1536, "in_proj_qkv")
    ug = _matmul(xb, w_in_b[:, qkv_cols:gate_col], 1024, 1024, "in_proj_lru_mem")
    gates = _matmul(xb, w_in_b[:, gate_col:], 1024, 1536, "in_proj_gates")

    c_p, sa_p, sb_p = _rope_tables(jnp.arange(SEQ, dtype=I32))
    y_att_p, k_rot = _prompt_attention(qkv, c_p, sa_p, sb_p)
    wa, wx = layer(w_rg_a).astype(BF16), layer(w_rg_x).astype(BF16)
    lru_args = (layer(conv_w), row(conv_b), wa, row(b_rg_a), wx, row(b_rg_x), row(lru_lambda))
    y_lru_p, h_last_p = _prompt_lru(ug, *lru_args)
    mem_kv_p = _matmul(mem_prompt.reshape(BATCH * MEM_TOKENS, D_MODEL), layer(w_mem_kv).astype(BF16), 512, 1024,
                       "mem_kv_proj")
    y_mem_p = _prompt_mem_attention(ug, mem_kv_p)

    merged = _merge(y_att_p, y_lru_p, y_mem_p, gates,
                    layer(p_att).astype(BF16), layer(p_lru).astype(BF16), layer(p_mem).astype(BF16))
    x1, x1_rows, top_e, gate = _outproj_router(merged, x_p, layer(w_out).astype(BF16), row(ln1_g), row(ln1_b),
                                               layer(w_router).astype(BF16), row(b_router))

    h_s_in = _matmul(x_s, layer(w_in), DEC_BATCH, 512, "sample_in_proj", full_precision=True)
    c_s, sa_s, sb_s = _rope_tables(jnp.full((1,), PAST_LEN, I32))
    caches = [cache.reshape(DEC_BATCH, ATT_BLOCK, dil, 2, HEADS, HEAD_DIM)
              for cache, dil in zip((cache_kv_w128, cache_kv_w512, cache_kv_w2048), DILATIONS)]
    y_att_s, y_mem_s, k_rot_s = _sample_attention(h_s_in, c_s, sa_s, sb_s, caches, layer(cache_mem_kv))
    y_lru_s, h_s = _sample_lru(h_s_in, state_conv.reshape(DEC_BATCH, (CONV_W - 1) * LRU_WIDTH), layer(state_lru_h),
                               layer(conv_w), row(conv_b), layer(w_rg_a), row(b_rg_a), layer(w_rg_x), row(b_rg_x),
                               row(lru_lambda))
    hi_mm = lambda a, w, name: _matmul(a, layer(w), DEC_BATCH, 512, name, full_precision=True)
    merged_s = _sample_merge(h_s_in, hi_mm(y_att_s, p_att, "sample_p_att"), hi_mm(y_lru_s, p_lru, "sample_p_lru"),
                             hi_mm(y_mem_s, p_mem, "sample_p_mem"))
    mix_s = hi_mm(merged_s, w_out, "sample_out_proj")
    x1_s, top_e_s, gate_s = _sample_ln_router(mix_s, x_s, row(ln1_g), row(ln1_b), layer(w_router), row(b_router))

    pad = TOKENS - N_PROMPT - DEC_BATCH
    tail = lambda s, p: jnp.concatenate([s, p], axis=0)
    pad_e = (jnp.arange(pad, dtype=I32)[:, None] * TOP_K + jnp.arange(HEAD_DIM, dtype=I32)[None, :]) % N_EXPERTS
    x1_tail = tail(x1_s, jnp.zeros((pad, D_MODEL), F32))
    x1 = lax.dynamic_update_slice(x1, x1_tail, (N_PROMPT, 0))
    x1_rows = lax.dynamic_update_slice(x1_rows, x1_tail.reshape(-1, LANES), (N_PROMPT * TOKEN_ROWS, 0))
    top_e = lax.dynamic_update_slice(top_e, tail(top_e_s, pad_e), (N_PROMPT, 0))
    gate = lax.dynamic_update_slice(gate, tail(gate_s, jnp.zeros((pad, HEAD_DIM), F32)), (N_PROMPT, 0))

    dest, slot_tok, blk_expert, n_used = _moe_layout(top_e[:, :TOP_K])
    hdn = _expert_gate_up(blk_expert, n_used, slot_tok, x1_rows, layer(w_gate_up),
                          b_gate_up.reshape(N_EXPERTS, 1, 2 * D_FF))
    y_slots = _expert_down(blk_expert, n_used, hdn, layer(w_down), b_down.reshape(N_EXPERTS, 1, D_MODEL))
    y = _combine(dest, y_slots, gate, x1, row(ln2_g), row(ln2_b))

    y_prompt = y[:N_PROMPT].reshape(BATCH, SEQ, D_MODEL)
    y_sample = y[N_PROMPT:N_PROMPT + DEC_BATCH].reshape(DEC_BATCH, 1, D_MODEL)
    p_kv, s_kv = [], []
    for g, window in enumerate(WINDOWS):
        keep = min(window, SEQ)
        k_g = k_rot[g].reshape(BATCH, SEQ, HEADS, HEAD_DIM)[:, SEQ - keep:]
        v_cols = slice(2 * ATT_WIDTH + g * ATT_OUT, 2 * ATT_WIDTH + (g + 1) * ATT_OUT)
        v_g = qkv[:, v_cols].reshape(BATCH, SEQ, HEADS, HEAD_DIM)[:, SEQ - keep:]
        p_kv.append(jnp.stack([k_g, v_g], axis=2)[None])
        ks = k_rot_s[:, g * ATT_OUT:(g + 1) * ATT_OUT].reshape(DEC_BATCH, 1, HEADS, HEAD_DIM)
        vs = h_s_in[:, v_cols].reshape(DEC_BATCH, 1, HEADS, HEAD_DIM)
        s_kv.append(jnp.stack([ks, vs], axis=2)[None])
    p_mem_kv = mem_kv_p.reshape(1, BATCH, MEM_TOKENS, 2, MEM_HEADS, MEM_HEAD_DIM)
    u_p = ug[:, :LRU_WIDTH].reshape(BATCH, SEQ, LRU_WIDTH)
    p_conv = u_p[:, SEQ - (CONV_W - 1):][None]
    p_lru_h = h_last_p.reshape(1, BATCH, LRU_WIDTH)
    u_s = h_s_in[:, qkv_cols:qkv_cols + LRU_WIDTH]
    s_conv = jnp.concatenate([layer(state_conv)[:, 1:], u_s[:, None, :]], axis=1)[None]
    s_lru_h = h_s[None]
    return (y_prompt, y_sample, p_kv[0], p_kv[1], p_kv[2], p_mem_kv, p_conv, p_lru_h,
            s_kv[0], s_kv[1], s_kv[2], s_conv, s_lru_h)
```

```python
import jax
import jax.numpy as jnp
from jax import lax
from jax.experimental import pallas as pl
from jax.experimental.pallas import tpu as pltpu

F32 = jnp.float32
BF16 = jnp.bfloat16
I32 = jnp.int32

D_MODEL = 2048
BATCH = 8
SEQ = 2048
DEC_BATCH = 32
PAST_LEN = 8192
HEAD_DIM = 128
HEADS = 4
DILATIONS = (1, 4, 16)
WINDOWS = (128, 512, 2048)
ATT_BLOCK = 128
ATT_WIDTH = 1536
ATT_OUT = 512
ROT_DIM = 32
ROPE_THETA = 500000.0
LRU_WIDTH = 1536
LRU_BLOCK = 128
LRU_C = 8.0
CONV_W = 4
MEM_TOKENS = 256
MEM_HEADS = 4
MEM_HEAD_DIM = 256
MEM_WIDTH = 1024
N_EXPERTS = 32
TOP_K = 4
D_FF = 2048
SWIGLU_LIMIT = 7.0
SWIGLU_ALPHA = 1.702
LN_EPS = 1e-5
DN_ALPHA = 2.0 ** 0.25

N_PROMPT = BATCH * SEQ
OUTPROJ_ROWS = 256
TOKENS = N_PROMPT + OUTPROJ_ROWS
UG_WIDTH = 2 * LRU_WIDTH + MEM_WIDTH

V7X_VMEM_LIMIT = 56 * 1024 * 1024

LRU_COLS = 512
LRU_ROWS = 256
SUBLANES = 8
MOE_ROWS = 256
MOE_BLOCKS = TOKENS * TOP_K // MOE_ROWS + N_EXPERTS
MOE_SLOTS = MOE_BLOCKS * MOE_ROWS
FF_TILE = 1024
COMBINE_TOKENS = 128
LANES = 128
TOKEN_ROWS = D_MODEL // LANES


def _params(*sem):
    return pltpu.CompilerParams(dimension_semantics=sem, vmem_limit_bytes=V7X_VMEM_LIMIT)


def _bdot(a, b):
    return jnp.dot(a.astype(BF16), b.astype(BF16), preferred_element_type=F32)


def _hdot(a, b):
    return jnp.dot(a, b, precision=lax.Precision.HIGHEST, preferred_element_type=F32)


def _round_bf16(x):
    return x.astype(BF16).astype(F32)


def _keep_f32(x):
    return x


def _sigmoid(x):
    return 1.0 / (1.0 + jnp.exp(-x))


def _gelu_tanh(x):
    return 0.5 * x * (1.0 + jnp.tanh(0.7978845608028654 * (x + 0.044715 * (x * x * x))))


def _layer_norm(x, g, b):
    mu = jnp.mean(x, axis=-1, keepdims=True)
    xc = x - mu
    var = jnp.mean(xc * xc, axis=-1, keepdims=True)
    return xc * lax.rsqrt(var + LN_EPS) * g + b


def _mm_kernel(x_ref, w_ref, o_ref):
    o_ref[...] = _bdot(x_ref[...], w_ref[...]).astype(o_ref.dtype)


def _mm_hi_kernel(x_ref, w_ref, o_ref):
    o_ref[...] = _hdot(x_ref[...], w_ref[...])


def _matmul(x, w, tm, tn, name, full_precision=False):
    m, k = x.shape
    n = w.shape[1]
    return pl.pallas_call(
        _mm_hi_kernel if full_precision else _mm_kernel,
        out_shape=jax.ShapeDtypeStruct((m, n), F32),
        grid=(n // tn, m // tm),
        in_specs=[pl.BlockSpec((tm, k), lambda j, i: (i, 0)),
                  pl.BlockSpec((k, tn), lambda j, i: (0, j))],
        out_specs=pl.BlockSpec((tm, tn), lambda j, i: (i, j)),
        compiler_params=_params("arbitrary", "arbitrary"),
        name=name,
    )(x, w)


def _rope_tables(pos):
    half = ROT_DIM // 2
    inv_freq = ROPE_THETA ** (-jnp.arange(half, dtype=F32) / half)
    ang = pos.astype(F32)[:, None] * inv_freq[None, :]
    cos, sin = jnp.cos(ang), jnp.sin(ang)
    t = pos.shape[0]
    rest = HEAD_DIM - ROT_DIM
    c = jnp.concatenate([cos, cos, jnp.ones((t, rest), F32)], axis=1)
    sa = jnp.concatenate([-sin, jnp.zeros((t, half + rest), F32)], axis=1)
    sb = jnp.concatenate([jnp.zeros((t, half), F32), sin, jnp.zeros((t, rest), F32)], axis=1)
    return c, sa, sb


def _rope(x, c, sa, sb):
    half = ROT_DIM // 2
    return x * c + pltpu.roll(x, HEAD_DIM - half, 1) * sa + pltpu.roll(x, half, 1) * sb


def _group_merge(lses, outs, operand=_round_bf16):
    m = jnp.maximum(jnp.maximum(lses[0], lses[1]), lses[2])
    es = [jnp.exp(l - m) for l in lses]
    den = es[0] + es[1] + es[2]
    y = None
    for e, o in zip(es, outs):
        term = operand(e / den) * operand(o)
        y = term if y is None else y + term
    return y


def _attn_kernel(q0, q1, q2, k0, k1, k2, v0, v1, v2, c_ref, sa_ref, sb_ref,
                 y_ref, kr0, kr1, kr2, qs, os_, ls):
    c, sa, sb = c_ref[...], sa_ref[...], sb_ref[...]
    scale = HEAD_DIM ** -0.5
    ii = lax.broadcasted_iota(I32, (ATT_BLOCK, ATT_BLOCK), 0)
    jj = lax.broadcasted_iota(I32, (ATT_BLOCK, ATT_BLOCK), 1)
    cur_mask = jj <= ii
    prev_mask = jj >= ii
    dn = (((1,), (1,)), ((), ()))
    for g, (q_ref, k_ref, v_ref, kr_ref) in enumerate(((q0, k0, v0, kr0), (q1, k1, v1, kr1), (q2, k2, v2, kr2))):
        dil = DILATIONS[g]
        qs[...] = _rope(q_ref[...], c, sa, sb)
        kr_ref[...] = _rope(k_ref[...], c, sa, sb)
        o_g, l_g = os_.at[g], ls.at[g]

        def rows_of(start, dil=dil):
            return pl.ds(start, ATT_BLOCK, stride=dil) if dil > 1 else pl.ds(start, ATT_BLOCK)

        for cls in range(dil):
            for b in range(SEQ // (dil * ATT_BLOCK)):
                start = cls + dil * ATT_BLOCK * b
                rows = rows_of(start)
                q = qs[rows, :].astype(BF16)
                kc = kr_ref[rows, :].astype(BF16)
                vc = v_ref[rows, :].astype(BF16)
                s_c = lax.dot_general(q, kc, dn, preferred_element_type=F32) * scale
                s_c = jnp.where(cur_mask, s_c, -jnp.inf)
                m = jnp.max(s_c, axis=-1, keepdims=True)
                if b > 0:
                    prows = rows_of(start - dil * ATT_BLOCK)
                    kp = kr_ref[prows, :].astype(BF16)
                    vp = v_ref[prows, :].astype(BF16)
                    s_p = lax.dot_general(q, kp, dn, preferred_element_type=F32) * scale
                    s_p = jnp.where(prev_mask, s_p, -jnp.inf)
                    m = jnp.maximum(m, jnp.max(s_p, axis=-1, keepdims=True))
                    p_p = jnp.exp(s_p - m)
                p_c = jnp.exp(s_c - m)
                den = jnp.sum(p_c, axis=-1, keepdims=True)
                if b > 0:
                    den = den + jnp.sum(p_p, axis=-1, keepdims=True)
                o = jnp.dot((p_c / den).astype(BF16), vc, preferred_element_type=F32)
                if b > 0:
                    o = o + jnp.dot((p_p / den).astype(BF16), vp, preferred_element_type=F32)
                lse = m + jnp.log(den)
                o_g[rows, :] = o
                l_g[rows, :] = jnp.broadcast_to(lse, (ATT_BLOCK, HEAD_DIM))
    y = _group_merge([ls[0], ls[1], ls[2]], [os_[0], os_[1], os_[2]])
    y_ref[...] = y.astype(y_ref.dtype)


def _prompt_attention(qkv, c, sa, sb):
    blk = (SEQ, HEAD_DIM)
    head_cols = ATT_WIDTH // HEAD_DIM

    def col(base, g):
        return pl.BlockSpec(blk, lambda n, h: (n, base + g * HEADS + h))

    in_specs = ([col(0, g) for g in range(3)] + [col(head_cols, g) for g in range(3)]
                + [col(2 * head_cols, g) for g in range(3)] + [pl.BlockSpec(blk, lambda n, h: (0, 0))] * 3)
    res = pl.pallas_call(
        _attn_kernel,
        out_shape=[jax.ShapeDtypeStruct((N_PROMPT, ATT_OUT), BF16)]
        + [jax.ShapeDtypeStruct((N_PROMPT, ATT_OUT), F32)] * 3,
        grid=(BATCH, HEADS),
        in_specs=in_specs,
        out_specs=[pl.BlockSpec(blk, lambda n, h: (n, h))] * 4,
        scratch_shapes=[pltpu.VMEM(blk, F32), pltpu.VMEM((3,) + blk, F32), pltpu.VMEM((3,) + blk, F32)],
        compiler_params=_params("arbitrary", "arbitrary"),
        name="prompt_attention",
    )(*([qkv] * 9), c, sa, sb)
    return res[0], res[1:]


KV_ROWS = 256


def _kv_cache_kernel(k0, k1, k2, v0, v1, v2, o0, o1, o2):
    tb = pl.program_id(1)

    def put(o_ref, k, v):
        for h in range(HEADS):
            o_ref[:, 0, h, :] = k[:, h * HEAD_DIM:(h + 1) * HEAD_DIM]
            o_ref[:, 1, h, :] = v[:, h * HEAD_DIM:(h + 1) * HEAD_DIM]

    for (k_ref, v_ref, o_ref), window in zip(((k0, v0, o0), (k1, v1, o1), (k2, v2, o2)), WINDOWS):
        keep = min(window, SEQ)
        if keep >= KV_ROWS:
            first = (SEQ - keep) // KV_ROWS
            pl.when(tb >= first)(lambda k_ref=k_ref, v_ref=v_ref, o_ref=o_ref: put(o_ref, k_ref[...], v_ref[...]))
        else:
            pl.when(tb == SEQ // KV_ROWS - 1)(
                lambda k_ref=k_ref, v_ref=v_ref, o_ref=o_ref, keep=keep:
                put(o_ref, k_ref[KV_ROWS - keep:, :], v_ref[KV_ROWS - keep:, :]))


def _prompt_kv_caches(k_rot, qkv):
    steps = SEQ // KV_ROWS
    v_col = 2 * ATT_WIDTH // ATT_OUT
    out_shapes, out_specs = [], []
    for window in WINDOWS:
        keep = min(window, SEQ)
        rows = min(keep, KV_ROWS)
        first = (SEQ - keep) // KV_ROWS
        out_shapes.append(jax.ShapeDtypeStruct((1, BATCH, keep, 2, HEADS, HEAD_DIM), F32))
        out_specs.append(pl.BlockSpec((None, None, rows, 2, HEADS, HEAD_DIM),
                                      lambda n, tb, first=first: (0, n, jnp.maximum(tb - first, 0), 0, 0, 0)))
    return pl.pallas_call(
        _kv_cache_kernel,
        out_shape=out_shapes,
        grid=(BATCH, steps),
        in_specs=[pl.BlockSpec((KV_ROWS, ATT_OUT), lambda n, tb: (n * steps + tb, 0))] * 3
        + [pl.BlockSpec((KV_ROWS, ATT_OUT), lambda n, tb, g=g: (n * steps + tb, v_col + g)) for g in range(3)],
        out_specs=out_specs,
        compiler_params=_params("arbitrary", "arbitrary"),
        name="prompt_kv_caches",
    )(*k_rot, qkv, qkv, qkv)


def _lru_gates(uc, wa_ref, ba, wx_ref, bx, lam, dot=_bdot):
    n_blk = uc.shape[1] // LRU_BLOCK
    r = jnp.concatenate([dot(uc[:, j * LRU_BLOCK:(j + 1) * LRU_BLOCK], wa_ref[j]) for j in range(n_blk)], axis=1)
    i = jnp.concatenate([dot(uc[:, j * LRU_BLOCK:(j + 1) * LRU_BLOCK], wx_ref[j]) for j in range(n_blk)], axis=1)
    r = _sigmoid(r + ba)
    i = _sigmoid(i + bx)
    neg = -lam
    softplus = jnp.maximum(neg, 0.0) + jnp.log1p(jnp.exp(-jnp.abs(neg)))
    log_a = (-LRU_C * r) * softplus
    a = jnp.exp(log_a)
    b = jnp.sqrt(-jnp.tanh(log_a) * (a * a + 1.0)) * (i * uc)
    return a, b


def _lru_kernel(u_ref, g_ref, cw_ref, cb_ref, wa_ref, ba_ref, wx_ref, bx_ref, lam_ref,
                y_ref, hl_ref, hist, carry):
    @pl.when(pl.program_id(2) == 0)
    def _():
        hist[...] = jnp.zeros_like(hist)
        carry[...] = jnp.zeros_like(carry)

    u = u_ref[...]
    cw = cw_ref[...]
    ext = jnp.concatenate([hist[...], u], axis=0)
    off = SUBLANES - (CONV_W - 1)
    conv = ext[off:off + LRU_ROWS] * cw[0:1]
    for t in range(1, CONV_W):
        conv = conv + ext[off + t:off + t + LRU_ROWS] * cw[t:t + 1]
    uc = cb_ref[...] + conv
    hist[...] = u[LRU_ROWS - SUBLANES:]

    a, b = _lru_gates(uc, wa_ref, ba_ref[...], wx_ref, bx_ref[...], lam_ref[...])

    step = lax.broadcasted_iota(I32, a.shape, 0) & (SUBLANES - 1)
    for s in (1, 2, 4):
        a_s = pltpu.roll(a, s, 0)
        b_s = pltpu.roll(b, s, 0)
        valid = step >= s
        b = jnp.where(valid, a * b_s + b, b)
        a = jnp.where(valid, a * a_s, a)
    h_prev = carry[0:1, :]
    hs = []
    for j in range(LRU_ROWS // SUBLANES):
        h_j = a[j * SUBLANES:(j + 1) * SUBLANES] * h_prev + b[j * SUBLANES:(j + 1) * SUBLANES]
        h_prev = h_j[SUBLANES - 1:SUBLANES]
        hs.append(h_j)
    h = jnp.concatenate(hs, axis=0)
    carry[...] = jnp.broadcast_to(h_prev, carry.shape)
    y_ref[...] = (h * _gelu_tanh(g_ref[...])).astype(y_ref.dtype)
    hl_ref[...] = h_prev


def _prompt_lru(ug, conv_w, conv_b, wa, ba, wx, bx, lam):
    t_blocks = SEQ // LRU_ROWS
    c_blocks = LRU_WIDTH // LRU_COLS
    row = pl.BlockSpec((1, LRU_COLS), lambda n, cb, tb: (0, cb))
    gate_w = pl.BlockSpec((LRU_COLS // LRU_BLOCK, LRU_BLOCK, LRU_BLOCK), lambda n, cb, tb: (cb, 0, 0))
    return pl.pallas_call(
        _lru_kernel,
        out_shape=[jax.ShapeDtypeStruct((N_PROMPT, LRU_WIDTH), BF16),
                   jax.ShapeDtypeStruct((BATCH, 1, LRU_WIDTH), F32)],
        grid=(BATCH, c_blocks, t_blocks),
        in_specs=[pl.BlockSpec((LRU_ROWS, LRU_COLS), lambda n, cb, tb: (n * t_blocks + tb, cb)),
                  pl.BlockSpec((LRU_ROWS, LRU_COLS), lambda n, cb, tb: (n * t_blocks + tb, c_blocks + cb)),
                  pl.BlockSpec((CONV_W, LRU_COLS), lambda n, cb, tb: (0, cb)),
                  row, gate_w, row, gate_w, row, row],
        out_specs=[pl.BlockSpec((LRU_ROWS, LRU_COLS), lambda n, cb, tb: (n * t_blocks + tb, cb)),
                   pl.BlockSpec((None, 1, LRU_COLS), lambda n, cb, tb: (n, 0, cb))],
        scratch_shapes=[pltpu.VMEM((SUBLANES, LRU_COLS), F32), pltpu.VMEM((SUBLANES, LRU_COLS), F32)],
        compiler_params=_params("arbitrary", "arbitrary", "arbitrary"),
        name="prompt_lru",
    )(ug, ug, conv_w, conv_b, wa, ba, wx, bx, lam)


def _sample_lru_kernel(h_ref_in, sc_ref, h0_ref, cw_ref, cb_ref, wa_ref, ba_ref, wx_ref, bx_ref, lam_ref,
                       y_ref, h_ref):
    u = h_ref_in[:, 3 * ATT_WIDTH:3 * ATT_WIDTH + LRU_WIDTH]
    g = h_ref_in[:, 3 * ATT_WIDTH + LRU_WIDTH:3 * ATT_WIDTH + 2 * LRU_WIDTH]
    cw = cw_ref[...]
    conv = sc_ref[:, 0:LRU_WIDTH] * cw[0:1]
    for t in range(1, CONV_W - 1):
        conv = conv + sc_ref[:, t * LRU_WIDTH:(t + 1) * LRU_WIDTH] * cw[t:t + 1]
    conv = conv + u * cw[CONV_W - 1:CONV_W]
    uc = cb_ref[...] + conv
    a, b = _lru_gates(uc, wa_ref, ba_ref[...], wx_ref, bx_ref[...], lam_ref[...], dot=_hdot)
    h = b + a * h0_ref[...]
    h_ref[...] = h
    y_ref[...] = h * _gelu_tanh(g)


def _sample_lru(h_in, state_conv, h0, conv_w, conv_b, wa, ba, wx, bx, lam):
    args = (h_in, state_conv, h0, conv_w, conv_b, wa, ba, wx, bx, lam)
    full = lambda shape: pl.BlockSpec(shape, lambda i: (0,) * len(shape))
    return pl.pallas_call(
        _sample_lru_kernel,
        out_shape=[jax.ShapeDtypeStruct((DEC_BATCH, LRU_WIDTH), F32)] * 2,
        grid=(1,),
        in_specs=[full(a.shape) for a in args],
        out_specs=[full((DEC_BATCH, LRU_WIDTH))] * 2,
        compiler_params=_params("arbitrary"),
        name="sample_lru",
    )(*args)


def _mem_attn_kernel(q_ref, k_ref, v_ref, y_ref):
    dn = (((1,), (1,)), ((), ()))
    s = lax.dot_general(q_ref[...].astype(BF16), k_ref[...].astype(BF16), dn,
                        preferred_element_type=F32) * (MEM_HEAD_DIM ** -0.5)
    m = jnp.max(s, axis=-1, keepdims=True)
    p = jnp.exp(s - m)
    p = p / jnp.sum(p, axis=-1, keepdims=True)
    y_ref[...] = _bdot(p, v_ref[...]).astype(y_ref.dtype)


def _prompt_mem_attention(ug, mem_kv):
    rows = 1024
    r_blocks = SEQ // rows
    q_col = 2 * LRU_WIDTH // MEM_HEAD_DIM
    return pl.pallas_call(
        _mem_attn_kernel,
        out_shape=jax.ShapeDtypeStruct((N_PROMPT, MEM_WIDTH), BF16),
        grid=(BATCH, MEM_HEADS, r_blocks),
        in_specs=[pl.BlockSpec((rows, MEM_HEAD_DIM), lambda n, h, r: (n * r_blocks + r, q_col + h)),
                  pl.BlockSpec((MEM_TOKENS, MEM_HEAD_DIM), lambda n, h, r: (n, h)),
                  pl.BlockSpec((MEM_TOKENS, MEM_HEAD_DIM), lambda n, h, r: (n, MEM_HEADS + h))],
        out_specs=pl.BlockSpec((rows, MEM_HEAD_DIM), lambda n, h, r: (n * r_blocks + r, h)),
        compiler_params=_params("arbitrary", "arbitrary", "arbitrary"),
        name="prompt_mem_attention",
    )(ug, mem_kv, mem_kv)


def _heads_attend(q, keys, values, scale, extra=None):
    s = jnp.sum(keys * q[None], axis=-1, keepdims=True) * scale
    m = jnp.max(s, axis=0)
    if extra is not None:
        s_x = jnp.sum(extra[0] * q, axis=-1, keepdims=True) * scale
        m = jnp.maximum(m, s_x)
        p_x = jnp.exp(s_x - m)
    p = jnp.exp(s - m[None])
    den = jnp.sum(p, axis=0)
    if extra is not None:
        den = den + p_x
    o = jnp.sum((p / den[None]) * values, axis=0)
    if extra is not None:
        o = o + (p_x / den) * extra[1]
    return o, m + jnp.log(den)


def _sample_attn_kernel(h_ref, c_ref, sa_ref, sb_ref, w0_ref, w1_ref, w2_ref, mem_ref, ya_ref, ym_ref, kr_ref):
    c, sa, sb = c_ref[...], sa_ref[...], sb_ref[...]
    outs, lses = [], []
    for g, w_ref in enumerate((w0_ref, w1_ref, w2_ref)):
        q, k_new, v_new = [], [], []
        for h in range(HEADS):
            col = (g * HEADS + h) * HEAD_DIM
            q.append(_rope(h_ref[:, col:col + HEAD_DIM], c, sa, sb))
            k_new.append(_rope(h_ref[:, ATT_WIDTH + col:ATT_WIDTH + col + HEAD_DIM], c, sa, sb))
            v_new.append(h_ref[:, 2 * ATT_WIDTH + col:2 * ATT_WIDTH + col + HEAD_DIM])
            kr_ref[:, col:col + HEAD_DIM] = k_new[h]
        stack = lambda rows: jnp.concatenate(rows, axis=0)
        o, lse = _heads_attend(stack(q), w_ref[:, 0], w_ref[:, 1], HEAD_DIM ** -0.5,
                               extra=(stack(k_new), stack(v_new)))
        outs.append(o)
        lses.append(lse)
    y = _group_merge(lses, outs, operand=_keep_f32)
    for h in range(HEADS):
        ya_ref[:, h * HEAD_DIM:(h + 1) * HEAD_DIM] = y[h:h + 1]
    q_col = 3 * ATT_WIDTH + 2 * LRU_WIDTH
    qm = jnp.concatenate([h_ref[:, q_col + h * MEM_HEAD_DIM:q_col + (h + 1) * MEM_HEAD_DIM]
                          for h in range(MEM_HEADS)], axis=0)
    om, _ = _heads_attend(qm, mem_ref[:, 0], mem_ref[:, 1], MEM_HEAD_DIM ** -0.5)
    for h in range(MEM_HEADS):
        ym_ref[:, h * MEM_HEAD_DIM:(h + 1) * MEM_HEAD_DIM] = om[h:h + 1]


def _sample_attention(h_in, c, sa, sb, caches, cache_mem):
    full = lambda shape: pl.BlockSpec(shape, lambda i: (0,) * len(shape))
    row = lambda w: pl.BlockSpec((None, 1, w), lambda i: (i, 0, 0))
    window = pl.BlockSpec((None, ATT_BLOCK, None, 2, HEADS, HEAD_DIM), lambda i: (i, 0, 0, 0, 0, 0))
    widths = (ATT_OUT, MEM_WIDTH, ATT_WIDTH)
    y_att, y_mem, k_rot = pl.pallas_call(
        _sample_attn_kernel,
        out_shape=[jax.ShapeDtypeStruct((DEC_BATCH, 1, w), F32) for w in widths],
        grid=(DEC_BATCH,),
        in_specs=[row(h_in.shape[1]), full(c.shape), full(sa.shape), full(sb.shape), window, window, window,
                  pl.BlockSpec((None, MEM_TOKENS, 2, MEM_HEADS, MEM_HEAD_DIM), lambda i: (i, 0, 0, 0, 0))],
        out_specs=[row(w) for w in widths],
        compiler_params=_params("arbitrary"),
        name="sample_attention",
    )(h_in.reshape(DEC_BATCH, 1, -1), c, sa, sb, *caches, cache_mem)
    return [a.reshape(DEC_BATCH, -1) for a in (y_att, y_mem, k_rot)]


def _merge_kernel(ya_ref, yl_ref, ym_ref, ga_ref, gl_ref, gm_ref, pa_ref, pl_ref, pm_ref, o_ref):
    merged = _sigmoid(ga_ref[...]) * _bdot(ya_ref[...], pa_ref[...])
    merged = merged + _sigmoid(gl_ref[...]) * _bdot(yl_ref[...], pl_ref[...])
    merged = merged + _sigmoid(gm_ref[...]) * _bdot(ym_ref[...], pm_ref[...])
    o_ref[...] = merged.astype(o_ref.dtype)


def _merge(y_att, y_lru, y_mem, gates, p_att, p_lru, p_mem):
    tm = 256
    rows = lambda w: pl.BlockSpec((tm, w), lambda i: (i, 0))
    full = lambda a: pl.BlockSpec(a.shape, lambda i: (0, 0))
    return pl.pallas_call(
        _merge_kernel,
        out_shape=jax.ShapeDtypeStruct((N_PROMPT, D_MODEL), BF16),
        grid=(N_PROMPT // tm,),
        in_specs=[rows(ATT_OUT), rows(LRU_WIDTH), rows(MEM_WIDTH)]
        + [pl.BlockSpec((tm, D_MODEL), lambda i, b=b: (i, b)) for b in range(3)]
        + [full(p_att), full(p_lru), full(p_mem)],
        out_specs=rows(D_MODEL),
        compiler_params=_params("arbitrary"),
        name="branch_merge",
    )(y_att, y_lru, y_mem, gates, gates, gates, p_att, p_lru, p_mem)


def _sample_merge_kernel(h_ref, pa_ref, pl_ref, pm_ref, o_ref):
    gate_col = 3 * ATT_WIDTH + UG_WIDTH
    gate = lambda b: _sigmoid(h_ref[:, gate_col + b * D_MODEL:gate_col + (b + 1) * D_MODEL])
    o_ref[...] = (gate(0) * pa_ref[...] + gate(1) * pl_ref[...]) + gate(2) * pm_ref[...]


def _sample_merge(h_in, pa, pl_, pm):
    args = (h_in, pa, pl_, pm)
    full = lambda a: pl.BlockSpec(a.shape, lambda i: (0, 0))
    return pl.pallas_call(
        _sample_merge_kernel,
        out_shape=jax.ShapeDtypeStruct((DEC_BATCH, D_MODEL), F32),
        grid=(1,),
        in_specs=[full(a) for a in args],
        out_specs=pl.BlockSpec((DEC_BATCH, D_MODEL), lambda i: (0, 0)),
        compiler_params=_params("arbitrary"),
        name="sample_merge",
    )(*args)


def _ln_route(mix, x_ref, g_ref, b_ref, wr_ref, br_ref, x1_ref, e_ref, gt_ref, dot):
    x1 = _layer_norm(DN_ALPHA * x_ref[...] + mix, g_ref[...], b_ref[...])
    x1_ref[...] = x1
    logits = dot(x1, wr_ref[...]) + br_ref[...]
    lane = lax.broadcasted_iota(I32, logits.shape, 1).astype(F32)
    out_lane = lax.broadcasted_iota(I32, e_ref.shape, 1)
    top_v = []
    e_out = jnp.zeros(e_ref.shape, I32)
    for k in range(TOP_K):
        v = jnp.max(logits, axis=-1, keepdims=True)
        e = jnp.min(jnp.where(logits == v, lane, float(N_EXPERTS)), axis=-1, keepdims=True)
        logits = jnp.where(lane == e, -jnp.inf, logits)
        top_v.append(v)
        e_out = jnp.where(out_lane == k, e.astype(I32), e_out)
    ps = [jnp.exp(v - top_v[0]) for v in top_v]
    den = ps[0] + ps[1] + ps[2] + ps[3]
    g_out = jnp.zeros(gt_ref.shape, F32)
    for k in range(TOP_K):
        g_out = jnp.where(out_lane == k, ps[k] / den, g_out)
    e_ref[...] = e_out
    gt_ref[...] = g_out


def _outproj_kernel(m_ref, x_ref, w_ref, g_ref, b_ref, wr_ref, br_ref, x1_ref, x1r_ref, e_ref, gt_ref):
    @pl.when(pl.program_id(0) < N_PROMPT // OUTPROJ_ROWS)
    def _():
        mix = jnp.dot(m_ref[...], w_ref[...], preferred_element_type=F32)
        _ln_route(mix, x_ref, g_ref, b_ref, wr_ref, br_ref, x1_ref, e_ref, gt_ref, _bdot)
        _to_token_rows(x1_ref[...], x1r_ref, OUTPROJ_ROWS)

    @pl.when(pl.program_id(0) >= N_PROMPT // OUTPROJ_ROWS)
    def _():
        for ref in (x1_ref, x1r_ref, e_ref, gt_ref):
            ref[...] = jnp.zeros_like(ref)


def _sample_ln_router_kernel(mix_ref, x_ref, g_ref, b_ref, wr_ref, br_ref, x1_ref, e_ref, gt_ref):
    _ln_route(mix_ref[...], x_ref, g_ref, b_ref, wr_ref, br_ref, x1_ref, e_ref, gt_ref, _hdot)


def _sample_ln_router(mix, x, ln_g, ln_b, w_router, b_router):
    args = (mix, x, ln_g, ln_b, w_router, b_router)
    full = lambda shape: pl.BlockSpec(shape, lambda i: (0, 0))
    out_shapes = [jax.ShapeDtypeStruct((DEC_BATCH, D_MODEL), F32), jax.ShapeDtypeStruct((DEC_BATCH, HEAD_DIM), I32),
                  jax.ShapeDtypeStruct((DEC_BATCH, HEAD_DIM), F32)]
    return pl.pallas_call(
        _sample_ln_router_kernel,
        out_shape=out_shapes,
        grid=(1,),
        in_specs=[full(a.shape) for a in args],
        out_specs=[full(s.shape) for s in out_shapes],
        compiler_params=_params("arbitrary"),
        name="sample_ln_router",
    )(*args)


def _outproj_router(merged, x, w_out, ln_g, ln_b, w_router, b_router):
    tm = OUTPROJ_ROWS
    last = N_PROMPT // tm - 1
    in_rows = lambda w: pl.BlockSpec((tm, w), lambda i: (jnp.minimum(i, last), 0))
    rows = lambda w: pl.BlockSpec((tm, w), lambda i: (i, 0))
    full = lambda a: pl.BlockSpec(a.shape, lambda i: (0, 0))
    return pl.pallas_call(
        _outproj_kernel,
        out_shape=[jax.ShapeDtypeStruct((TOKENS, D_MODEL), F32),
                   jax.ShapeDtypeStruct((TOKENS * TOKEN_ROWS, LANES), F32),
                   jax.ShapeDtypeStruct((TOKENS, HEAD_DIM), I32),
                   jax.ShapeDtypeStruct((TOKENS, HEAD_DIM), F32)],
        grid=(TOKENS // tm,),
        in_specs=[in_rows(D_MODEL), in_rows(D_MODEL), full(w_out), full(ln_g), full(ln_b), full(w_router),
                  full(b_router)],
        out_specs=[rows(D_MODEL), pl.BlockSpec((tm * TOKEN_ROWS, LANES), lambda i: (i, 0)), rows(HEAD_DIM),
                   rows(HEAD_DIM)],
        compiler_params=_params("arbitrary"),
        name="outproj_ln_router",
    )(merged, x, w_out, ln_g, ln_b, w_router, b_router)


def _to_token_rows(x, ref, n):
    for j in range(TOKEN_ROWS):
        ref[pl.ds(j, n, stride=TOKEN_ROWS), :] = x[:, j * LANES:(j + 1) * LANES]


def _from_token_rows(ref, first, n):
    return jnp.concatenate([ref[pl.ds(first * TOKEN_ROWS + j, n, stride=TOKEN_ROWS), :] for j in range(TOKEN_ROWS)],
                           axis=1)


def _gather_pipeline(t, last, idx_hbm, src_hbm, bufs, idx_smem, idx_sem, row_sem, per_step, variants):
    def idx_copy(step, slot):
        return pltpu.make_async_copy(idx_hbm.at[step], idx_smem.at[slot], idx_sem.at[slot])

    def row_copy(slot, r):
        tok = idx_smem[slot, 0, r]
        return pltpu.make_async_copy(src_hbm.at[pl.ds(tok * TOKEN_ROWS, TOKEN_ROWS)],
                                     bufs[slot].at[pl.ds(r * TOKEN_ROWS, TOKEN_ROWS)], row_sem.at[slot])

    def wait_rows(slot):
        pltpu.make_async_copy(src_hbm.at[pl.ds(0, per_step * TOKEN_ROWS)], bufs[slot], row_sem.at[slot]).wait()

    @pl.when(t == 0)
    def _():
        idx_copy(0, 0).start()
        idx_copy(0, 0).wait()

        def first(r, carry):
            row_copy(0, r).start()
            return carry

        lax.fori_loop(0, per_step, first, 0)
        idx_copy(1, 1).start()

    def step(slot, body):
        idx_copy(t + 1, 1 - slot).wait()
        for r in range(per_step):
            row_copy(1 - slot, r).start()
        idx_copy(t + 2, slot).start()
        wait_rows(slot)
        body(bufs[slot])

    for slot in (0, 1):
        parity = t % 2 == slot
        for cond, body in variants:
            pl.when(parity if cond is None else jnp.logical_and(parity, cond))(
                lambda slot=slot, body=body: step(slot, body))

    @pl.when(t == last)
    def _():
        idx_copy(0, last % 2).wait()
        wait_rows(1 - last % 2)


def _expert_changed(be_ref, i):
    return jnp.logical_or(i == 0, be_ref[i] != be_ref[jnp.maximum(i - 1, 0)])


def _gate_up_kernel(be_ref, nu_ref, tok_hbm, x_hbm, wg_ref, wu_ref, bg_ref, bu_ref, o_ref,
                    wg_b, wu_b, buf0, buf1, idx_smem, idx_sem, row_sem):
    i = pl.program_id(1)
    t = pl.program_id(0) * MOE_BLOCKS + i
    last = D_FF // FF_TILE * MOE_BLOCKS - 1

    @pl.when(_expert_changed(be_ref, i))
    def _():
        wg_b[...] = wg_ref[...].astype(BF16)
        wu_b[...] = wu_ref[...].astype(BF16)

    def compute(buf):
        x = _from_token_rows(buf, 0, MOE_ROWS).astype(BF16)
        g = jnp.dot(x, wg_b[...], preferred_element_type=F32) + bg_ref[...]
        u = jnp.dot(x, wu_b[...], preferred_element_type=F32) + bu_ref[...]
        g = jnp.minimum(g, SWIGLU_LIMIT)
        u = jnp.clip(u, -SWIGLU_LIMIT, SWIGLU_LIMIT)
        o_ref[...] = ((u + 1.0) * (g * _sigmoid(SWIGLU_ALPHA * g))).astype(o_ref.dtype)

    def unused(buf):
        o_ref[...] = jnp.zeros_like(o_ref)

    used = i < nu_ref[0]
    _gather_pipeline(t, last, tok_hbm, x_hbm, (buf0, buf1), idx_smem, idx_sem, row_sem, MOE_ROWS,
                     [(used, compute), (jnp.logical_not(used), unused)])


def _expert_gate_up(blk_expert, n_used, slot_tok, x1_rows, w_gate_up, b_gate_up):
    ff_tiles = D_FF // FF_TILE
    blocks = slot_tok.reshape(MOE_BLOCKS, 1, MOE_ROWS)
    tok_steps = jnp.concatenate([blocks] * ff_tiles + [blocks[:2]], axis=0)
    return pl.pallas_call(
        _gate_up_kernel,
        out_shape=jax.ShapeDtypeStruct((MOE_SLOTS, D_FF), BF16),
        grid_spec=pltpu.PrefetchScalarGridSpec(
            num_scalar_prefetch=2,
            grid=(ff_tiles, MOE_BLOCKS),
            in_specs=[pl.BlockSpec(memory_space=pl.ANY), pl.BlockSpec(memory_space=pl.ANY),
                      pl.BlockSpec((None, D_MODEL, FF_TILE), lambda j, i, be, nu: (be[i], 0, j)),
                      pl.BlockSpec((None, D_MODEL, FF_TILE), lambda j, i, be, nu: (be[i], 0, ff_tiles + j)),
                      pl.BlockSpec((None, 1, FF_TILE), lambda j, i, be, nu: (be[i], 0, j)),
                      pl.BlockSpec((None, 1, FF_TILE), lambda j, i, be, nu: (be[i], 0, ff_tiles + j))],
            out_specs=pl.BlockSpec((MOE_ROWS, FF_TILE), lambda j, i, be, nu: (i, j)),
            scratch_shapes=[pltpu.VMEM((D_MODEL, FF_TILE), BF16), pltpu.VMEM((D_MODEL, FF_TILE), BF16),
                            pltpu.VMEM((MOE_ROWS * TOKEN_ROWS, LANES), F32),
                            pltpu.VMEM((MOE_ROWS * TOKEN_ROWS, LANES), F32),
                            pltpu.SMEM((2, 1, MOE_ROWS), I32), pltpu.SemaphoreType.DMA((2,)),
                            pltpu.SemaphoreType.DMA((2,))]),
        compiler_params=_params("arbitrary", "arbitrary"),
        name="moe_gate_up",
    )(blk_expert, n_used, tok_steps, x1_rows, w_gate_up, w_gate_up, b_gate_up, b_gate_up)


def _down_kernel(be_ref, nu_ref, h_ref, w_ref, b_ref, o_ref, w_b):
    i = pl.program_id(1)

    @pl.when(_expert_changed(be_ref, i))
    def _():
        w_b[...] = w_ref[...].astype(BF16)

    @pl.when(i < nu_ref[0])
    def _():
        y = jnp.dot(h_ref[...], w_b[...], preferred_element_type=F32) + b_ref[...]
        _to_token_rows(y, o_ref, MOE_ROWS)

    @pl.when(i >= nu_ref[0])
    def _():
        o_ref[...] = jnp.zeros_like(o_ref)


def _expert_down(blk_expert, n_used, hdn, w_down, b_down):
    row_blk = lambda j, i, be, nu: jnp.minimum(i, nu[0] - 1)
    return pl.pallas_call(
        _down_kernel,
        out_shape=jax.ShapeDtypeStruct((MOE_SLOTS * TOKEN_ROWS, LANES), F32),
        grid_spec=pltpu.PrefetchScalarGridSpec(
            num_scalar_prefetch=2,
            grid=(1, MOE_BLOCKS),
            in_specs=[pl.BlockSpec((MOE_ROWS, D_FF), lambda j, i, be, nu: (row_blk(j, i, be, nu), 0)),
                      pl.BlockSpec((None, D_FF, D_MODEL), lambda j, i, be, nu: (be[i], 0, 0)),
                      pl.BlockSpec((None, 1, D_MODEL), lambda j, i, be, nu: (be[i], 0, 0))],
            out_specs=pl.BlockSpec((MOE_ROWS * TOKEN_ROWS, LANES), lambda j, i, be, nu: (i, 0)),
            scratch_shapes=[pltpu.VMEM((D_FF, D_MODEL), BF16)]),
        compiler_params=_params("arbitrary", "arbitrary"),
        name="moe_down",
    )(blk_expert, n_used, hdn, w_down, b_down)


def _combine_kernel(dest_hbm, ys_hbm, gate_ref, x1_ref, g_ref, b_ref, op_ref, ot_ref, buf0, buf1, idx_smem, idx_sem,
                    row_sem):
    i = pl.program_id(0)
    prompt_steps = N_PROMPT // COMBINE_TOKENS

    def compute(buf):
        gate = gate_ref[...]
        ffn = None
        for k in range(TOP_K):
            term = _from_token_rows(buf, k * COMBINE_TOKENS, COMBINE_TOKENS) * gate[:, k:k + 1]
            ffn = term if ffn is None else ffn + term
        y = _layer_norm(DN_ALPHA * x1_ref[...] + ffn, g_ref[...], b_ref[...])

        @pl.when(i < prompt_steps)
        def _():
            op_ref[...] = y

        @pl.when(i >= prompt_steps)
        def _():
            ot_ref[...] = y

    _gather_pipeline(i, TOKENS // COMBINE_TOKENS - 1, dest_hbm, ys_hbm, (buf0, buf1), idx_smem,
                     idx_sem, row_sem, COMBINE_TOKENS * TOP_K, [(None, compute)])


def _combine(dest, y_slots, gate, x1, ln_g, ln_b):
    steps = TOKENS // COMBINE_TOKENS
    prompt_steps = N_PROMPT // COMBINE_TOKENS
    per_step = COMBINE_TOKENS * TOP_K
    rows = lambda w: pl.BlockSpec((COMBINE_TOKENS, w), lambda i: (i, 0))
    full = lambda a: pl.BlockSpec(a.shape, lambda i: (0, 0))
    order = dest.reshape(steps, COMBINE_TOKENS, TOP_K).transpose(0, 2, 1).reshape(steps, 1, per_step)
    order = jnp.concatenate([order, order[:2]], axis=0)
    return pl.pallas_call(
        _combine_kernel,
        out_shape=[jax.ShapeDtypeStruct((N_PROMPT, D_MODEL), F32),
                   jax.ShapeDtypeStruct((TOKENS - N_PROMPT, D_MODEL), F32)],
        grid=(steps,),
        in_specs=[pl.BlockSpec(memory_space=pl.ANY), pl.BlockSpec(memory_space=pl.ANY),
                  rows(HEAD_DIM), rows(D_MODEL), full(ln_g), full(ln_b)],
        out_specs=[pl.BlockSpec((COMBINE_TOKENS, D_MODEL), lambda i: (jnp.minimum(i, prompt_steps - 1), 0)),
                   pl.BlockSpec((COMBINE_TOKENS, D_MODEL), lambda i: (jnp.maximum(i - prompt_steps, 0), 0))],
        scratch_shapes=[pltpu.VMEM((per_step * TOKEN_ROWS, LANES), F32), pltpu.VMEM((per_step * TOKEN_ROWS, LANES), F32),
                        pltpu.SMEM((2, 1, per_step), I32), pltpu.SemaphoreType.DMA((2,)),
                        pltpu.SemaphoreType.DMA((2,))],
        compiler_params=_params("arbitrary"),
        name="moe_combine_ln",
    )(order, y_slots, gate, x1, ln_g, ln_b)


def _expert_onehot(e_ref, k):
    e = e_ref[...]
    lane = lax.broadcasted_iota(I32, e.shape, 1)
    return e[:, k:k + 1] == lane


def _rank_kernel(e_ref, tri_ref, rank_ref, cnt_ref, base):
    @pl.when(pl.program_id(0) == 0)
    def _():
        base[...] = jnp.zeros_like(base)

    lane = lax.broadcasted_iota(I32, rank_ref.shape, 1)
    seen = base[0:1, :]
    out = jnp.zeros(rank_ref.shape, F32)
    for k in range(TOP_K):
        hit = _expert_onehot(e_ref, k)
        ones = jnp.where(hit, 1.0, 0.0)
        before = jnp.dot(tri_ref[...], ones.astype(BF16), preferred_element_type=F32) + seen
        out = jnp.where(lane == k, jnp.sum(jnp.where(hit, before, 0.0), axis=-1, keepdims=True), out)
        seen = seen + jnp.sum(ones, axis=0, keepdims=True)
    base[...] = jnp.broadcast_to(seen, base.shape)
    rank_ref[...] = out.astype(I32)
    cnt_ref[...] = jnp.broadcast_to(seen, cnt_ref.shape).astype(I32)


def _dest_kernel(e_ref, rank_ref, start_ref, dest_ref):
    lane = lax.broadcasted_iota(I32, dest_ref.shape, 1)
    out = jnp.zeros(dest_ref.shape, F32)
    for k in range(TOP_K):
        start = jnp.sum(jnp.where(_expert_onehot(e_ref, k), start_ref[0:1, :], 0.0), axis=-1, keepdims=True)
        out = jnp.where(lane == k, start, out)
    dest_ref[...] = out.astype(I32) + rank_ref[...]


def _moe_layout(top_e):
    tm = 512
    steps = -(-TOKENS // tm)
    tri = (jnp.arange(tm)[:, None] > jnp.arange(tm)[None, :]).astype(BF16)
    rows = pl.BlockSpec((tm, LANES), lambda i: (i, 0))
    fixed = lambda shape: pl.BlockSpec(shape, lambda i: (0, 0))
    e_pad = jnp.pad(top_e, ((0, steps * tm - TOKENS), (0, 0)), constant_values=N_EXPERTS)
    rank, counts = pl.pallas_call(
        _rank_kernel,
        out_shape=[jax.ShapeDtypeStruct((steps * tm, LANES), I32), jax.ShapeDtypeStruct((SUBLANES, LANES), I32)],
        grid=(steps,),
        in_specs=[rows, fixed((tm, tm))],
        out_specs=[rows, fixed((SUBLANES, LANES))],
        scratch_shapes=[pltpu.VMEM((SUBLANES, LANES), F32)],
        compiler_params=_params("arbitrary"),
        name="moe_rank",
    )(e_pad, tri)
    counts = counts[0, :N_EXPERTS]
    padded = (counts + MOE_ROWS - 1) // MOE_ROWS * MOE_ROWS
    pad_end = jnp.cumsum(padded)
    starts = jnp.zeros((SUBLANES, LANES), F32).at[:, :N_EXPERTS].set((pad_end - padded).astype(F32))
    dest = pl.pallas_call(
        _dest_kernel,
        out_shape=jax.ShapeDtypeStruct((steps * tm, LANES), I32),
        grid=(steps,),
        in_specs=[rows, rows, fixed((SUBLANES, LANES))],
        out_specs=rows,
        compiler_params=_params("arbitrary"),
        name="moe_dest",
    )(e_pad, rank, starts)[:TOKENS, :TOP_K]
    tok = jnp.broadcast_to(jnp.arange(TOKENS, dtype=I32)[:, None], (TOKENS, TOP_K))
    slot_tok = jnp.zeros((MOE_SLOTS,), I32).at[dest.reshape(-1)].set(tok.reshape(-1), unique_indices=True)
    n_used = pad_end[-1] // MOE_ROWS
    blk = jnp.minimum(jnp.arange(MOE_BLOCKS, dtype=I32), n_used - 1) * MOE_ROWS
    blk_expert = jnp.minimum(jnp.searchsorted(pad_end, blk, side='right'), N_EXPERTS - 1).astype(I32)
    return dest, slot_tok, blk_expert, n_used.reshape(1).astype(I32)


def kernel(x_prompt, x_sample, mem_prompt, cache_kv_w128, cache_kv_w512, cache_kv_w2048, cache_mem_kv, state_conv, state_lru_h, w_in, conv_w, conv_b, w_rg_a, b_rg_a, w_rg_x, b_rg_x, lru_lambda, w_mem_kv, p_att, p_lru, p_mem, w_out, ln1_g, ln1_b, w_router, b_router, w_gate_up, b_gate_up, w_down, b_down, ln2_g, ln2_b):
    row = lambda a: a.reshape(1, -1)
    layer = lambda a: a.reshape(a.shape[1:])
    x_p = x_prompt.reshape(N_PROMPT, D_MODEL)
    x_s = x_sample.reshape(DEC_BATCH, D_MODEL)
    xb = x_p.astype(BF16)
    w_in_b = layer(w_in).astype(BF16)
    qkv_cols = 3 * ATT_WIDTH
    gate_col = qkv_cols + UG_WIDTH

    qkv = _matmul(xb, w_in_b[:, :qkv_cols], 1024, 1536, "in_proj_qkv")
    ug = _matmul(xb, w_in_b[:, qkv_cols:gate_col], 1024, 1024, "in_proj_lru_mem")
    gates = _matmul(xb, w_in_b[:, gate_col:], 1024, 1536, "in_proj_gates")

    c_p, sa_p, sb_p = _rope_tables(jnp.arange(SEQ, dtype=I32))
    y_att_p, k_rot = _prompt_attention(qkv, c_p, sa_p, sb_p)
    wa, wx = layer(w_rg_a).astype(BF16), layer(w_rg_x).astype(BF16)
    lru_args = (layer(conv_w), row(conv_b), wa, row(b_rg_a), wx, row(b_rg_x), row(lru_lambda))
    y_lru_p, h_last_p = _prompt_lru(ug, *lru_args)
    mem_kv_p = _matmul(mem_prompt.reshape(BATCH * MEM_TOKENS, D_MODEL), layer(w_mem_kv).astype(BF16), 512, 1024,
                       "mem_kv_proj")
    y_mem_p = _prompt_mem_attention(ug, mem_kv_p)

    merged = _merge(y_att_p, y_lru_p, y_mem_p, gates,
                    layer(p_att).astype(BF16), layer(p_lru).astype(BF16), layer(p_mem).astype(BF16))
    x1, x1_rows, top_e, gate = _outproj_router(merged, x_p, layer(w_out).astype(BF16), row(ln1_g), row(ln1_b),
                                               layer(w_router).astype(BF16), row(b_router))

    h_s_in = _matmul(x_s, layer(w_in), DEC_BATCH, 512, "sample_in_proj", full_precision=True)
    c_s, sa_s, sb_s = _rope_tables(jnp.full((1,), PAST_LEN, I32))
    caches = [cache.reshape(DEC_BATCH, ATT_BLOCK, dil, 2, HEADS, HEAD_DIM)
              for cache, dil in zip((cache_kv_w128, cache_kv_w512, cache_kv_w2048), DILATIONS)]
    y_att_s, y_mem_s, k_rot_s = _sample_attention(h_s_in, c_s, sa_s, sb_s, caches, layer(cache_mem_kv))
    y_lru_s, h_s = _sample_lru(h_s_in, state_conv.reshape(DEC_BATCH, (CONV_W - 1) * LRU_WIDTH), layer(state_lru_h),
                               layer(conv_w), row(conv_b), layer(w_rg_a), row(b_rg_a), layer(w_rg_x), row(b_rg_x),
                               row(lru_lambda))
    hi_mm = lambda a, w, name: _matmul(a, layer(w), DEC_BATCH, 512, name, full_precision=True)
    merged_s = _sample_merge(h_s_in, hi_mm(y_att_s, p_att, "sample_p_att"), hi_mm(y_lru_s, p_lru, "sample_p_lru"),
                             hi_mm(y_mem_s, p_mem, "sample_p_mem"))
    mix_s = hi_mm(merged_s, w_out, "sample_out_proj")
    x1_s, top_e_s, gate_s = _sample_ln_router(mix_s, x_s, row(ln1_g), row(ln1_b), layer(w_router), row(b_router))

    pad = TOKENS - N_PROMPT - DEC_BATCH
    tail = lambda s, p: jnp.concatenate([s, p], axis=0)
    pad_e = (jnp.arange(pad, dtype=I32)[:, None] * TOP_K + jnp.arange(HEAD_DIM, dtype=I32)[None, :]) % N_EXPERTS
    x1_tail = tail(x1_s, jnp.zeros((pad, D_MODEL), F32))
    x1 = lax.dynamic_update_slice(x1, x1_tail, (N_PROMPT, 0))
    x1_rows = lax.dynamic_update_slice(x1_rows, x1_tail.reshape(-1, LANES), (N_PROMPT * TOKEN_ROWS, 0))
    top_e = lax.dynamic_update_slice(top_e, tail(top_e_s, pad_e), (N_PROMPT, 0))
    gate = lax.dynamic_update_slice(gate, tail(gate_s, jnp.zeros((pad, HEAD_DIM), F32)), (N_PROMPT, 0))

    dest, slot_tok, blk_expert, n_used = _moe_layout(top_e)
    hdn = _expert_gate_up(blk_expert, n_used, slot_tok, x1_rows, layer(w_gate_up),
                          b_gate_up.reshape(N_EXPERTS, 1, 2 * D_FF))
    y_slots = _expert_down(blk_expert, n_used, hdn, layer(w_down), b_down.reshape(N_EXPERTS, 1, D_MODEL))
    y_p, y_tail = _combine(dest, y_slots, gate, x1, row(ln2_g), row(ln2_b))

    y_prompt = y_p.reshape(BATCH, SEQ, D_MODEL)
    y_sample = y_tail[:DEC_BATCH].reshape(DEC_BATCH, 1, D_MODEL)
    p_kv = _prompt_kv_caches(k_rot, qkv)
    s_kv = []
    for g in range(len(WINDOWS)):
        v_cols = slice(2 * ATT_WIDTH + g * ATT_OUT, 2 * ATT_WIDTH + (g + 1) * ATT_OUT)
        ks = k_rot_s[:, g * ATT_OUT:(g + 1) * ATT_OUT].reshape(DEC_BATCH, 1, HEADS, HEAD_DIM)
        vs = h_s_in[:, v_cols].reshape(DEC_BATCH, 1, HEADS, HEAD_DIM)
        s_kv.append(jnp.stack([ks, vs], axis=2)[None])
    p_mem_kv = mem_kv_p.reshape(1, BATCH, MEM_TOKENS, 2, MEM_HEADS, MEM_HEAD_DIM)
    p_conv = ug.reshape(BATCH, SEQ, UG_WIDTH)[:, SEQ - (CONV_W - 1):, :LRU_WIDTH][None]
    p_lru_h = h_last_p.reshape(1, BATCH, LRU_WIDTH)
    u_s = h_s_in[:, qkv_cols:qkv_cols + LRU_WIDTH]
    s_conv = jnp.concatenate([layer(state_conv)[:, 1:], u_s[:, None, :]], axis=1)[None]
    s_lru_h = h_s[None]
    return (y_prompt, y_sample, p_kv[0], p_kv[1], p_kv[2], p_mem_kv, p_conv, p_lru_h,
            s_kv[0], s_kv[1], s_kv[2], s_conv, s_lru_h)
```

```python
import jax
import jax.numpy as jnp
from jax import lax
from jax.experimental import pallas as pl
from jax.experimental.pallas import tpu as pltpu

F32 = jnp.float32
BF16 = jnp.bfloat16
I32 = jnp.int32

D_MODEL = 2048
BATCH = 8
SEQ = 2048
DEC_BATCH = 32
PAST_LEN = 8192
HEAD_DIM = 128
HEADS = 4
DILATIONS = (1, 4, 16)
WINDOWS = (128, 512, 2048)
ATT_BLOCK = 128
ATT_WIDTH = 1536
ATT_OUT = 512
ROT_DIM = 32
ROPE_THETA = 500000.0
LRU_WIDTH = 1536
LRU_BLOCK = 128
LRU_C = 8.0
CONV_W = 4
MEM_TOKENS = 256
MEM_HEADS = 4
MEM_HEAD_DIM = 256
MEM_WIDTH = 1024
N_EXPERTS = 32
TOP_K = 4
D_FF = 2048
SWIGLU_LIMIT = 7.0
SWIGLU_ALPHA = 1.702
LN_EPS = 1e-5
DN_ALPHA = 2.0 ** 0.25

N_PROMPT = BATCH * SEQ
OUTPROJ_ROWS = 256
TOKENS = N_PROMPT + OUTPROJ_ROWS
UG_WIDTH = 2 * LRU_WIDTH + MEM_WIDTH

V7X_VMEM_LIMIT = 56 * 1024 * 1024

LRU_COLS = 512
LRU_ROWS = 256
SUBLANES = 8
MOE_ROWS = 256
MOE_BLOCKS = TOKENS * TOP_K // MOE_ROWS + N_EXPERTS
MOE_SLOTS = MOE_BLOCKS * MOE_ROWS
FF_TILE = 1024
COMBINE_TOKENS = 128
GATHER_DEPTH = 3
LANES = 128
TOKEN_ROWS = D_MODEL // LANES


def _params(*sem):
    return pltpu.CompilerParams(dimension_semantics=sem, vmem_limit_bytes=V7X_VMEM_LIMIT)


def _bdot(a, b):
    return jnp.dot(a.astype(BF16), b.astype(BF16), preferred_element_type=F32)


def _hdot(a, b):
    return jnp.dot(a, b, precision=lax.Precision.HIGHEST, preferred_element_type=F32)


def _round_bf16(x):
    return x.astype(BF16).astype(F32)


def _keep_f32(x):
    return x


def _sigmoid(x):
    return 1.0 / (1.0 + jnp.exp(-x))


def _gelu_tanh(x):
    return 0.5 * x * (1.0 + jnp.tanh(0.7978845608028654 * (x + 0.044715 * (x * x * x))))


def _layer_norm(x, g, b):
    mu = jnp.mean(x, axis=-1, keepdims=True)
    xc = x - mu
    var = jnp.mean(xc * xc, axis=-1, keepdims=True)
    return xc * lax.rsqrt(var + LN_EPS) * g + b


def _mm_kernel(x_ref, w_ref, o_ref):
    o_ref[...] = _bdot(x_ref[...], w_ref[...]).astype(o_ref.dtype)


def _mm_hi_kernel(x_ref, w_ref, o_ref):
    o_ref[...] = _hdot(x_ref[...], w_ref[...])


def _matmul(x, w, tm, tn, name, full_precision=False):
    m, k = x.shape
    n = w.shape[1]
    return pl.pallas_call(
        _mm_hi_kernel if full_precision else _mm_kernel,
        out_shape=jax.ShapeDtypeStruct((m, n), F32),
        grid=(n // tn, m // tm),
        in_specs=[pl.BlockSpec((tm, k), lambda j, i: (i, 0)),
                  pl.BlockSpec((k, tn), lambda j, i: (0, j))],
        out_specs=pl.BlockSpec((tm, tn), lambda j, i: (i, j)),
        compiler_params=_params("arbitrary", "arbitrary"),
        name=name,
    )(x, w)


def _rope_tables(pos):
    half = ROT_DIM // 2
    inv_freq = ROPE_THETA ** (-jnp.arange(half, dtype=F32) / half)
    ang = pos.astype(F32)[:, None] * inv_freq[None, :]
    cos, sin = jnp.cos(ang), jnp.sin(ang)
    t = pos.shape[0]
    rest = HEAD_DIM - ROT_DIM
    c = jnp.concatenate([cos, cos, jnp.ones((t, rest), F32)], axis=1)
    sa = jnp.concatenate([-sin, jnp.zeros((t, half + rest), F32)], axis=1)
    sb = jnp.concatenate([jnp.zeros((t, half), F32), sin, jnp.zeros((t, rest), F32)], axis=1)
    return c, sa, sb


def _rope(x, c, sa, sb):
    half = ROT_DIM // 2
    return x * c + pltpu.roll(x, HEAD_DIM - half, 1) * sa + pltpu.roll(x, half, 1) * sb


def _group_merge(lses, outs, operand=_round_bf16):
    m = jnp.maximum(jnp.maximum(lses[0], lses[1]), lses[2])
    es = [jnp.exp(l - m) for l in lses]
    den = es[0] + es[1] + es[2]
    y = None
    for e, o in zip(es, outs):
        term = operand(e / den) * operand(o)
        y = term if y is None else y + term
    return y


def _attn_kernel(q0, q1, q2, k0, k1, k2, v0, v1, v2, c_ref, sa_ref, sb_ref,
                 y_ref, kr0, kr1, kr2, qs, os_, ls):
    c, sa, sb = c_ref[...], sa_ref[...], sb_ref[...]
    scale = HEAD_DIM ** -0.5
    ii = lax.broadcasted_iota(I32, (ATT_BLOCK, ATT_BLOCK), 0)
    jj = lax.broadcasted_iota(I32, (ATT_BLOCK, ATT_BLOCK), 1)
    cur_mask = jj <= ii
    prev_mask = jj >= ii
    dn = (((1,), (1,)), ((), ()))
    for g, (q_ref, k_ref, v_ref, kr_ref) in enumerate(((q0, k0, v0, kr0), (q1, k1, v1, kr1), (q2, k2, v2, kr2))):
        dil = DILATIONS[g]
        qs[...] = _rope(q_ref[...], c, sa, sb)
        kr_ref[...] = _rope(k_ref[...], c, sa, sb)
        o_g, l_g = os_.at[g], ls.at[g]

        def rows_of(start, dil=dil):
            return pl.ds(start, ATT_BLOCK, stride=dil) if dil > 1 else pl.ds(start, ATT_BLOCK)

        for cls in range(dil):
            for b in range(SEQ // (dil * ATT_BLOCK)):
                start = cls + dil * ATT_BLOCK * b
                rows = rows_of(start)
                q = qs[rows, :].astype(BF16)
                kc = kr_ref[rows, :].astype(BF16)
                vc = v_ref[rows, :].astype(BF16)
                s_c = lax.dot_general(q, kc, dn, preferred_element_type=F32) * scale
                s_c = jnp.where(cur_mask, s_c, -jnp.inf)
                m = jnp.max(s_c, axis=-1, keepdims=True)
                if b > 0:
                    prows = rows_of(start - dil * ATT_BLOCK)
                    kp = kr_ref[prows, :].astype(BF16)
                    vp = v_ref[prows, :].astype(BF16)
                    s_p = lax.dot_general(q, kp, dn, preferred_element_type=F32) * scale
                    s_p = jnp.where(prev_mask, s_p, -jnp.inf)
                    m = jnp.maximum(m, jnp.max(s_p, axis=-1, keepdims=True))
                    p_p = jnp.exp(s_p - m)
                p_c = jnp.exp(s_c - m)
                den = jnp.sum(p_c, axis=-1, keepdims=True)
                if b > 0:
                    den = den + jnp.sum(p_p, axis=-1, keepdims=True)
                o = jnp.dot((p_c / den).astype(BF16), vc, preferred_element_type=F32)
                if b > 0:
                    o = o + jnp.dot((p_p / den).astype(BF16), vp, preferred_element_type=F32)
                lse = m + jnp.log(den)
                o_g[rows, :] = o
                l_g[rows, :] = jnp.broadcast_to(lse, (ATT_BLOCK, HEAD_DIM))
    y = _group_merge([ls[0], ls[1], ls[2]], [os_[0], os_[1], os_[2]])
    y_ref[...] = y.astype(y_ref.dtype)


def _prompt_attention(qkv, c, sa, sb):
    blk = (SEQ, HEAD_DIM)
    head_cols = ATT_WIDTH // HEAD_DIM

    def col(base, g):
        return pl.BlockSpec(blk, lambda n, h: (n, base + g * HEADS + h))

    in_specs = ([col(0, g) for g in range(3)] + [col(head_cols, g) for g in range(3)]
                + [col(2 * head_cols, g) for g in range(3)] + [pl.BlockSpec(blk, lambda n, h: (0, 0))] * 3)
    res = pl.pallas_call(
        _attn_kernel,
        out_shape=[jax.ShapeDtypeStruct((N_PROMPT, ATT_OUT), BF16)]
        + [jax.ShapeDtypeStruct((N_PROMPT, ATT_OUT), F32)] * 3,
        grid=(BATCH, HEADS),
        in_specs=in_specs,
        out_specs=[pl.BlockSpec(blk, lambda n, h: (n, h))] * 4,
        scratch_shapes=[pltpu.VMEM(blk, F32), pltpu.VMEM((3,) + blk, F32), pltpu.VMEM((3,) + blk, F32)],
        compiler_params=_params("arbitrary", "arbitrary"),
        name="prompt_attention",
    )(*([qkv] * 9), c, sa, sb)
    return res[0], res[1:]


KV_ROWS = 256


def _kv_cache_kernel(k0, k1, k2, v0, v1, v2, o0, o1, o2):
    tb = pl.program_id(1)

    def put(o_ref, k, v):
        for h in range(HEADS):
            o_ref[:, 0, h, :] = k[:, h * HEAD_DIM:(h + 1) * HEAD_DIM]
            o_ref[:, 1, h, :] = v[:, h * HEAD_DIM:(h + 1) * HEAD_DIM]

    for (k_ref, v_ref, o_ref), window in zip(((k0, v0, o0), (k1, v1, o1), (k2, v2, o2)), WINDOWS):
        keep = min(window, SEQ)
        if keep >= KV_ROWS:
            first = (SEQ - keep) // KV_ROWS
            pl.when(tb >= first)(lambda k_ref=k_ref, v_ref=v_ref, o_ref=o_ref: put(o_ref, k_ref[...], v_ref[...]))
        else:
            pl.when(tb == SEQ // KV_ROWS - 1)(
                lambda k_ref=k_ref, v_ref=v_ref, o_ref=o_ref, keep=keep:
                put(o_ref, k_ref[KV_ROWS - keep:, :], v_ref[KV_ROWS - keep:, :]))


def _prompt_kv_caches(k_rot, qkv):
    steps = SEQ // KV_ROWS
    v_col = 2 * ATT_WIDTH // ATT_OUT
    out_shapes, out_specs = [], []
    for window in WINDOWS:
        keep = min(window, SEQ)
        rows = min(keep, KV_ROWS)
        first = (SEQ - keep) // KV_ROWS
        out_shapes.append(jax.ShapeDtypeStruct((1, BATCH, keep, 2, HEADS, HEAD_DIM), F32))
        out_specs.append(pl.BlockSpec((None, None, rows, 2, HEADS, HEAD_DIM),
                                      lambda n, tb, first=first: (0, n, jnp.maximum(tb - first, 0), 0, 0, 0)))
    return pl.pallas_call(
        _kv_cache_kernel,
        out_shape=out_shapes,
        grid=(BATCH, steps),
        in_specs=[pl.BlockSpec((KV_ROWS, ATT_OUT), lambda n, tb: (n * steps + tb, 0))] * 3
        + [pl.BlockSpec((KV_ROWS, ATT_OUT), lambda n, tb, g=g: (n * steps + tb, v_col + g)) for g in range(3)],
        out_specs=out_specs,
        compiler_params=_params("arbitrary", "arbitrary"),
        name="prompt_kv_caches",
    )(*k_rot, qkv, qkv, qkv)


def _lru_gates(uc, wa_ref, ba, wx_ref, bx, lam, dot=_bdot):
    n_blk = uc.shape[1] // LRU_BLOCK
    r = jnp.concatenate([dot(uc[:, j * LRU_BLOCK:(j + 1) * LRU_BLOCK], wa_ref[j]) for j in range(n_blk)], axis=1)
    i = jnp.concatenate([dot(uc[:, j * LRU_BLOCK:(j + 1) * LRU_BLOCK], wx_ref[j]) for j in range(n_blk)], axis=1)
    r = _sigmoid(r + ba)
    i = _sigmoid(i + bx)
    neg = -lam
    softplus = jnp.maximum(neg, 0.0) + jnp.log1p(jnp.exp(-jnp.abs(neg)))
    log_a = (-LRU_C * r) * softplus
    a = jnp.exp(log_a)
    b = jnp.sqrt(-jnp.tanh(log_a) * (a * a + 1.0)) * (i * uc)
    return a, b


def _lru_kernel(u_ref, g_ref, cw_ref, cb_ref, wa_ref, ba_ref, wx_ref, bx_ref, lam_ref,
                y_ref, hl_ref, hist, carry):
    @pl.when(pl.program_id(2) == 0)
    def _():
        hist[...] = jnp.zeros_like(hist)
        carry[...] = jnp.zeros_like(carry)

    u = u_ref[...]
    cw = cw_ref[...]
    ext = jnp.concatenate([hist[...], u], axis=0)
    off = SUBLANES - (CONV_W - 1)
    conv = ext[off:off + LRU_ROWS] * cw[0:1]
    for t in range(1, CONV_W):
        conv = conv + ext[off + t:off + t + LRU_ROWS] * cw[t:t + 1]
    uc = cb_ref[...] + conv
    hist[...] = u[LRU_ROWS - SUBLANES:]

    a, b = _lru_gates(uc, wa_ref, ba_ref[...], wx_ref, bx_ref[...], lam_ref[...])

    step = lax.broadcasted_iota(I32, a.shape, 0) & (SUBLANES - 1)
    for s in (1, 2, 4):
        a_s = pltpu.roll(a, s, 0)
        b_s = pltpu.roll(b, s, 0)
        valid = step >= s
        b = jnp.where(valid, a * b_s + b, b)
        a = jnp.where(valid, a * a_s, a)
    h_prev = carry[0:1, :]
    hs = []
    for j in range(LRU_ROWS // SUBLANES):
        h_j = a[j * SUBLANES:(j + 1) * SUBLANES] * h_prev + b[j * SUBLANES:(j + 1) * SUBLANES]
        h_prev = h_j[SUBLANES - 1:SUBLANES]
        hs.append(h_j)
    h = jnp.concatenate(hs, axis=0)
    carry[...] = jnp.broadcast_to(h_prev, carry.shape)
    y_ref[...] = (h * _gelu_tanh(g_ref[...])).astype(y_ref.dtype)
    hl_ref[...] = h_prev


def _prompt_lru(ug, conv_w, conv_b, wa, ba, wx, bx, lam):
    t_blocks = SEQ // LRU_ROWS
    c_blocks = LRU_WIDTH // LRU_COLS
    row = pl.BlockSpec((1, LRU_COLS), lambda n, cb, tb: (0, cb))
    gate_w = pl.BlockSpec((LRU_COLS // LRU_BLOCK, LRU_BLOCK, LRU_BLOCK), lambda n, cb, tb: (cb, 0, 0))
    return pl.pallas_call(
        _lru_kernel,
        out_shape=[jax.ShapeDtypeStruct((N_PROMPT, LRU_WIDTH), BF16),
                   jax.ShapeDtypeStruct((BATCH, 1, LRU_WIDTH), F32)],
        grid=(BATCH, c_blocks, t_blocks),
        in_specs=[pl.BlockSpec((LRU_ROWS, LRU_COLS), lambda n, cb, tb: (n * t_blocks + tb, cb)),
                  pl.BlockSpec((LRU_ROWS, LRU_COLS), lambda n, cb, tb: (n * t_blocks + tb, c_blocks + cb)),
                  pl.BlockSpec((CONV_W, LRU_COLS), lambda n, cb, tb: (0, cb)),
                  row, gate_w, row, gate_w, row, row],
        out_specs=[pl.BlockSpec((LRU_ROWS, LRU_COLS), lambda n, cb, tb: (n * t_blocks + tb, cb)),
                   pl.BlockSpec((None, 1, LRU_COLS), lambda n, cb, tb: (n, 0, cb))],
        scratch_shapes=[pltpu.VMEM((SUBLANES, LRU_COLS), F32), pltpu.VMEM((SUBLANES, LRU_COLS), F32)],
        compiler_params=_params("arbitrary", "arbitrary", "arbitrary"),
        name="prompt_lru",
    )(ug, ug, conv_w, conv_b, wa, ba, wx, bx, lam)


def _sample_lru_kernel(h_ref_in, sc_ref, h0_ref, cw_ref, cb_ref, wa_ref, ba_ref, wx_ref, bx_ref, lam_ref,
                       y_ref, h_ref):
    u = h_ref_in[:, 3 * ATT_WIDTH:3 * ATT_WIDTH + LRU_WIDTH]
    g = h_ref_in[:, 3 * ATT_WIDTH + LRU_WIDTH:3 * ATT_WIDTH + 2 * LRU_WIDTH]
    cw = cw_ref[...]
    conv = sc_ref[:, 0:LRU_WIDTH] * cw[0:1]
    for t in range(1, CONV_W - 1):
        conv = conv + sc_ref[:, t * LRU_WIDTH:(t + 1) * LRU_WIDTH] * cw[t:t + 1]
    conv = conv + u * cw[CONV_W - 1:CONV_W]
    uc = cb_ref[...] + conv
    a, b = _lru_gates(uc, wa_ref, ba_ref[...], wx_ref, bx_ref[...], lam_ref[...], dot=_hdot)
    h = b + a * h0_ref[...]
    h_ref[...] = h
    y_ref[...] = h * _gelu_tanh(g)


def _sample_lru(h_in, state_conv, h0, conv_w, conv_b, wa, ba, wx, bx, lam):
    args = (h_in, state_conv, h0, conv_w, conv_b, wa, ba, wx, bx, lam)
    full = lambda shape: pl.BlockSpec(shape, lambda i: (0,) * len(shape))
    return pl.pallas_call(
        _sample_lru_kernel,
        out_shape=[jax.ShapeDtypeStruct((DEC_BATCH, LRU_WIDTH), F32)] * 2,
        grid=(1,),
        in_specs=[full(a.shape) for a in args],
        out_specs=[full((DEC_BATCH, LRU_WIDTH))] * 2,
        compiler_params=_params("arbitrary"),
        name="sample_lru",
    )(*args)


def _mem_attn_kernel(q_ref, k_ref, v_ref, y_ref):
    dn = (((1,), (1,)), ((), ()))
    s = lax.dot_general(q_ref[...].astype(BF16), k_ref[...].astype(BF16), dn,
                        preferred_element_type=F32) * (MEM_HEAD_DIM ** -0.5)
    m = jnp.max(s, axis=-1, keepdims=True)
    p = jnp.exp(s - m)
    p = p / jnp.sum(p, axis=-1, keepdims=True)
    y_ref[...] = _bdot(p, v_ref[...]).astype(y_ref.dtype)


def _prompt_mem_attention(ug, mem_kv):
    rows = 1024
    r_blocks = SEQ // rows
    q_col = 2 * LRU_WIDTH // MEM_HEAD_DIM
    return pl.pallas_call(
        _mem_attn_kernel,
        out_shape=jax.ShapeDtypeStruct((N_PROMPT, MEM_WIDTH), BF16),
        grid=(BATCH, MEM_HEADS, r_blocks),
        in_specs=[pl.BlockSpec((rows, MEM_HEAD_DIM), lambda n, h, r: (n * r_blocks + r, q_col + h)),
                  pl.BlockSpec((MEM_TOKENS, MEM_HEAD_DIM), lambda n, h, r: (n, h)),
                  pl.BlockSpec((MEM_TOKENS, MEM_HEAD_DIM), lambda n, h, r: (n, MEM_HEADS + h))],
        out_specs=pl.BlockSpec((rows, MEM_HEAD_DIM), lambda n, h, r: (n * r_blocks + r, h)),
        compiler_params=_params("arbitrary", "arbitrary", "arbitrary"),
        name="prompt_mem_attention",
    )(ug, mem_kv, mem_kv)


def _heads_attend(q, keys, values, scale, extra=None):
    s = jnp.sum(keys * q[None], axis=-1, keepdims=True) * scale
    m = jnp.max(s, axis=0)
    if extra is not None:
        s_x = jnp.sum(extra[0] * q, axis=-1, keepdims=True) * scale
        m = jnp.maximum(m, s_x)
        p_x = jnp.exp(s_x - m)
    p = jnp.exp(s - m[None])
    den = jnp.sum(p, axis=0)
    if extra is not None:
        den = den + p_x
    o = jnp.sum((p / den[None]) * values, axis=0)
    if extra is not None:
        o = o + (p_x / den) * extra[1]
    return o, m + jnp.log(den)


def _sample_attn_kernel(h_ref, c_ref, sa_ref, sb_ref, w0_ref, w1_ref, w2_ref, mem_ref, ya_ref, ym_ref, kr_ref):
    c, sa, sb = c_ref[...], sa_ref[...], sb_ref[...]
    outs, lses = [], []
    for g, w_ref in enumerate((w0_ref, w1_ref, w2_ref)):
        q, k_new, v_new = [], [], []
        for h in range(HEADS):
            col = (g * HEADS + h) * HEAD_DIM
            q.append(_rope(h_ref[:, col:col + HEAD_DIM], c, sa, sb))
            k_new.append(_rope(h_ref[:, ATT_WIDTH + col:ATT_WIDTH + col + HEAD_DIM], c, sa, sb))
            v_new.append(h_ref[:, 2 * ATT_WIDTH + col:2 * ATT_WIDTH + col + HEAD_DIM])
            kr_ref[:, col:col + HEAD_DIM] = k_new[h]
        stack = lambda rows: jnp.concatenate(rows, axis=0)
        o, lse = _heads_attend(stack(q), w_ref[:, 0], w_ref[:, 1], HEAD_DIM ** -0.5,
                               extra=(stack(k_new), stack(v_new)))
        outs.append(o)
        lses.append(lse)
    y = _group_merge(lses, outs, operand=_keep_f32)
    for h in range(HEADS):
        ya_ref[:, h * HEAD_DIM:(h + 1) * HEAD_DIM] = y[h:h + 1]
    q_col = 3 * ATT_WIDTH + 2 * LRU_WIDTH
    qm = jnp.concatenate([h_ref[:, q_col + h * MEM_HEAD_DIM:q_col + (h + 1) * MEM_HEAD_DIM]
                          for h in range(MEM_HEADS)], axis=0)
    om, _ = _heads_attend(qm, mem_ref[:, 0], mem_ref[:, 1], MEM_HEAD_DIM ** -0.5)
    for h in range(MEM_HEADS):
        ym_ref[:, h * MEM_HEAD_DIM:(h + 1) * MEM_HEAD_DIM] = om[h:h + 1]


def _sample_attention(h_in, c, sa, sb, caches, cache_mem):
    full = lambda shape: pl.BlockSpec(shape, lambda i: (0,) * len(shape))
    row = lambda w: pl.BlockSpec((None, 1, w), lambda i: (i, 0, 0))
    window = pl.BlockSpec((None, ATT_BLOCK, None, 2, HEADS, HEAD_DIM), lambda i: (i, 0, 0, 0, 0, 0))
    widths = (ATT_OUT, MEM_WIDTH, ATT_WIDTH)
    y_att, y_mem, k_rot = pl.pallas_call(
        _sample_attn_kernel,
        out_shape=[jax.ShapeDtypeStruct((DEC_BATCH, 1, w), F32) for w in widths],
        grid=(DEC_BATCH,),
        in_specs=[row(h_in.shape[1]), full(c.shape), full(sa.shape), full(sb.shape), window, window, window,
                  pl.BlockSpec((None, MEM_TOKENS, 2, MEM_HEADS, MEM_HEAD_DIM), lambda i: (i, 0, 0, 0, 0))],
        out_specs=[row(w) for w in widths],
        compiler_params=_params("arbitrary"),
        name="sample_attention",
    )(h_in.reshape(DEC_BATCH, 1, -1), c, sa, sb, *caches, cache_mem)
    return [a.reshape(DEC_BATCH, -1) for a in (y_att, y_mem, k_rot)]


def _merge_kernel(ya_ref, yl_ref, ym_ref, ga_ref, gl_ref, gm_ref, pa_ref, pl_ref, pm_ref, o_ref):
    merged = _sigmoid(ga_ref[...]) * _bdot(ya_ref[...], pa_ref[...])
    merged = merged + _sigmoid(gl_ref[...]) * _bdot(yl_ref[...], pl_ref[...])
    merged = merged + _sigmoid(gm_ref[...]) * _bdot(ym_ref[...], pm_ref[...])
    o_ref[...] = merged.astype(o_ref.dtype)


def _merge(y_att, y_lru, y_mem, gates, p_att, p_lru, p_mem):
    tm = 256
    rows = lambda w: pl.BlockSpec((tm, w), lambda i: (i, 0))
    full = lambda a: pl.BlockSpec(a.shape, lambda i: (0, 0))
    return pl.pallas_call(
        _merge_kernel,
        out_shape=jax.ShapeDtypeStruct((N_PROMPT, D_MODEL), BF16),
        grid=(N_PROMPT // tm,),
        in_specs=[rows(ATT_OUT), rows(LRU_WIDTH), rows(MEM_WIDTH)]
        + [pl.BlockSpec((tm, D_MODEL), lambda i, b=b: (i, b)) for b in range(3)]
        + [full(p_att), full(p_lru), full(p_mem)],
        out_specs=rows(D_MODEL),
        compiler_params=_params("arbitrary"),
        name="branch_merge",
    )(y_att, y_lru, y_mem, gates, gates, gates, p_att, p_lru, p_mem)


def _sample_merge_kernel(h_ref, pa_ref, pl_ref, pm_ref, o_ref):
    gate_col = 3 * ATT_WIDTH + UG_WIDTH
    gate = lambda b: _sigmoid(h_ref[:, gate_col + b * D_MODEL:gate_col + (b + 1) * D_MODEL])
    o_ref[...] = (gate(0) * pa_ref[...] + gate(1) * pl_ref[...]) + gate(2) * pm_ref[...]


def _sample_merge(h_in, pa, pl_, pm):
    args = (h_in, pa, pl_, pm)
    full = lambda a: pl.BlockSpec(a.shape, lambda i: (0, 0))
    return pl.pallas_call(
        _sample_merge_kernel,
        out_shape=jax.ShapeDtypeStruct((DEC_BATCH, D_MODEL), F32),
        grid=(1,),
        in_specs=[full(a) for a in args],
        out_specs=pl.BlockSpec((DEC_BATCH, D_MODEL), lambda i: (0, 0)),
        compiler_params=_params("arbitrary"),
        name="sample_merge",
    )(*args)


def _ln_route(mix, x_ref, g_ref, b_ref, wr_ref, br_ref, x1_ref, e_ref, gt_ref, dot):
    x1 = _layer_norm(DN_ALPHA * x_ref[...] + mix, g_ref[...], b_ref[...])
    x1_ref[...] = x1
    logits = dot(x1, wr_ref[...]) + br_ref[...]
    lane = lax.broadcasted_iota(I32, logits.shape, 1).astype(F32)
    out_lane = lax.broadcasted_iota(I32, e_ref.shape, 1)
    top_v = []
    e_out = jnp.zeros(e_ref.shape, I32)
    for k in range(TOP_K):
        v = jnp.max(logits, axis=-1, keepdims=True)
        e = jnp.min(jnp.where(logits == v, lane, float(N_EXPERTS)), axis=-1, keepdims=True)
        logits = jnp.where(lane == e, -jnp.inf, logits)
        top_v.append(v)
        e_out = jnp.where(out_lane == k, e.astype(I32), e_out)
    ps = [jnp.exp(v - top_v[0]) for v in top_v]
    den = ps[0] + ps[1] + ps[2] + ps[3]
    g_out = jnp.zeros(gt_ref.shape, F32)
    for k in range(TOP_K):
        g_out = jnp.where(out_lane == k, ps[k] / den, g_out)
    e_ref[...] = e_out
    gt_ref[...] = g_out


def _outproj_kernel(m_ref, x_ref, w_ref, g_ref, b_ref, wr_ref, br_ref, x1_ref, x1r_ref, e_ref, gt_ref):
    @pl.when(pl.program_id(0) < N_PROMPT // OUTPROJ_ROWS)
    def _():
        mix = jnp.dot(m_ref[...], w_ref[...], preferred_element_type=F32)
        _ln_route(mix, x_ref, g_ref, b_ref, wr_ref, br_ref, x1_ref, e_ref, gt_ref, _bdot)
        _to_token_rows(x1_ref[...], x1r_ref, OUTPROJ_ROWS)

    @pl.when(pl.program_id(0) >= N_PROMPT // OUTPROJ_ROWS)
    def _():
        for ref in (x1_ref, x1r_ref, e_ref, gt_ref):
            ref[...] = jnp.zeros_like(ref)


def _sample_ln_router_kernel(mix_ref, x_ref, g_ref, b_ref, wr_ref, br_ref, x1_ref, e_ref, gt_ref):
    _ln_route(mix_ref[...], x_ref, g_ref, b_ref, wr_ref, br_ref, x1_ref, e_ref, gt_ref, _hdot)


def _sample_ln_router(mix, x, ln_g, ln_b, w_router, b_router):
    args = (mix, x, ln_g, ln_b, w_router, b_router)
    full = lambda shape: pl.BlockSpec(shape, lambda i: (0, 0))
    out_shapes = [jax.ShapeDtypeStruct((DEC_BATCH, D_MODEL), F32), jax.ShapeDtypeStruct((DEC_BATCH, HEAD_DIM), I32),
                  jax.ShapeDtypeStruct((DEC_BATCH, HEAD_DIM), F32)]
    return pl.pallas_call(
        _sample_ln_router_kernel,
        out_shape=out_shapes,
        grid=(1,),
        in_specs=[full(a.shape) for a in args],
        out_specs=[full(s.shape) for s in out_shapes],
        compiler_params=_params("arbitrary"),
        name="sample_ln_router",
    )(*args)


def _outproj_router(merged, x, w_out, ln_g, ln_b, w_router, b_router):
    tm = OUTPROJ_ROWS
    last = N_PROMPT // tm - 1
    in_rows = lambda w: pl.BlockSpec((tm, w), lambda i: (jnp.minimum(i, last), 0))
    rows = lambda w: pl.BlockSpec((tm, w), lambda i: (i, 0))
    full = lambda a: pl.BlockSpec(a.shape, lambda i: (0, 0))
    return pl.pallas_call(
        _outproj_kernel,
        out_shape=[jax.ShapeDtypeStruct((TOKENS, D_MODEL), F32),
                   jax.ShapeDtypeStruct((TOKENS * TOKEN_ROWS, LANES), F32),
                   jax.ShapeDtypeStruct((TOKENS, HEAD_DIM), I32),
                   jax.ShapeDtypeStruct((TOKENS, HEAD_DIM), F32)],
        grid=(TOKENS // tm,),
        in_specs=[in_rows(D_MODEL), in_rows(D_MODEL), full(w_out), full(ln_g), full(ln_b), full(w_router),
                  full(b_router)],
        out_specs=[rows(D_MODEL), pl.BlockSpec((tm * TOKEN_ROWS, LANES), lambda i: (i, 0)), rows(HEAD_DIM),
                   rows(HEAD_DIM)],
        compiler_params=_params("arbitrary"),
        name="outproj_ln_router",
    )(merged, x, w_out, ln_g, ln_b, w_router, b_router)


def _to_token_rows(x, ref, n):
    for j in range(TOKEN_ROWS):
        ref[pl.ds(j, n, stride=TOKEN_ROWS), :] = x[:, j * LANES:(j + 1) * LANES]


def _from_token_rows(ref, first, n):
    return jnp.concatenate([ref[pl.ds(first * TOKEN_ROWS + j, n, stride=TOKEN_ROWS), :] for j in range(TOKEN_ROWS)],
                           axis=1)


def _gather_pipeline(t, last, idx_hbm, src_hbm, bufs, idx_smem, idx_sem, row_sem, per_step, variants):
    depth = GATHER_DEPTH

    def idx_copy(step, slot):
        return pltpu.make_async_copy(idx_hbm.at[step], idx_smem.at[slot], idx_sem.at[slot])

    def row_copy(slot, r):
        tok = idx_smem[slot, 0, r]
        return pltpu.make_async_copy(src_hbm.at[pl.ds(tok * TOKEN_ROWS, TOKEN_ROWS)],
                                     bufs[slot].at[pl.ds(r * TOKEN_ROWS, TOKEN_ROWS)], row_sem.at[slot])

    def wait_rows(slot):
        pltpu.make_async_copy(src_hbm.at[pl.ds(0, per_step * TOKEN_ROWS)], bufs[slot], row_sem.at[slot]).wait()

    @pl.when(t == 0)
    def _():
        for s in range(depth - 1):
            idx_copy(s, s).start()
            idx_copy(s, s).wait()

            def first(r, carry, s=s):
                row_copy(s, r).start()
                return carry

            lax.fori_loop(0, per_step, first, 0)
        idx_copy(depth - 1, depth - 1).start()

    def step(slot, body):
        ahead = (slot + depth - 1) % depth
        wait_rows(slot)
        idx_copy(t + depth - 1, ahead).wait()
        for r in range(per_step):
            row_copy(ahead, r).start()
        idx_copy(t + depth, slot).start()
        body(bufs[slot])

    for slot in range(depth):
        mine = t % depth == slot
        for cond, body in variants:
            pl.when(mine if cond is None else jnp.logical_and(mine, cond))(
                lambda slot=slot, body=body: step(slot, body))

    @pl.when(t == last)
    def _():
        idx_copy(0, last % depth).wait()
        for s in range(1, depth):
            wait_rows((last + s) % depth)


def _gather_scratch(per_step):
    return ([pltpu.VMEM((per_step * TOKEN_ROWS, LANES), F32)] * GATHER_DEPTH
            + [pltpu.SMEM((GATHER_DEPTH, 1, per_step), I32), pltpu.SemaphoreType.DMA((GATHER_DEPTH,)),
               pltpu.SemaphoreType.DMA((GATHER_DEPTH,))])


def _expert_changed(be_ref, i):
    return jnp.logical_or(i == 0, be_ref[i] != be_ref[jnp.maximum(i - 1, 0)])


def _gate_up_kernel(be_ref, nu_ref, tok_hbm, x_hbm, wg_ref, wu_ref, bg_ref, bu_ref, o_ref,
                    wg_b, wu_b, *gather):
    i = pl.program_id(1)
    t = pl.program_id(0) * MOE_BLOCKS + i
    last = D_FF // FF_TILE * MOE_BLOCKS - 1

    @pl.when(_expert_changed(be_ref, i))
    def _():
        wg_b[...] = wg_ref[...].astype(BF16)
        wu_b[...] = wu_ref[...].astype(BF16)

    def compute(buf):
        x = _from_token_rows(buf, 0, MOE_ROWS).astype(BF16)
        g = jnp.dot(x, wg_b[...], preferred_element_type=F32) + bg_ref[...]
        u = jnp.dot(x, wu_b[...], preferred_element_type=F32) + bu_ref[...]
        g = jnp.minimum(g, SWIGLU_LIMIT)
        u = jnp.clip(u, -SWIGLU_LIMIT, SWIGLU_LIMIT)
        o_ref[...] = ((u + 1.0) * (g * _sigmoid(SWIGLU_ALPHA * g))).astype(o_ref.dtype)

    def unused(buf):
        o_ref[...] = jnp.zeros_like(o_ref)

    used = i < nu_ref[0]
    _gather_pipeline(t, last, tok_hbm, x_hbm, gather[:GATHER_DEPTH], *gather[GATHER_DEPTH:], MOE_ROWS,
                     [(used, compute), (jnp.logical_not(used), unused)])


def _expert_gate_up(blk_expert, n_used, slot_tok, x1_rows, w_gate_up, b_gate_up):
    ff_tiles = D_FF // FF_TILE
    blocks = slot_tok.reshape(MOE_BLOCKS, 1, MOE_ROWS)
    tok_steps = jnp.concatenate([blocks] * ff_tiles + [blocks[:GATHER_DEPTH]], axis=0)
    return pl.pallas_call(
        _gate_up_kernel,
        out_shape=jax.ShapeDtypeStruct((MOE_SLOTS, D_FF), BF16),
        grid_spec=pltpu.PrefetchScalarGridSpec(
            num_scalar_prefetch=2,
            grid=(ff_tiles, MOE_BLOCKS),
            in_specs=[pl.BlockSpec(memory_space=pl.ANY), pl.BlockSpec(memory_space=pl.ANY),
                      pl.BlockSpec((None, D_MODEL, FF_TILE), lambda j, i, be, nu: (be[i], 0, j)),
                      pl.BlockSpec((None, D_MODEL, FF_TILE), lambda j, i, be, nu: (be[i], 0, ff_tiles + j)),
                      pl.BlockSpec((None, 1, FF_TILE), lambda j, i, be, nu: (be[i], 0, j)),
                      pl.BlockSpec((None, 1, FF_TILE), lambda j, i, be, nu: (be[i], 0, ff_tiles + j))],
            out_specs=pl.BlockSpec((MOE_ROWS, FF_TILE), lambda j, i, be, nu: (i, j)),
            scratch_shapes=[pltpu.VMEM((D_MODEL, FF_TILE), BF16), pltpu.VMEM((D_MODEL, FF_TILE), BF16)]
            + _gather_scratch(MOE_ROWS)),
        compiler_params=_params("arbitrary", "arbitrary"),
        name="moe_gate_up",
    )(blk_expert, n_used, tok_steps, x1_rows, w_gate_up, w_gate_up, b_gate_up, b_gate_up)


def _down_kernel(be_ref, nu_ref, h_ref, w_ref, b_ref, o_ref, w_b):
    i = pl.program_id(1)

    @pl.when(_expert_changed(be_ref, i))
    def _():
        w_b[...] = w_ref[...].astype(BF16)

    @pl.when(i < nu_ref[0])
    def _():
        y = jnp.dot(h_ref[...], w_b[...], preferred_element_type=F32) + b_ref[...]
        _to_token_rows(y, o_ref, MOE_ROWS)

    @pl.when(i >= nu_ref[0])
    def _():
        o_ref[...] = jnp.zeros_like(o_ref)


def _expert_down(blk_expert, n_used, hdn, w_down, b_down):
    row_blk = lambda j, i, be, nu: jnp.minimum(i, nu[0] - 1)
    return pl.pallas_call(
        _down_kernel,
        out_shape=jax.ShapeDtypeStruct((MOE_SLOTS * TOKEN_ROWS, LANES), F32),
        grid_spec=pltpu.PrefetchScalarGridSpec(
            num_scalar_prefetch=2,
            grid=(1, MOE_BLOCKS),
            in_specs=[pl.BlockSpec((MOE_ROWS, D_FF), lambda j, i, be, nu: (row_blk(j, i, be, nu), 0)),
                      pl.BlockSpec((None, D_FF, D_MODEL), lambda j, i, be, nu: (be[i], 0, 0)),
                      pl.BlockSpec((None, 1, D_MODEL), lambda j, i, be, nu: (be[i], 0, 0))],
            out_specs=pl.BlockSpec((MOE_ROWS * TOKEN_ROWS, LANES), lambda j, i, be, nu: (i, 0)),
            scratch_shapes=[pltpu.VMEM((D_FF, D_MODEL), BF16)]),
        compiler_params=_params("arbitrary", "arbitrary"),
        name="moe_down",
    )(blk_expert, n_used, hdn, w_down, b_down)


def _combine_kernel(dest_hbm, ys_hbm, gate_ref, x1_ref, g_ref, b_ref, op_ref, ot_ref, *gather):
    i = pl.program_id(0)
    prompt_steps = N_PROMPT // COMBINE_TOKENS

    def compute(buf):
        gate = gate_ref[...]
        ffn = None
        for k in range(TOP_K):
            term = _from_token_rows(buf, k * COMBINE_TOKENS, COMBINE_TOKENS) * gate[:, k:k + 1]
            ffn = term if ffn is None else ffn + term
        y = _layer_norm(DN_ALPHA * x1_ref[...] + ffn, g_ref[...], b_ref[...])

        @pl.when(i < prompt_steps)
        def _():
            op_ref[...] = y

        @pl.when(i >= prompt_steps)
        def _():
            ot_ref[...] = y

    _gather_pipeline(i, TOKENS // COMBINE_TOKENS - 1, dest_hbm, ys_hbm, gather[:GATHER_DEPTH],
                     *gather[GATHER_DEPTH:], COMBINE_TOKENS * TOP_K, [(None, compute)])


def _combine(dest, y_slots, gate, x1, ln_g, ln_b):
    steps = TOKENS // COMBINE_TOKENS
    prompt_steps = N_PROMPT // COMBINE_TOKENS
    per_step = COMBINE_TOKENS * TOP_K
    rows = lambda w: pl.BlockSpec((COMBINE_TOKENS, w), lambda i: (i, 0))
    full = lambda a: pl.BlockSpec(a.shape, lambda i: (0, 0))
    order = dest.reshape(steps, COMBINE_TOKENS, TOP_K).transpose(0, 2, 1).reshape(steps, 1, per_step)
    order = jnp.concatenate([order, order[:GATHER_DEPTH]], axis=0)
    return pl.pallas_call(
        _combine_kernel,
        out_shape=[jax.ShapeDtypeStruct((N_PROMPT, D_MODEL), F32),
                   jax.ShapeDtypeStruct((TOKENS - N_PROMPT, D_MODEL), F32)],
        grid=(steps,),
        in_specs=[pl.BlockSpec(memory_space=pl.ANY), pl.BlockSpec(memory_space=pl.ANY),
                  rows(HEAD_DIM), rows(D_MODEL), full(ln_g), full(ln_b)],
        out_specs=[pl.BlockSpec((COMBINE_TOKENS, D_MODEL), lambda i: (jnp.minimum(i, prompt_steps - 1), 0)),
                   pl.BlockSpec((COMBINE_TOKENS, D_MODEL), lambda i: (jnp.maximum(i - prompt_steps, 0), 0))],
        scratch_shapes=_gather_scratch(per_step),
        compiler_params=_params("arbitrary"),
        name="moe_combine_ln",
    )(order, y_slots, gate, x1, ln_g, ln_b)


def _expert_onehot(e_ref, k):
    e = e_ref[...]
    lane = lax.broadcasted_iota(I32, e.shape, 1)
    return e[:, k:k + 1] == lane


def _rank_kernel(e_ref, tri_ref, rank_ref, cnt_ref, base):
    @pl.when(pl.program_id(0) == 0)
    def _():
        base[...] = jnp.zeros_like(base)

    lane = lax.broadcasted_iota(I32, rank_ref.shape, 1)
    seen = base[0:1, :]
    out = jnp.zeros(rank_ref.shape, F32)
    for k in range(TOP_K):
        hit = _expert_onehot(e_ref, k)
        ones = jnp.where(hit, 1.0, 0.0)
        before = jnp.dot(tri_ref[...], ones.astype(BF16), preferred_element_type=F32) + seen
        out = jnp.where(lane == k, jnp.sum(jnp.where(hit, before, 0.0), axis=-1, keepdims=True), out)
        seen = seen + jnp.sum(ones, axis=0, keepdims=True)
    base[...] = jnp.broadcast_to(seen, base.shape)
    rank_ref[...] = out.astype(I32)
    cnt_ref[...] = jnp.broadcast_to(seen, cnt_ref.shape).astype(I32)


def _dest_kernel(e_ref, rank_ref, start_ref, dest_ref):
    lane = lax.broadcasted_iota(I32, dest_ref.shape, 1)
    out = jnp.zeros(dest_ref.shape, F32)
    for k in range(TOP_K):
        start = jnp.sum(jnp.where(_expert_onehot(e_ref, k), start_ref[0:1, :], 0.0), axis=-1, keepdims=True)
        out = jnp.where(lane == k, start, out)
    dest_ref[...] = out.astype(I32) + rank_ref[...]


def _moe_layout(top_e):
    tm = 512
    steps = -(-TOKENS // tm)
    tri = (jnp.arange(tm)[:, None] > jnp.arange(tm)[None, :]).astype(BF16)
    rows = pl.BlockSpec((tm, LANES), lambda i: (i, 0))
    fixed = lambda shape: pl.BlockSpec(shape, lambda i: (0, 0))
    e_pad = jnp.pad(top_e, ((0, steps * tm - TOKENS), (0, 0)), constant_values=N_EXPERTS)
    rank, counts = pl.pallas_call(
        _rank_kernel,
        out_shape=[jax.ShapeDtypeStruct((steps * tm, LANES), I32), jax.ShapeDtypeStruct((SUBLANES, LANES), I32)],
        grid=(steps,),
        in_specs=[rows, fixed((tm, tm))],
        out_specs=[rows, fixed((SUBLANES, LANES))],
        scratch_shapes=[pltpu.VMEM((SUBLANES, LANES), F32)],
        compiler_params=_params("arbitrary"),
        name="moe_rank",
    )(e_pad, tri)
    counts = counts[0, :N_EXPERTS]
    padded = (counts + MOE_ROWS - 1) // MOE_ROWS * MOE_ROWS
    pad_end = jnp.cumsum(padded)
    starts = jnp.zeros((SUBLANES, LANES), F32).at[:, :N_EXPERTS].set((pad_end - padded).astype(F32))
    dest = pl.pallas_call(
        _dest_kernel,
        out_shape=jax.ShapeDtypeStruct((steps * tm, LANES), I32),
        grid=(steps,),
        in_specs=[rows, rows, fixed((SUBLANES, LANES))],
        out_specs=rows,
        compiler_params=_params("arbitrary"),
        name="moe_dest",
    )(e_pad, rank, starts)[:TOKENS, :TOP_K]
    tok = jnp.broadcast_to(jnp.arange(TOKENS, dtype=I32)[:, None], (TOKENS, TOP_K))
    slot_tok = jnp.zeros((MOE_SLOTS,), I32).at[dest.reshape(-1)].set(tok.reshape(-1), unique_indices=True)
    n_used = pad_end[-1] // MOE_ROWS
    blk = jnp.minimum(jnp.arange(MOE_BLOCKS, dtype=I32), n_used - 1) * MOE_ROWS
    blk_expert = jnp.sum((pad_end[None, :] <= blk[:, None]).astype(I32), axis=1)
    return dest, slot_tok, jnp.minimum(blk_expert, N_EXPERTS - 1), n_used.reshape(1).astype(I32)


def kernel(x_prompt, x_sample, mem_prompt, cache_kv_w128, cache_kv_w512, cache_kv_w2048, cache_mem_kv, state_conv, state_lru_h, w_in, conv_w, conv_b, w_rg_a, b_rg_a, w_rg_x, b_rg_x, lru_lambda, w_mem_kv, p_att, p_lru, p_mem, w_out, ln1_g, ln1_b, w_router, b_router, w_gate_up, b_gate_up, w_down, b_down, ln2_g, ln2_b):
    row = lambda a: a.reshape(1, -1)
    layer = lambda a: a.reshape(a.shape[1:])
    x_p = x_prompt.reshape(N_PROMPT, D_MODEL)
    x_s = x_sample.reshape(DEC_BATCH, D_MODEL)
    xb = x_p.astype(BF16)
    w_in_b = layer(w_in).astype(BF16)
    qkv_cols = 3 * ATT_WIDTH
    gate_col = qkv_cols + UG_WIDTH

    qkv = _matmul(xb, w_in_b[:, :qkv_cols], 1024, 1536, "in_proj_qkv")
    ug = _matmul(xb, w_in_b[:, qkv_cols:gate_col], 1024, 1024, "in_proj_lru_mem")
    gates = _matmul(xb, w_in_b[:, gate_col:], 1024, 1536, "in_proj_gates")

    c_p, sa_p, sb_p = _rope_tables(jnp.arange(SEQ, dtype=I32))
    y_att_p, k_rot = _prompt_attention(qkv, c_p, sa_p, sb_p)
    wa, wx = layer(w_rg_a).astype(BF16), layer(w_rg_x).astype(BF16)
    lru_args = (layer(conv_w), row(conv_b), wa, row(b_rg_a), wx, row(b_rg_x), row(lru_lambda))
    y_lru_p, h_last_p = _prompt_lru(ug, *lru_args)
    mem_kv_p = _matmul(mem_prompt.reshape(BATCH * MEM_TOKENS, D_MODEL), layer(w_mem_kv).astype(BF16), 512, 1024,
                       "mem_kv_proj")
    y_mem_p = _prompt_mem_attention(ug, mem_kv_p)

    merged = _merge(y_att_p, y_lru_p, y_mem_p, gates,
                    layer(p_att).astype(BF16), layer(p_lru).astype(BF16), layer(p_mem).astype(BF16))
    x1, x1_rows, top_e, gate = _outproj_router(merged, x_p, layer(w_out).astype(BF16), row(ln1_g), row(ln1_b),
                                               layer(w_router).astype(BF16), row(b_router))

    h_s_in = _matmul(x_s, layer(w_in), DEC_BATCH, 512, "sample_in_proj", full_precision=True)
    c_s, sa_s, sb_s = _rope_tables(jnp.full((1,), PAST_LEN, I32))
    caches = [cache.reshape(DEC_BATCH, ATT_BLOCK, dil, 2, HEADS, HEAD_DIM)
              for cache, dil in zip((cache_kv_w128, cache_kv_w512, cache_kv_w2048), DILATIONS)]
    y_att_s, y_mem_s, k_rot_s = _sample_attention(h_s_in, c_s, sa_s, sb_s, caches, layer(cache_mem_kv))
    y_lru_s, h_s = _sample_lru(h_s_in, state_conv.reshape(DEC_BATCH, (CONV_W - 1) * LRU_WIDTH), layer(state_lru_h),
                               layer(conv_w), row(conv_b), layer(w_rg_a), row(b_rg_a), layer(w_rg_x), row(b_rg_x),
                               row(lru_lambda))
    hi_mm = lambda a, w, name: _matmul(a, layer(w), DEC_BATCH, 512, name, full_precision=True)
    merged_s = _sample_merge(h_s_in, hi_mm(y_att_s, p_att, "sample_p_att"), hi_mm(y_lru_s, p_lru, "sample_p_lru"),
                             hi_mm(y_mem_s, p_mem, "sample_p_mem"))
    mix_s = hi_mm(merged_s, w_out, "sample_out_proj")
    x1_s, top_e_s, gate_s = _sample_ln_router(mix_s, x_s, row(ln1_g), row(ln1_b), layer(w_router), row(b_router))

    pad = TOKENS - N_PROMPT - DEC_BATCH
    tail = lambda s, p: jnp.concatenate([s, p], axis=0)
    pad_e = (jnp.arange(pad, dtype=I32)[:, None] * TOP_K + jnp.arange(HEAD_DIM, dtype=I32)[None, :]) % N_EXPERTS
    x1_tail = tail(x1_s, jnp.zeros((pad, D_MODEL), F32))
    x1 = lax.dynamic_update_slice(x1, x1_tail, (N_PROMPT, 0))
    x1_rows = lax.dynamic_update_slice(x1_rows, x1_tail.reshape(-1, LANES), (N_PROMPT * TOKEN_ROWS, 0))
    top_e = lax.dynamic_update_slice(top_e, tail(top_e_s, pad_e), (N_PROMPT, 0))
    gate = lax.dynamic_update_slice(gate, tail(gate_s, jnp.zeros((pad, HEAD_DIM), F32)), (N_PROMPT, 0))

    dest, slot_tok, blk_expert, n_used = _moe_layout(top_e)
    hdn = _expert_gate_up(blk_expert, n_used, slot_tok, x1_rows, layer(w_gate_up),
                          b_gate_up.reshape(N_EXPERTS, 1, 2 * D_FF))
    y_slots = _expert_down(blk_expert, n_used, hdn, layer(w_down), b_down.reshape(N_EXPERTS, 1, D_MODEL))
    y_p, y_tail = _combine(dest, y_slots, gate, x1, row(ln2_g), row(ln2_b))

    y_prompt = y_p.reshape(BATCH, SEQ, D_MODEL)
    y_sample = y_tail[:DEC_BATCH].reshape(DEC_BATCH, 1, D_MODEL)
    p_kv = _prompt_kv_caches(k_rot, qkv)
    s_kv = []
    for g in range(len(WINDOWS)):
        v_cols = slice(2 * ATT_WIDTH + g * ATT_OUT, 2 * ATT_WIDTH + (g + 1) * ATT_OUT)
        ks = k_rot_s[:, g * ATT_OUT:(g + 1) * ATT_OUT].reshape(DEC_BATCH, 1, HEADS, HEAD_DIM)
        vs = h_s_in[:, v_cols].reshape(DEC_BATCH, 1, HEADS, HEAD_DIM)
        s_kv.append(jnp.stack([ks, vs], axis=2)[None])
    p_mem_kv = mem_kv_p.reshape(1, BATCH, MEM_TOKENS, 2, MEM_HEADS, MEM_HEAD_DIM)
    p_conv = ug.reshape(BATCH, SEQ, UG_WIDTH)[:, SEQ - (CONV_W - 1):, :LRU_WIDTH][None]
    p_lru_h = h_last_p.reshape(1, BATCH, LRU_WIDTH)
    u_s = h_s_in[:, qkv_cols:qkv_cols + LRU_WIDTH]
    s_conv = jnp.concatenate([layer(state_conv)[:, 1:], u_s[:, None, :]], axis=1)[None]
    s_lru_h = h_s[None]
    return (y_prompt, y_sample, p_kv[0], p_kv[1], p_kv[2], p_mem_kv, p_conv, p_lru_h,
            s_kv[0], s_kv[1], s_kv[2], s_conv, s_lru_h)
```

```python
import jax
import jax.numpy as jnp
from jax import lax
from jax.experimental import pallas as pl
from jax.experimental.pallas import tpu as pltpu

F32 = jnp.float32
BF16 = jnp.bfloat16
I32 = jnp.int32

D_MODEL = 2048
BATCH = 8
SEQ = 2048
DEC_BATCH = 32
PAST_LEN = 8192
HEAD_DIM = 128
HEADS = 4
DILATIONS = (1, 4, 16)
WINDOWS = (128, 512, 2048)
ATT_BLOCK = 128
ATT_WIDTH = 1536
ATT_OUT = 512
ROT_DIM = 32
ROPE_THETA = 500000.0
LRU_WIDTH = 1536
LRU_BLOCK = 128
LRU_C = 8.0
CONV_W = 4
MEM_TOKENS = 256
MEM_HEADS = 4
MEM_HEAD_DIM = 256
MEM_WIDTH = 1024
N_EXPERTS = 32
TOP_K = 4
D_FF = 2048
SWIGLU_LIMIT = 7.0
SWIGLU_ALPHA = 1.702
LN_EPS = 1e-5
DN_ALPHA = 2.0 ** 0.25

N_PROMPT = BATCH * SEQ
OUTPROJ_ROWS = 256
TOKENS = N_PROMPT + OUTPROJ_ROWS
UG_WIDTH = 2 * LRU_WIDTH + MEM_WIDTH

V7X_VMEM_LIMIT = 56 * 1024 * 1024

LRU_COLS = 512
LRU_ROWS = 256
SUBLANES = 8
MOE_ROWS = 256
MOE_BLOCKS = TOKENS * TOP_K // MOE_ROWS + N_EXPERTS
MOE_SLOTS = MOE_BLOCKS * MOE_ROWS
FF_TILE = 1024
COMBINE_TOKENS = 128
GATHER_DEPTH = 3
LANES = 128
TOKEN_ROWS = D_MODEL // LANES


def _params(*sem):
    return pltpu.CompilerParams(dimension_semantics=sem, vmem_limit_bytes=V7X_VMEM_LIMIT)


def _bdot(a, b):
    return jnp.dot(a.astype(BF16), b.astype(BF16), preferred_element_type=F32)


def _hdot(a, b):
    return jnp.dot(a, b, precision=lax.Precision.HIGHEST, preferred_element_type=F32)


def _round_bf16(x):
    return x.astype(BF16).astype(F32)


def _keep_f32(x):
    return x


def _sigmoid(x):
    return 1.0 / (1.0 + jnp.exp(-x))


def _gelu_tanh(x):
    return 0.5 * x * (1.0 + jnp.tanh(0.7978845608028654 * (x + 0.044715 * (x * x * x))))


def _layer_norm(x, g, b):
    mu = jnp.mean(x, axis=-1, keepdims=True)
    xc = x - mu
    var = jnp.mean(xc * xc, axis=-1, keepdims=True)
    return xc * lax.rsqrt(var + LN_EPS) * g + b


def _mm_kernel(x_ref, w_ref, o_ref):
    o_ref[...] = _bdot(x_ref[...], w_ref[...]).astype(o_ref.dtype)


def _mm_hi_kernel(x_ref, w_ref, o_ref):
    o_ref[...] = _hdot(x_ref[...], w_ref[...])


def _matmul(x, w, tm, tn, name, full_precision=False):
    m, k = x.shape
    n = w.shape[1]
    return pl.pallas_call(
        _mm_hi_kernel if full_precision else _mm_kernel,
        out_shape=jax.ShapeDtypeStruct((m, n), F32),
        grid=(n // tn, m // tm),
        in_specs=[pl.BlockSpec((tm, k), lambda j, i: (i, 0)),
                  pl.BlockSpec((k, tn), lambda j, i: (0, j))],
        out_specs=pl.BlockSpec((tm, tn), lambda j, i: (i, j)),
        compiler_params=_params("arbitrary", "arbitrary"),
        name=name,
    )(x, w)


def _rope_tables(pos):
    half = ROT_DIM // 2
    inv_freq = ROPE_THETA ** (-jnp.arange(half, dtype=F32) / half)
    ang = pos.astype(F32)[:, None] * inv_freq[None, :]
    cos, sin = jnp.cos(ang), jnp.sin(ang)
    t = pos.shape[0]
    rest = HEAD_DIM - ROT_DIM
    c = jnp.concatenate([cos, cos, jnp.ones((t, rest), F32)], axis=1)
    sa = jnp.concatenate([-sin, jnp.zeros((t, half + rest), F32)], axis=1)
    sb = jnp.concatenate([jnp.zeros((t, half), F32), sin, jnp.zeros((t, rest), F32)], axis=1)
    return c, sa, sb


def _rope(x, c, sa, sb):
    half = ROT_DIM // 2
    return x * c + pltpu.roll(x, HEAD_DIM - half, 1) * sa + pltpu.roll(x, half, 1) * sb


def _group_merge(lses, outs, operand=_round_bf16):
    m = jnp.maximum(jnp.maximum(lses[0], lses[1]), lses[2])
    es = [jnp.exp(l - m) for l in lses]
    den = es[0] + es[1] + es[2]
    y = None
    for e, o in zip(es, outs):
        term = operand(e / den) * operand(o)
        y = term if y is None else y + term
    return y


def _attn_kernel(q0, q1, q2, k0, k1, k2, v0, v1, v2, c_ref, sa_ref, sb_ref,
                 y_ref, kr0, kr1, kr2, qs, os_, ls):
    c, sa, sb = c_ref[...], sa_ref[...], sb_ref[...]
    scale = HEAD_DIM ** -0.5
    ii = lax.broadcasted_iota(I32, (ATT_BLOCK, ATT_BLOCK), 0)
    jj = lax.broadcasted_iota(I32, (ATT_BLOCK, ATT_BLOCK), 1)
    cur_mask = jj <= ii
    prev_mask = jj >= ii
    dn = (((1,), (1,)), ((), ()))
    for g, (q_ref, k_ref, v_ref, kr_ref) in enumerate(((q0, k0, v0, kr0), (q1, k1, v1, kr1), (q2, k2, v2, kr2))):
        dil = DILATIONS[g]
        qs[...] = _rope(q_ref[...], c, sa, sb)
        kr_ref[...] = _rope(k_ref[...], c, sa, sb)
        o_g, l_g = os_.at[g], ls.at[g]

        def rows_of(start, dil=dil):
            return pl.ds(start, ATT_BLOCK, stride=dil) if dil > 1 else pl.ds(start, ATT_BLOCK)

        for cls in range(dil):
            for b in range(SEQ // (dil * ATT_BLOCK)):
                start = cls + dil * ATT_BLOCK * b
                rows = rows_of(start)
                q = qs[rows, :].astype(BF16)
                kc = kr_ref[rows, :].astype(BF16)
                vc = v_ref[rows, :].astype(BF16)
                s_c = lax.dot_general(q, kc, dn, preferred_element_type=F32) * scale
                s_c = jnp.where(cur_mask, s_c, -jnp.inf)
                m = jnp.max(s_c, axis=-1, keepdims=True)
                if b > 0:
                    prows = rows_of(start - dil * ATT_BLOCK)
                    kp = kr_ref[prows, :].astype(BF16)
                    vp = v_ref[prows, :].astype(BF16)
                    s_p = lax.dot_general(q, kp, dn, preferred_element_type=F32) * scale
                    s_p = jnp.where(prev_mask, s_p, -jnp.inf)
                    m = jnp.maximum(m, jnp.max(s_p, axis=-1, keepdims=True))
                    p_p = jnp.exp(s_p - m)
                p_c = jnp.exp(s_c - m)
                den = jnp.sum(p_c, axis=-1, keepdims=True)
                if b > 0:
                    den = den + jnp.sum(p_p, axis=-1, keepdims=True)
                o = jnp.dot((p_c / den).astype(BF16), vc, preferred_element_type=F32)
                if b > 0:
                    o = o + jnp.dot((p_p / den).astype(BF16), vp, preferred_element_type=F32)
                lse = m + jnp.log(den)
                o_g[rows, :] = o
                l_g[rows, :] = jnp.broadcast_to(lse, (ATT_BLOCK, HEAD_DIM))
    y = _group_merge([ls[0], ls[1], ls[2]], [os_[0], os_[1], os_[2]])
    y_ref[...] = y.astype(y_ref.dtype)


def _prompt_attention(qkv, c, sa, sb):
    blk = (SEQ, HEAD_DIM)
    head_cols = ATT_WIDTH // HEAD_DIM

    def col(base, g):
        return pl.BlockSpec(blk, lambda n, h: (n, base + g * HEADS + h))

    in_specs = ([col(0, g) for g in range(3)] + [col(head_cols, g) for g in range(3)]
                + [col(2 * head_cols, g) for g in range(3)] + [pl.BlockSpec(blk, lambda n, h: (0, 0))] * 3)
    res = pl.pallas_call(
        _attn_kernel,
        out_shape=[jax.ShapeDtypeStruct((N_PROMPT, ATT_OUT), BF16)]
        + [jax.ShapeDtypeStruct((N_PROMPT, ATT_OUT), F32)] * 3,
        grid=(BATCH, HEADS),
        in_specs=in_specs,
        out_specs=[pl.BlockSpec(blk, lambda n, h: (n, h))] * 4,
        scratch_shapes=[pltpu.VMEM(blk, F32), pltpu.VMEM((3,) + blk, F32), pltpu.VMEM((3,) + blk, F32)],
        compiler_params=_params("arbitrary", "arbitrary"),
        name="prompt_attention",
    )(*([qkv] * 9), c, sa, sb)
    return res[0], res[1:]


KV_ROWS = 256


def _kv_cache_kernel(k0, k1, k2, v0, v1, v2, o0, o1, o2):
    tb = pl.program_id(1)

    def put(o_ref, k, v):
        for h in range(HEADS):
            o_ref[:, 0, h, :] = k[:, h * HEAD_DIM:(h + 1) * HEAD_DIM]
            o_ref[:, 1, h, :] = v[:, h * HEAD_DIM:(h + 1) * HEAD_DIM]

    for (k_ref, v_ref, o_ref), window in zip(((k0, v0, o0), (k1, v1, o1), (k2, v2, o2)), WINDOWS):
        keep = min(window, SEQ)
        if keep >= KV_ROWS:
            first = (SEQ - keep) // KV_ROWS
            pl.when(tb >= first)(lambda k_ref=k_ref, v_ref=v_ref, o_ref=o_ref: put(o_ref, k_ref[...], v_ref[...]))
        else:
            pl.when(tb == SEQ // KV_ROWS - 1)(
                lambda k_ref=k_ref, v_ref=v_ref, o_ref=o_ref, keep=keep:
                put(o_ref, k_ref[KV_ROWS - keep:, :], v_ref[KV_ROWS - keep:, :]))


def _prompt_kv_caches(k_rot, qkv):
    steps = SEQ // KV_ROWS
    v_col = 2 * ATT_WIDTH // ATT_OUT
    out_shapes, out_specs = [], []
    for window in WINDOWS:
        keep = min(window, SEQ)
        rows = min(keep, KV_ROWS)
        first = (SEQ - keep) // KV_ROWS
        out_shapes.append(jax.ShapeDtypeStruct((1, BATCH, keep, 2, HEADS, HEAD_DIM), F32))
        out_specs.append(pl.BlockSpec((None, None, rows, 2, HEADS, HEAD_DIM),
                                      lambda n, tb, first=first: (0, n, jnp.maximum(tb - first, 0), 0, 0, 0)))
    return pl.pallas_call(
        _kv_cache_kernel,
        out_shape=out_shapes,
        grid=(BATCH, steps),
        in_specs=[pl.BlockSpec((KV_ROWS, ATT_OUT), lambda n, tb: (n * steps + tb, 0))] * 3
        + [pl.BlockSpec((KV_ROWS, ATT_OUT), lambda n, tb, g=g: (n * steps + tb, v_col + g)) for g in range(3)],
        out_specs=out_specs,
        compiler_params=_params("arbitrary", "arbitrary"),
        name="prompt_kv_caches",
    )(*k_rot, qkv, qkv, qkv)


def _lru_gates(uc, wa_ref, ba, wx_ref, bx, lam, dot=_bdot):
    n_blk = uc.shape[1] // LRU_BLOCK
    r = jnp.concatenate([dot(uc[:, j * LRU_BLOCK:(j + 1) * LRU_BLOCK], wa_ref[j]) for j in range(n_blk)], axis=1)
    i = jnp.concatenate([dot(uc[:, j * LRU_BLOCK:(j + 1) * LRU_BLOCK], wx_ref[j]) for j in range(n_blk)], axis=1)
    r = _sigmoid(r + ba)
    i = _sigmoid(i + bx)
    neg = -lam
    softplus = jnp.maximum(neg, 0.0) + jnp.log1p(jnp.exp(-jnp.abs(neg)))
    log_a = (-LRU_C * r) * softplus
    a = jnp.exp(log_a)
    b = jnp.sqrt(-jnp.tanh(log_a) * (a * a + 1.0)) * (i * uc)
    return a, b


def _lru_kernel(u_ref, g_ref, cw_ref, cb_ref, wa_ref, ba_ref, wx_ref, bx_ref, lam_ref,
                y_ref, hl_ref, hist, carry):
    @pl.when(pl.program_id(2) == 0)
    def _():
        hist[...] = jnp.zeros_like(hist)
        carry[...] = jnp.zeros_like(carry)

    u = u_ref[...]
    cw = cw_ref[...]
    ext = jnp.concatenate([hist[...], u], axis=0)
    off = SUBLANES - (CONV_W - 1)
    conv = ext[off:off + LRU_ROWS] * cw[0:1]
    for t in range(1, CONV_W):
        conv = conv + ext[off + t:off + t + LRU_ROWS] * cw[t:t + 1]
    uc = cb_ref[...] + conv
    hist[...] = u[LRU_ROWS - SUBLANES:]

    a, b = _lru_gates(uc, wa_ref, ba_ref[...], wx_ref, bx_ref[...], lam_ref[...])

    step = lax.broadcasted_iota(I32, a.shape, 0) & (SUBLANES - 1)
    for s in (1, 2, 4):
        a_s = pltpu.roll(a, s, 0)
        b_s = pltpu.roll(b, s, 0)
        valid = step >= s
        b = jnp.where(valid, a * b_s + b, b)
        a = jnp.where(valid, a * a_s, a)
    h_prev = carry[0:1, :]
    hs = []
    for j in range(LRU_ROWS // SUBLANES):
        h_j = a[j * SUBLANES:(j + 1) * SUBLANES] * h_prev + b[j * SUBLANES:(j + 1) * SUBLANES]
        h_prev = h_j[SUBLANES - 1:SUBLANES]
        hs.append(h_j)
    h = jnp.concatenate(hs, axis=0)
    carry[...] = jnp.broadcast_to(h_prev, carry.shape)
    y_ref[...] = (h * _gelu_tanh(g_ref[...])).astype(y_ref.dtype)
    hl_ref[...] = h_prev


def _prompt_lru(ug, conv_w, conv_b, wa, ba, wx, bx, lam):
    t_blocks = SEQ // LRU_ROWS
    c_blocks = LRU_WIDTH // LRU_COLS
    row = pl.BlockSpec((1, LRU_COLS), lambda n, cb, tb: (0, cb))
    gate_w = pl.BlockSpec((LRU_COLS // LRU_BLOCK, LRU_BLOCK, LRU_BLOCK), lambda n, cb, tb: (cb, 0, 0))
    return pl.pallas_call(
        _lru_kernel,
        out_shape=[jax.ShapeDtypeStruct((N_PROMPT, LRU_WIDTH), BF16),
                   jax.ShapeDtypeStruct((BATCH, 1, LRU_WIDTH), F32)],
        grid=(BATCH, c_blocks, t_blocks),
        in_specs=[pl.BlockSpec((LRU_ROWS, LRU_COLS), lambda n, cb, tb: (n * t_blocks + tb, cb)),
                  pl.BlockSpec((LRU_ROWS, LRU_COLS), lambda n, cb, tb: (n * t_blocks + tb, c_blocks + cb)),
                  pl.BlockSpec((CONV_W, LRU_COLS), lambda n, cb, tb: (0, cb)),
                  row, gate_w, row, gate_w, row, row],
        out_specs=[pl.BlockSpec((LRU_ROWS, LRU_COLS), lambda n, cb, tb: (n * t_blocks + tb, cb)),
                   pl.BlockSpec((None, 1, LRU_COLS), lambda n, cb, tb: (n, 0, cb))],
        scratch_shapes=[pltpu.VMEM((SUBLANES, LRU_COLS), F32), pltpu.VMEM((SUBLANES, LRU_COLS), F32)],
        compiler_params=_params("arbitrary", "arbitrary", "arbitrary"),
        name="prompt_lru",
    )(ug, ug, conv_w, conv_b, wa, ba, wx, bx, lam)


def _sample_lru_kernel(h_ref_in, sc_ref, h0_ref, cw_ref, cb_ref, wa_ref, ba_ref, wx_ref, bx_ref, lam_ref,
                       y_ref, h_ref):
    u = h_ref_in[:, 3 * ATT_WIDTH:3 * ATT_WIDTH + LRU_WIDTH]
    g = h_ref_in[:, 3 * ATT_WIDTH + LRU_WIDTH:3 * ATT_WIDTH + 2 * LRU_WIDTH]
    cw = cw_ref[...]
    conv = sc_ref[:, 0:LRU_WIDTH] * cw[0:1]
    for t in range(1, CONV_W - 1):
        conv = conv + sc_ref[:, t * LRU_WIDTH:(t + 1) * LRU_WIDTH] * cw[t:t + 1]
    conv = conv + u * cw[CONV_W - 1:CONV_W]
    uc = cb_ref[...] + conv
    a, b = _lru_gates(uc, wa_ref, ba_ref[...], wx_ref, bx_ref[...], lam_ref[...], dot=_hdot)
    h = b + a * h0_ref[...]
    h_ref[...] = h
    y_ref[...] = h * _gelu_tanh(g)


def _sample_lru(h_in, state_conv, h0, conv_w, conv_b, wa, ba, wx, bx, lam):
    args = (h_in, state_conv, h0, conv_w, conv_b, wa, ba, wx, bx, lam)
    full = lambda shape: pl.BlockSpec(shape, lambda i: (0,) * len(shape))
    return pl.pallas_call(
        _sample_lru_kernel,
        out_shape=[jax.ShapeDtypeStruct((DEC_BATCH, LRU_WIDTH), F32)] * 2,
        grid=(1,),
        in_specs=[full(a.shape) for a in args],
        out_specs=[full((DEC_BATCH, LRU_WIDTH))] * 2,
        compiler_params=_params("arbitrary"),
        name="sample_lru",
    )(*args)


def _mem_attn_kernel(q_ref, k_ref, v_ref, y_ref):
    dn = (((1,), (1,)), ((), ()))
    s = lax.dot_general(q_ref[...].astype(BF16), k_ref[...].astype(BF16), dn,
                        preferred_element_type=F32) * (MEM_HEAD_DIM ** -0.5)
    m = jnp.max(s, axis=-1, keepdims=True)
    p = jnp.exp(s - m)
    p = p / jnp.sum(p, axis=-1, keepdims=True)
    y_ref[...] = _bdot(p, v_ref[...]).astype(y_ref.dtype)


def _prompt_mem_attention(ug, mem_kv):
    rows = 1024
    r_blocks = SEQ // rows
    q_col = 2 * LRU_WIDTH // MEM_HEAD_DIM
    return pl.pallas_call(
        _mem_attn_kernel,
        out_shape=jax.ShapeDtypeStruct((N_PROMPT, MEM_WIDTH), BF16),
        grid=(BATCH, MEM_HEADS, r_blocks),
        in_specs=[pl.BlockSpec((rows, MEM_HEAD_DIM), lambda n, h, r: (n * r_blocks + r, q_col + h)),
                  pl.BlockSpec((MEM_TOKENS, MEM_HEAD_DIM), lambda n, h, r: (n, h)),
                  pl.BlockSpec((MEM_TOKENS, MEM_HEAD_DIM), lambda n, h, r: (n, MEM_HEADS + h))],
        out_specs=pl.BlockSpec((rows, MEM_HEAD_DIM), lambda n, h, r: (n * r_blocks + r, h)),
        compiler_params=_params("arbitrary", "arbitrary", "arbitrary"),
        name="prompt_mem_attention",
    )(ug, mem_kv, mem_kv)


def _heads_attend(q, keys, values, scale, extra=None):
    s = jnp.sum(keys * q[None], axis=-1, keepdims=True) * scale
    m = jnp.max(s, axis=0)
    if extra is not None:
        s_x = jnp.sum(extra[0] * q, axis=-1, keepdims=True) * scale
        m = jnp.maximum(m, s_x)
        p_x = jnp.exp(s_x - m)
    p = jnp.exp(s - m[None])
    den = jnp.sum(p, axis=0)
    if extra is not None:
        den = den + p_x
    o = jnp.sum((p / den[None]) * values, axis=0)
    if extra is not None:
        o = o + (p_x / den) * extra[1]
    return o, m + jnp.log(den)


def _sample_attn_kernel(h_ref, c_ref, sa_ref, sb_ref, w0_ref, w1_ref, w2_ref, mem_ref, ya_ref, ym_ref, kr_ref):
    c, sa, sb = c_ref[...], sa_ref[...], sb_ref[...]
    outs, lses = [], []
    for g, w_ref in enumerate((w0_ref, w1_ref, w2_ref)):
        q, k_new, v_new = [], [], []
        for h in range(HEADS):
            col = (g * HEADS + h) * HEAD_DIM
            q.append(_rope(h_ref[:, col:col + HEAD_DIM], c, sa, sb))
            k_new.append(_rope(h_ref[:, ATT_WIDTH + col:ATT_WIDTH + col + HEAD_DIM], c, sa, sb))
            v_new.append(h_ref[:, 2 * ATT_WIDTH + col:2 * ATT_WIDTH + col + HEAD_DIM])
            kr_ref[:, col:col + HEAD_DIM] = k_new[h]
        stack = lambda rows: jnp.concatenate(rows, axis=0)
        o, lse = _heads_attend(stack(q), w_ref[:, 0], w_ref[:, 1], HEAD_DIM ** -0.5,
                               extra=(stack(k_new), stack(v_new)))
        outs.append(o)
        lses.append(lse)
    y = _group_merge(lses, outs, operand=_keep_f32)
    for h in range(HEADS):
        ya_ref[:, h * HEAD_DIM:(h + 1) * HEAD_DIM] = y[h:h + 1]
    q_col = 3 * ATT_WIDTH + 2 * LRU_WIDTH
    qm = jnp.concatenate([h_ref[:, q_col + h * MEM_HEAD_DIM:q_col + (h + 1) * MEM_HEAD_DIM]
                          for h in range(MEM_HEADS)], axis=0)
    om, _ = _heads_attend(qm, mem_ref[:, 0], mem_ref[:, 1], MEM_HEAD_DIM ** -0.5)
    for h in range(MEM_HEADS):
        ym_ref[:, h * MEM_HEAD_DIM:(h + 1) * MEM_HEAD_DIM] = om[h:h + 1]


def _sample_attention(h_in, c, sa, sb, caches, cache_mem):
    full = lambda shape: pl.BlockSpec(shape, lambda i: (0,) * len(shape))
    row = lambda w: pl.BlockSpec((None, 1, w), lambda i: (i, 0, 0))
    window = pl.BlockSpec((None, ATT_BLOCK, None, 2, HEADS, HEAD_DIM), lambda i: (i, 0, 0, 0, 0, 0))
    widths = (ATT_OUT, MEM_WIDTH, ATT_WIDTH)
    y_att, y_mem, k_rot = pl.pallas_call(
        _sample_attn_kernel,
        out_shape=[jax.ShapeDtypeStruct((DEC_BATCH, 1, w), F32) for w in widths],
        grid=(DEC_BATCH,),
        in_specs=[row(h_in.shape[1]), full(c.shape), full(sa.shape), full(sb.shape), window, window, window,
                  pl.BlockSpec((None, MEM_TOKENS, 2, MEM_HEADS, MEM_HEAD_DIM), lambda i: (i, 0, 0, 0, 0))],
        out_specs=[row(w) for w in widths],
        compiler_params=_params("arbitrary"),
        name="sample_attention",
    )(h_in.reshape(DEC_BATCH, 1, -1), c, sa, sb, *caches, cache_mem)
    return [a.reshape(DEC_BATCH, -1) for a in (y_att, y_mem, k_rot)]


def _merge_kernel(ya_ref, yl_ref, ym_ref, ga_ref, gl_ref, gm_ref, pa_ref, pl_ref, pm_ref, o_ref):
    merged = _sigmoid(ga_ref[...]) * _bdot(ya_ref[...], pa_ref[...])
    merged = merged + _sigmoid(gl_ref[...]) * _bdot(yl_ref[...], pl_ref[...])
    merged = merged + _sigmoid(gm_ref[...]) * _bdot(ym_ref[...], pm_ref[...])
    o_ref[...] = merged.astype(o_ref.dtype)


def _merge(y_att, y_lru, y_mem, gates, p_att, p_lru, p_mem):
    tm = 256
    rows = lambda w: pl.BlockSpec((tm, w), lambda i: (i, 0))
    full = lambda a: pl.BlockSpec(a.shape, lambda i: (0, 0))
    return pl.pallas_call(
        _merge_kernel,
        out_shape=jax.ShapeDtypeStruct((N_PROMPT, D_MODEL), BF16),
        grid=(N_PROMPT // tm,),
        in_specs=[rows(ATT_OUT), rows(LRU_WIDTH), rows(MEM_WIDTH)]
        + [pl.BlockSpec((tm, D_MODEL), lambda i, b=b: (i, b)) for b in range(3)]
        + [full(p_att), full(p_lru), full(p_mem)],
        out_specs=rows(D_MODEL),
        compiler_params=_params("arbitrary"),
        name="branch_merge",
    )(y_att, y_lru, y_mem, gates, gates, gates, p_att, p_lru, p_mem)


def _sample_merge_kernel(h_ref, pa_ref, pl_ref, pm_ref, o_ref):
    gate_col = 3 * ATT_WIDTH + UG_WIDTH
    gate = lambda b: _sigmoid(h_ref[:, gate_col + b * D_MODEL:gate_col + (b + 1) * D_MODEL])
    o_ref[...] = (gate(0) * pa_ref[...] + gate(1) * pl_ref[...]) + gate(2) * pm_ref[...]


def _sample_merge(h_in, pa, pl_, pm):
    args = (h_in, pa, pl_, pm)
    full = lambda a: pl.BlockSpec(a.shape, lambda i: (0, 0))
    return pl.pallas_call(
        _sample_merge_kernel,
        out_shape=jax.ShapeDtypeStruct((DEC_BATCH, D_MODEL), F32),
        grid=(1,),
        in_specs=[full(a) for a in args],
        out_specs=pl.BlockSpec((DEC_BATCH, D_MODEL), lambda i: (0, 0)),
        compiler_params=_params("arbitrary"),
        name="sample_merge",
    )(*args)


def _ln_route(mix, x_ref, g_ref, b_ref, wr_ref, br_ref, x1_ref, e_ref, gt_ref, dot):
    x1 = _layer_norm(DN_ALPHA * x_ref[...] + mix, g_ref[...], b_ref[...])
    x1_ref[...] = x1
    logits = dot(x1, wr_ref[...]) + br_ref[...]
    lane = lax.broadcasted_iota(I32, logits.shape, 1).astype(F32)
    out_lane = lax.broadcasted_iota(I32, e_ref.shape, 1)
    top_v = []
    e_out = jnp.zeros(e_ref.shape, I32)
    for k in range(TOP_K):
        v = jnp.max(logits, axis=-1, keepdims=True)
        e = jnp.min(jnp.where(logits == v, lane, float(N_EXPERTS)), axis=-1, keepdims=True)
        logits = jnp.where(lane == e, -jnp.inf, logits)
        top_v.append(v)
        e_out = jnp.where(out_lane == k, e.astype(I32), e_out)
    ps = [jnp.exp(v - top_v[0]) for v in top_v]
    den = ps[0] + ps[1] + ps[2] + ps[3]
    g_out = jnp.zeros(gt_ref.shape, F32)
    for k in range(TOP_K):
        g_out = jnp.where(out_lane == k, ps[k] / den, g_out)
    e_ref[...] = e_out
    gt_ref[...] = g_out


def _outproj_kernel(m_ref, x_ref, w_ref, g_ref, b_ref, wr_ref, br_ref, x1_ref, x1r_ref, e_ref, gt_ref):
    @pl.when(pl.program_id(0) < N_PROMPT // OUTPROJ_ROWS)
    def _():
        mix = jnp.dot(m_ref[...], w_ref[...], preferred_element_type=F32)
        _ln_route(mix, x_ref, g_ref, b_ref, wr_ref, br_ref, x1_ref, e_ref, gt_ref, _bdot)
        _to_token_rows(x1_ref[...], x1r_ref, OUTPROJ_ROWS)

    @pl.when(pl.program_id(0) >= N_PROMPT // OUTPROJ_ROWS)
    def _():
        for ref in (x1_ref, x1r_ref, e_ref, gt_ref):
            ref[...] = jnp.zeros_like(ref)


def _sample_ln_router_kernel(mix_ref, x_ref, g_ref, b_ref, wr_ref, br_ref, x1_ref, e_ref, gt_ref):
    _ln_route(mix_ref[...], x_ref, g_ref, b_ref, wr_ref, br_ref, x1_ref, e_ref, gt_ref, _hdot)


def _sample_ln_router(mix, x, ln_g, ln_b, w_router, b_router):
    args = (mix, x, ln_g, ln_b, w_router, b_router)
    full = lambda shape: pl.BlockSpec(shape, lambda i: (0, 0))
    out_shapes = [jax.ShapeDtypeStruct((DEC_BATCH, D_MODEL), F32), jax.ShapeDtypeStruct((DEC_BATCH, HEAD_DIM), I32),
                  jax.ShapeDtypeStruct((DEC_BATCH, HEAD_DIM), F32)]
    return pl.pallas_call(
        _sample_ln_router_kernel,
        out_shape=out_shapes,
        grid=(1,),
        in_specs=[full(a.shape) for a in args],
        out_specs=[full(s.shape) for s in out_shapes],
        compiler_params=_params("arbitrary"),
        name="sample_ln_router",
    )(*args)


def _outproj_router(merged, x, w_out, ln_g, ln_b, w_router, b_router):
    tm = OUTPROJ_ROWS
    last = N_PROMPT // tm - 1
    in_rows = lambda w: pl.BlockSpec((tm, w), lambda i: (jnp.minimum(i, last), 0))
    rows = lambda w: pl.BlockSpec((tm, w), lambda i: (i, 0))
    full = lambda a: pl.BlockSpec(a.shape, lambda i: (0, 0))
    return pl.pallas_call(
        _outproj_kernel,
        out_shape=[jax.ShapeDtypeStruct((TOKENS, D_MODEL), F32),
                   jax.ShapeDtypeStruct((TOKENS * TOKEN_ROWS, LANES), F32),
                   jax.ShapeDtypeStruct((TOKENS, HEAD_DIM), I32),
                   jax.ShapeDtypeStruct((TOKENS, HEAD_DIM), F32)],
        grid=(TOKENS // tm,),
        in_specs=[in_rows(D_MODEL), in_rows(D_MODEL), full(w_out), full(ln_g), full(ln_b), full(w_router),
                  full(b_router)],
        out_specs=[rows(D_MODEL), pl.BlockSpec((tm * TOKEN_ROWS, LANES), lambda i: (i, 0)), rows(HEAD_DIM),
                   rows(HEAD_DIM)],
        compiler_params=_params("arbitrary"),
        name="outproj_ln_router",
    )(merged, x, w_out, ln_g, ln_b, w_router, b_router)


def _to_token_rows(x, ref, n):
    for j in range(TOKEN_ROWS):
        ref[pl.ds(j, n, stride=TOKEN_ROWS), :] = x[:, j * LANES:(j + 1) * LANES]


def _from_token_rows(ref, first, n):
    return jnp.concatenate([ref[pl.ds(first * TOKEN_ROWS + j, n, stride=TOKEN_ROWS), :] for j in range(TOKEN_ROWS)],
                           axis=1)


def _gather_pipeline(t, last, idx_hbm, src_hbm, bufs, idx_smem, idx_sem, row_sem, per_step, variants):
    depth = GATHER_DEPTH

    def idx_copy(step, slot):
        return pltpu.make_async_copy(idx_hbm.at[step], idx_smem.at[slot], idx_sem.at[slot])

    def row_copy(slot, r):
        tok = idx_smem[slot, 0, r]
        return pltpu.make_async_copy(src_hbm.at[pl.ds(tok * TOKEN_ROWS, TOKEN_ROWS)],
                                     bufs[slot].at[pl.ds(r * TOKEN_ROWS, TOKEN_ROWS)], row_sem.at[slot])

    def wait_rows(slot):
        pltpu.make_async_copy(src_hbm.at[pl.ds(0, per_step * TOKEN_ROWS)], bufs[slot], row_sem.at[slot]).wait()

    @pl.when(t == 0)
    def _():
        for s in range(depth - 1):
            idx_copy(s, s).start()
            idx_copy(s, s).wait()

            def first(r, carry, s=s):
                row_copy(s, r).start()
                return carry

            lax.fori_loop(0, per_step, first, 0)
        idx_copy(depth - 1, depth - 1).start()

    def step(slot, body):
        ahead = (slot + depth - 1) % depth
        wait_rows(slot)
        idx_copy(t + depth - 1, ahead).wait()
        for r in range(per_step):
            row_copy(ahead, r).start(priority=r % 2)
        idx_copy(t + depth, slot).start()
        body(bufs[slot])

    for slot in range(depth):
        mine = t % depth == slot
        for cond, body in variants:
            pl.when(mine if cond is None else jnp.logical_and(mine, cond))(
                lambda slot=slot, body=body: step(slot, body))

    @pl.when(t == last)
    def _():
        idx_copy(0, last % depth).wait()
        for s in range(1, depth):
            wait_rows((last + s) % depth)


def _gather_scratch(per_step):
    return ([pltpu.VMEM((per_step * TOKEN_ROWS, LANES), F32)] * GATHER_DEPTH
            + [pltpu.SMEM((GATHER_DEPTH, 1, per_step), I32), pltpu.SemaphoreType.DMA((GATHER_DEPTH,)),
               pltpu.SemaphoreType.DMA((GATHER_DEPTH,))])


def _expert_changed(be_ref, i):
    return jnp.logical_or(i == 0, be_ref[i] != be_ref[jnp.maximum(i - 1, 0)])


def _gate_up_kernel(be_ref, nu_ref, tok_hbm, x_hbm, wg_ref, wu_ref, bg_ref, bu_ref, o_ref,
                    wg_b, wu_b, *gather):
    i = pl.program_id(1)
    t = pl.program_id(0) * MOE_BLOCKS + i
    last = D_FF // FF_TILE * MOE_BLOCKS - 1

    @pl.when(_expert_changed(be_ref, i))
    def _():
        wg_b[...] = wg_ref[...].astype(BF16)
        wu_b[...] = wu_ref[...].astype(BF16)

    def compute(buf):
        x = _from_token_rows(buf, 0, MOE_ROWS).astype(BF16)
        g = jnp.dot(x, wg_b[...], preferred_element_type=F32) + bg_ref[...]
        u = jnp.dot(x, wu_b[...], preferred_element_type=F32) + bu_ref[...]
        g = jnp.minimum(g, SWIGLU_LIMIT)
        u = jnp.clip(u, -SWIGLU_LIMIT, SWIGLU_LIMIT)
        o_ref[...] = ((u + 1.0) * (g * _sigmoid(SWIGLU_ALPHA * g))).astype(o_ref.dtype)

    def unused(buf):
        o_ref[...] = jnp.zeros_like(o_ref)

    used = i < nu_ref[0]
    _gather_pipeline(t, last, tok_hbm, x_hbm, gather[:GATHER_DEPTH], *gather[GATHER_DEPTH:], MOE_ROWS,
                     [(used, compute), (jnp.logical_not(used), unused)])


def _expert_gate_up(blk_expert, n_used, slot_tok, x1_rows, w_gate_up, b_gate_up):
    ff_tiles = D_FF // FF_TILE
    blocks = slot_tok.reshape(MOE_BLOCKS, 1, MOE_ROWS)
    tok_steps = jnp.concatenate([blocks] * ff_tiles + [blocks[:GATHER_DEPTH]], axis=0)
    return pl.pallas_call(
        _gate_up_kernel,
        out_shape=jax.ShapeDtypeStruct((MOE_SLOTS, D_FF), BF16),
        grid_spec=pltpu.PrefetchScalarGridSpec(
            num_scalar_prefetch=2,
            grid=(ff_tiles, MOE_BLOCKS),
            in_specs=[pl.BlockSpec(memory_space=pl.ANY), pl.BlockSpec(memory_space=pl.ANY),
                      pl.BlockSpec((None, D_MODEL, FF_TILE), lambda j, i, be, nu: (be[i], 0, j)),
                      pl.BlockSpec((None, D_MODEL, FF_TILE), lambda j, i, be, nu: (be[i], 0, ff_tiles + j)),
                      pl.BlockSpec((None, 1, FF_TILE), lambda j, i, be, nu: (be[i], 0, j)),
                      pl.BlockSpec((None, 1, FF_TILE), lambda j, i, be, nu: (be[i], 0, ff_tiles + j))],
            out_specs=pl.BlockSpec((MOE_ROWS, FF_TILE), lambda j, i, be, nu: (i, j)),
            scratch_shapes=[pltpu.VMEM((D_MODEL, FF_TILE), BF16), pltpu.VMEM((D_MODEL, FF_TILE), BF16)]
            + _gather_scratch(MOE_ROWS)),
        compiler_params=_params("arbitrary", "arbitrary"),
        name="moe_gate_up",
    )(blk_expert, n_used, tok_steps, x1_rows, w_gate_up, w_gate_up, b_gate_up, b_gate_up)


def _down_kernel(be_ref, nu_ref, h_ref, w_ref, b_ref, o_ref, w_b):
    i = pl.program_id(1)

    @pl.when(_expert_changed(be_ref, i))
    def _():
        w_b[...] = w_ref[...].astype(BF16)

    @pl.when(i < nu_ref[0])
    def _():
        y = jnp.dot(h_ref[...], w_b[...], preferred_element_type=F32) + b_ref[...]
        _to_token_rows(y, o_ref, MOE_ROWS)

    @pl.when(i >= nu_ref[0])
    def _():
        o_ref[...] = jnp.zeros_like(o_ref)


def _expert_down(blk_expert, n_used, hdn, w_down, b_down):
    row_blk = lambda j, i, be, nu: jnp.minimum(i, nu[0] - 1)
    return pl.pallas_call(
        _down_kernel,
        out_shape=jax.ShapeDtypeStruct((MOE_SLOTS * TOKEN_ROWS, LANES), F32),
        grid_spec=pltpu.PrefetchScalarGridSpec(
            num_scalar_prefetch=2,
            grid=(1, MOE_BLOCKS),
            in_specs=[pl.BlockSpec((MOE_ROWS, D_FF), lambda j, i, be, nu: (row_blk(j, i, be, nu), 0)),
                      pl.BlockSpec((None, D_FF, D_MODEL), lambda j, i, be, nu: (be[i], 0, 0)),
                      pl.BlockSpec((None, 1, D_MODEL), lambda j, i, be, nu: (be[i], 0, 0))],
            out_specs=pl.BlockSpec((MOE_ROWS * TOKEN_ROWS, LANES), lambda j, i, be, nu: (i, 0)),
            scratch_shapes=[pltpu.VMEM((D_FF, D_MODEL), BF16)]),
        compiler_params=_params("arbitrary", "arbitrary"),
        name="moe_down",
    )(blk_expert, n_used, hdn, w_down, b_down)


def _combine_kernel(dest_hbm, ys_hbm, gate_ref, x1_ref, g_ref, b_ref, op_ref, ot_ref, *gather):
    i = pl.program_id(0)
    prompt_steps = N_PROMPT // COMBINE_TOKENS

    def compute(buf):
        gate = gate_ref[...]
        ffn = None
        for k in range(TOP_K):
            term = _from_token_rows(buf, k * COMBINE_TOKENS, COMBINE_TOKENS) * gate[:, k:k + 1]
            ffn = term if ffn is None else ffn + term
        y = _layer_norm(DN_ALPHA * x1_ref[...] + ffn, g_ref[...], b_ref[...])

        @pl.when(i < prompt_steps)
        def _():
            op_ref[...] = y

        @pl.when(i >= prompt_steps)
        def _():
            ot_ref[...] = y

    _gather_pipeline(i, TOKENS // COMBINE_TOKENS - 1, dest_hbm, ys_hbm, gather[:GATHER_DEPTH],
                     *gather[GATHER_DEPTH:], COMBINE_TOKENS * TOP_K, [(None, compute)])


def _combine(dest, y_slots, gate, x1, ln_g, ln_b):
    steps = TOKENS // COMBINE_TOKENS
    prompt_steps = N_PROMPT // COMBINE_TOKENS
    per_step = COMBINE_TOKENS * TOP_K
    rows = lambda w: pl.BlockSpec((COMBINE_TOKENS, w), lambda i: (i, 0))
    full = lambda a: pl.BlockSpec(a.shape, lambda i: (0, 0))
    order = dest.reshape(steps, COMBINE_TOKENS, TOP_K).transpose(0, 2, 1).reshape(steps, 1, per_step)
    order = jnp.concatenate([order, order[:GATHER_DEPTH]], axis=0)
    return pl.pallas_call(
        _combine_kernel,
        out_shape=[jax.ShapeDtypeStruct((N_PROMPT, D_MODEL), F32),
                   jax.ShapeDtypeStruct((TOKENS - N_PROMPT, D_MODEL), F32)],
        grid=(steps,),
        in_specs=[pl.BlockSpec(memory_space=pl.ANY), pl.BlockSpec(memory_space=pl.ANY),
                  rows(HEAD_DIM), rows(D_MODEL), full(ln_g), full(ln_b)],
        out_specs=[pl.BlockSpec((COMBINE_TOKENS, D_MODEL), lambda i: (jnp.minimum(i, prompt_steps - 1), 0)),
                   pl.BlockSpec((COMBINE_TOKENS, D_MODEL), lambda i: (jnp.maximum(i - prompt_steps, 0), 0))],
        scratch_shapes=_gather_scratch(per_step),
        compiler_params=_params("arbitrary"),
        name="moe_combine_ln",
    )(order, y_slots, gate, x1, ln_g, ln_b)


def _expert_onehot(e_ref, k):
    e = e_ref[...]
    lane = lax.broadcasted_iota(I32, e.shape, 1)
    return e[:, k:k + 1] == lane


def _rank_kernel(e_ref, tri_ref, rank_ref, cnt_ref, base):
    @pl.when(pl.program_id(0) == 0)
    def _():
        base[...] = jnp.zeros_like(base)

    lane = lax.broadcasted_iota(I32, rank_ref.shape, 1)
    seen = base[0:1, :]
    out = jnp.zeros(rank_ref.shape, F32)
    for k in range(TOP_K):
        hit = _expert_onehot(e_ref, k)
        ones = jnp.where(hit, 1.0, 0.0)
        before = jnp.dot(tri_ref[...], ones.astype(BF16), preferred_element_type=F32) + seen
        out = jnp.where(lane == k, jnp.sum(jnp.where(hit, before, 0.0), axis=-1, keepdims=True), out)
        seen = seen + jnp.sum(ones, axis=0, keepdims=True)
    base[...] = jnp.broadcast_to(seen, base.shape)
    rank_ref[...] = out.astype(I32)
    cnt_ref[...] = jnp.broadcast_to(seen, cnt_ref.shape).astype(I32)


def _dest_kernel(e_ref, rank_ref, start_ref, dest_ref):
    lane = lax.broadcasted_iota(I32, dest_ref.shape, 1)
    out = jnp.zeros(dest_ref.shape, F32)
    for k in range(TOP_K):
        start = jnp.sum(jnp.where(_expert_onehot(e_ref, k), start_ref[0:1, :], 0.0), axis=-1, keepdims=True)
        out = jnp.where(lane == k, start, out)
    dest_ref[...] = out.astype(I32) + rank_ref[...]


def _moe_layout(top_e):
    tm = 512
    steps = -(-TOKENS // tm)
    tri = (jnp.arange(tm)[:, None] > jnp.arange(tm)[None, :]).astype(BF16)
    rows = pl.BlockSpec((tm, LANES), lambda i: (i, 0))
    fixed = lambda shape: pl.BlockSpec(shape, lambda i: (0, 0))
    e_pad = jnp.pad(top_e, ((0, steps * tm - TOKENS), (0, 0)), constant_values=N_EXPERTS)
    rank, counts = pl.pallas_call(
        _rank_kernel,
        out_shape=[jax.ShapeDtypeStruct((steps * tm, LANES), I32), jax.ShapeDtypeStruct((SUBLANES, LANES), I32)],
        grid=(steps,),
        in_specs=[rows, fixed((tm, tm))],
        out_specs=[rows, fixed((SUBLANES, LANES))],
        scratch_shapes=[pltpu.VMEM((SUBLANES, LANES), F32)],
        compiler_params=_params("arbitrary"),
        name="moe_rank",
    )(e_pad, tri)
    counts = counts[0, :N_EXPERTS]
    padded = (counts + MOE_ROWS - 1) // MOE_ROWS * MOE_ROWS
    pad_end = jnp.cumsum(padded)
    starts = jnp.zeros((SUBLANES, LANES), F32).at[:, :N_EXPERTS].set((pad_end - padded).astype(F32))
    dest = pl.pallas_call(
        _dest_kernel,
        out_shape=jax.ShapeDtypeStruct((steps * tm, LANES), I32),
        grid=(steps,),
        in_specs=[rows, rows, fixed((SUBLANES, LANES))],
        out_specs=rows,
        compiler_params=_params("arbitrary"),
        name="moe_dest",
    )(e_pad, rank, starts)[:TOKENS, :TOP_K]
    tok = jnp.broadcast_to(jnp.arange(TOKENS, dtype=I32)[:, None], (TOKENS, TOP_K))
    slot_tok = jnp.zeros((MOE_SLOTS,), I32).at[dest.reshape(-1)].set(tok.reshape(-1), unique_indices=True)
    n_used = pad_end[-1] // MOE_ROWS
    blk = jnp.minimum(jnp.arange(MOE_BLOCKS, dtype=I32), n_used - 1) * MOE_ROWS
    blk_expert = jnp.sum((pad_end[None, :] <= blk[:, None]).astype(I32), axis=1)
    return dest, slot_tok, jnp.minimum(blk_expert, N_EXPERTS - 1), n_used.reshape(1).astype(I32)


def kernel(x_prompt, x_sample, mem_prompt, cache_kv_w128, cache_kv_w512, cache_kv_w2048, cache_mem_kv, state_conv, state_lru_h, w_in, conv_w, conv_b, w_rg_a, b_rg_a, w_rg_x, b_rg_x, lru_lambda, w_mem_kv, p_att, p_lru, p_mem, w_out, ln1_g, ln1_b, w_router, b_router, w_gate_up, b_gate_up, w_down, b_down, ln2_g, ln2_b):
    row = lambda a: a.reshape(1, -1)
    layer = lambda a: a.reshape(a.shape[1:])
    x_p = x_prompt.reshape(N_PROMPT, D_MODEL)
    x_s = x_sample.reshape(DEC_BATCH, D_MODEL)
    xb = x_p.astype(BF16)
    w_in_b = layer(w_in).astype(BF16)
    qkv_cols = 3 * ATT_WIDTH
    gate_col = qkv_cols + UG_WIDTH

    qkv = _matmul(xb, w_in_b[:, :qkv_cols], 1024, 1536, "in_proj_qkv")
    ug = _matmul(xb, w_in_b[:, qkv_cols:gate_col], 1024, 1024, "in_proj_lru_mem")
    gates = _matmul(xb, w_in_b[:, gate_col:], 1024, 1536, "in_proj_gates")

    c_p, sa_p, sb_p = _rope_tables(jnp.arange(SEQ, dtype=I32))
    y_att_p, k_rot = _prompt_attention(qkv, c_p, sa_p, sb_p)
    wa, wx = layer(w_rg_a).astype(BF16), layer(w_rg_x).astype(BF16)
    lru_args = (layer(conv_w), row(conv_b), wa, row(b_rg_a), wx, row(b_rg_x), row(lru_lambda))
    y_lru_p, h_last_p = _prompt_lru(ug, *lru_args)
    mem_kv_p = _matmul(mem_prompt.reshape(BATCH * MEM_TOKENS, D_MODEL), layer(w_mem_kv).astype(BF16), 512, 1024,
                       "mem_kv_proj")
    y_mem_p = _prompt_mem_attention(ug, mem_kv_p)

    merged = _merge(y_att_p, y_lru_p, y_mem_p, gates,
                    layer(p_att).astype(BF16), layer(p_lru).astype(BF16), layer(p_mem).astype(BF16))
    x1, x1_rows, top_e, gate = _outproj_router(merged, x_p, layer(w_out).astype(BF16), row(ln1_g), row(ln1_b),
                                               layer(w_router).astype(BF16), row(b_router))

    h_s_in = _matmul(x_s, layer(w_in), DEC_BATCH, 512, "sample_in_proj", full_precision=True)
    c_s, sa_s, sb_s = _rope_tables(jnp.full((1,), PAST_LEN, I32))
    caches = [cache.reshape(DEC_BATCH, ATT_BLOCK, dil, 2, HEADS, HEAD_DIM)
              for cache, dil in zip((cache_kv_w128, cache_kv_w512, cache_kv_w2048), DILATIONS)]
    y_att_s, y_mem_s, k_rot_s = _sample_attention(h_s_in, c_s, sa_s, sb_s, caches, layer(cache_mem_kv))
    y_lru_s, h_s = _sample_lru(h_s_in, state_conv.reshape(DEC_BATCH, (CONV_W - 1) * LRU_WIDTH), layer(state_lru_h),
                               layer(conv_w), row(conv_b), layer(w_rg_a), row(b_rg_a), layer(w_rg_x), row(b_rg_x),
                               row(lru_lambda))
    hi_mm = lambda a, w, name: _matmul(a, layer(w), DEC_BATCH, 512, name, full_precision=True)
    merged_s = _sample_merge(h_s_in, hi_mm(y_att_s, p_att, "sample_p_att"), hi_mm(y_lru_s, p_lru, "sample_p_lru"),
                             hi_mm(y_mem_s, p_mem, "sample_p_mem"))
    mix_s = hi_mm(merged_s, w_out, "sample_out_proj")
    x1_s, top_e_s, gate_s = _sample_ln_router(mix_s, x_s, row(ln1_g), row(ln1_b), layer(w_router), row(b_router))

    pad = TOKENS - N_PROMPT - DEC_BATCH
    tail = lambda s, p: jnp.concatenate([s, p], axis=0)
    pad_e = (jnp.arange(pad, dtype=I32)[:, None] * TOP_K + jnp.arange(HEAD_DIM, dtype=I32)[None, :]) % N_EXPERTS
    x1_tail = tail(x1_s, jnp.zeros((pad, D_MODEL), F32))
    x1 = lax.dynamic_update_slice(x1, x1_tail, (N_PROMPT, 0))
    x1_rows = lax.dynamic_update_slice(x1_rows, x1_tail.reshape(-1, LANES), (N_PROMPT * TOKEN_ROWS, 0))
    top_e = lax.dynamic_update_slice(top_e, tail(top_e_s, pad_e), (N_PROMPT, 0))
    gate = lax.dynamic_update_slice(gate, tail(gate_s, jnp.zeros((pad, HEAD_DIM), F32)), (N_PROMPT, 0))

    dest, slot_tok, blk_expert, n_used = _moe_layout(top_e)
    hdn = _expert_gate_up(blk_expert, n_used, slot_tok, x1_rows, layer(w_gate_up),
                          b_gate_up.reshape(N_EXPERTS, 1, 2 * D_FF))
    y_slots = _expert_down(blk_expert, n_used, hdn, layer(w_down), b_down.reshape(N_EXPERTS, 1, D_MODEL))
    y_p, y_tail = _combine(dest, y_slots, gate, x1, row(ln2_g), row(ln2_b))

    y_prompt = y_p.reshape(BATCH, SEQ, D_MODEL)
    y_sample = y_tail[:DEC_BATCH].reshape(DEC_BATCH, 1, D_MODEL)
    p_kv = _prompt_kv_caches(k_rot, qkv)
    s_kv = []
    for g in range(len(WINDOWS)):
        v_cols = slice(2 * ATT_WIDTH + g * ATT_OUT, 2 * ATT_WIDTH + (g + 1) * ATT_OUT)
        ks = k_rot_s[:, g * ATT_OUT:(g + 1) * ATT_OUT].reshape(DEC_BATCH, 1, HEADS, HEAD_DIM)
        vs = h_s_in[:, v_cols].reshape(DEC_BATCH, 1, HEADS, HEAD_DIM)
        s_kv.append(jnp.stack([ks, vs], axis=2)[None])
    p_mem_kv = mem_kv_p.reshape(1, BATCH, MEM_TOKENS, 2, MEM_HEADS, MEM_HEAD_DIM)
    p_conv = ug.reshape(BATCH, SEQ, UG_WIDTH)[:, SEQ - (CONV_W - 1):, :LRU_WIDTH][None]
    p_lru_h = h_last_p.reshape(1, BATCH, LRU_WIDTH)
    u_s = h_s_in[:, qkv_cols:qkv_cols + LRU_WIDTH]
    s_conv = jnp.concatenate([layer(state_conv)[:, 1:], u_s[:, None, :]], axis=1)[None]
    s_lru_h = h_s[None]
    return (y_prompt, y_sample, p_kv[0], p_kv[1], p_kv[2], p_mem_kv, p_conv, p_lru_h,
            s_kv[0], s_kv[1], s_kv[2], s_conv, s_lru_h)
```

```python
import jax
import jax.numpy as jnp
from jax import lax
from jax.experimental import pallas as pl
from jax.experimental.pallas import tpu as pltpu

F32 = jnp.float32
BF16 = jnp.bfloat16
I32 = jnp.int32

D_MODEL = 2048
BATCH = 8
SEQ = 2048
DEC_BATCH = 32
PAST_LEN = 8192
HEAD_DIM = 128
HEADS = 4
DILATIONS = (1, 4, 16)
WINDOWS = (128, 512, 2048)
ATT_BLOCK = 128
ATT_WIDTH = 1536
ATT_OUT = 512
ROT_DIM = 32
ROPE_THETA = 500000.0
LRU_WIDTH = 1536
LRU_BLOCK = 128
LRU_C = 8.0
CONV_W = 4
MEM_TOKENS = 256
MEM_HEADS = 4
MEM_HEAD_DIM = 256
MEM_WIDTH = 1024
N_EXPERTS = 32
TOP_K = 4
D_FF = 2048
SWIGLU_LIMIT = 7.0
SWIGLU_ALPHA = 1.702
LN_EPS = 1e-5
DN_ALPHA = 2.0 ** 0.25

N_PROMPT = BATCH * SEQ
OUTPROJ_ROWS = 256
TOKENS = N_PROMPT + OUTPROJ_ROWS
UG_WIDTH = 2 * LRU_WIDTH + MEM_WIDTH

V7X_VMEM_LIMIT = 56 * 1024 * 1024

LRU_COLS = 512
LRU_ROWS = 256
SUBLANES = 8
MOE_ROWS = 256
MOE_BLOCKS = TOKENS * TOP_K // MOE_ROWS + N_EXPERTS
MOE_SLOTS = MOE_BLOCKS * MOE_ROWS
FF_TILE = 1024
COMBINE_TOKENS = 128
GATHER_DEPTH = 3
LANES = 128
TOKEN_ROWS = D_MODEL // LANES


def _params(*sem):
    return pltpu.CompilerParams(dimension_semantics=sem, vmem_limit_bytes=V7X_VMEM_LIMIT)


def _bdot(a, b):
    return jnp.dot(a.astype(BF16), b.astype(BF16), preferred_element_type=F32)


def _hdot(a, b):
    return jnp.dot(a, b, precision=lax.Precision.HIGHEST, preferred_element_type=F32)


def _round_bf16(x):
    return x.astype(BF16).astype(F32)


def _keep_f32(x):
    return x


def _sigmoid(x):
    return 1.0 / (1.0 + jnp.exp(-x))


def _gelu_tanh(x):
    return 0.5 * x * (1.0 + jnp.tanh(0.7978845608028654 * (x + 0.044715 * (x * x * x))))


def _layer_norm(x, g, b):
    mu = jnp.mean(x, axis=-1, keepdims=True)
    xc = x - mu
    var = jnp.mean(xc * xc, axis=-1, keepdims=True)
    return xc * lax.rsqrt(var + LN_EPS) * g + b


def _mm_kernel(x_ref, w_ref, o_ref):
    o_ref[...] = _bdot(x_ref[...], w_ref[...]).astype(o_ref.dtype)


def _mm_hi_kernel(x_ref, w_ref, o_ref):
    o_ref[...] = _hdot(x_ref[...], w_ref[...])


def _matmul(x, w, tm, tn, name, full_precision=False):
    m, k = x.shape
    n = w.shape[1]
    return pl.pallas_call(
        _mm_hi_kernel if full_precision else _mm_kernel,
        out_shape=jax.ShapeDtypeStruct((m, n), F32),
        grid=(n // tn, m // tm),
        in_specs=[pl.BlockSpec((tm, k), lambda j, i: (i, 0)),
                  pl.BlockSpec((k, tn), lambda j, i: (0, j))],
        out_specs=pl.BlockSpec((tm, tn), lambda j, i: (i, j)),
        compiler_params=_params("arbitrary", "arbitrary"),
        name=name,
    )(x, w)


def _rope_tables(pos):
    half = ROT_DIM // 2
    inv_freq = ROPE_THETA ** (-jnp.arange(half, dtype=F32) / half)
    ang = pos.astype(F32)[:, None] * inv_freq[None, :]
    cos, sin = jnp.cos(ang), jnp.sin(ang)
    t = pos.shape[0]
    rest = HEAD_DIM - ROT_DIM
    c = jnp.concatenate([cos, cos, jnp.ones((t, rest), F32)], axis=1)
    sa = jnp.concatenate([-sin, jnp.zeros((t, half + rest), F32)], axis=1)
    sb = jnp.concatenate([jnp.zeros((t, half), F32), sin, jnp.zeros((t, rest), F32)], axis=1)
    return c, sa, sb


def _rope(x, c, sa, sb):
    half = ROT_DIM // 2
    return x * c + pltpu.roll(x, HEAD_DIM - half, 1) * sa + pltpu.roll(x, half, 1) * sb


def _group_merge(lses, outs, operand=_round_bf16):
    m = jnp.maximum(jnp.maximum(lses[0], lses[1]), lses[2])
    es = [jnp.exp(l - m) for l in lses]
    den = es[0] + es[1] + es[2]
    y = None
    for e, o in zip(es, outs):
        term = operand(e / den) * operand(o)
        y = term if y is None else y + term
    return y


def _attn_kernel(q0, q1, q2, k0, k1, k2, v0, v1, v2, c_ref, sa_ref, sb_ref,
                 y_ref, kr0, kr1, kr2, qs, ss, ps, lsub, os_, ls):
    c, sa, sb = c_ref[...], sa_ref[...], sb_ref[...]
    scale = HEAD_DIM ** -0.5
    dn = (((1,), (1,)), ((), ()))
    chunk = 4 * ATT_BLOCK
    for g, (q_ref, k_ref, v_ref, kr_ref) in enumerate(((q0, k0, v0, kr0), (q1, k1, v1, kr1), (q2, k2, v2, kr2))):
        dil = DILATIONS[g]
        nb = SEQ // (dil * ATT_BLOCK)
        qs[...] = _rope(q_ref[...], c, sa, sb)
        kr_ref[...] = _rope(k_ref[...], c, sa, sb)
        o_g, l_g = os_.at[g], ls.at[g]

        def rows_of(start, dil=dil):
            return pl.ds(start, ATT_BLOCK, stride=dil) if dil > 1 else pl.ds(start, ATT_BLOCK)

        blocks = [(cls * nb + b, cls + dil * ATT_BLOCK * b, b) for cls in range(dil) for b in range(nb)]
        for idx, start, b in blocks:
            sub = pl.ds(idx * ATT_BLOCK, ATT_BLOCK)
            q = qs[rows_of(start), :].astype(BF16)
            kc = kr_ref[rows_of(start), :].astype(BF16)
            ss[sub, ATT_BLOCK:] = lax.dot_general(q, kc, dn, preferred_element_type=F32) * scale
            if b > 0:
                kp = kr_ref[rows_of(start - dil * ATT_BLOCK), :].astype(BF16)
                ss[sub, :ATT_BLOCK] = lax.dot_general(q, kp, dn, preferred_element_type=F32) * scale
            else:
                ss[sub, :ATT_BLOCK] = jnp.zeros((ATT_BLOCK, ATT_BLOCK), F32)
        for r0 in range(0, SEQ, chunk):
            row = lax.broadcasted_iota(I32, (chunk, 2 * ATT_BLOCK), 0) + r0
            col = lax.broadcasted_iota(I32, (chunk, 2 * ATT_BLOCK), 1)
            i = row & (ATT_BLOCK - 1)
            has_prev = (row & ((nb - 1) * ATT_BLOCK)) > 0
            in_prev = col < ATT_BLOCK
            valid = jnp.logical_or(jnp.logical_and(in_prev, jnp.logical_and(col >= i, has_prev)),
                                   jnp.logical_and(jnp.logical_not(in_prev), col - ATT_BLOCK <= i))
            s = jnp.where(valid, ss[pl.ds(r0, chunk), :], -jnp.inf)
            m = jnp.max(s, axis=-1, keepdims=True)
            p = jnp.exp(s - m)
            den = jnp.sum(p, axis=-1, keepdims=True)
            ps[pl.ds(r0, chunk), :] = (p / den).astype(BF16)
            lsub[pl.ds(r0, chunk), :] = jnp.broadcast_to(m + jnp.log(den), (chunk, HEAD_DIM))
        for idx, start, b in blocks:
            sub = pl.ds(idx * ATT_BLOCK, ATT_BLOCK)
            vc = v_ref[rows_of(start), :].astype(BF16)
            o = jnp.dot(ps[sub, ATT_BLOCK:], vc, preferred_element_type=F32)
            if b > 0:
                vp = v_ref[rows_of(start - dil * ATT_BLOCK), :].astype(BF16)
                o = o + jnp.dot(ps[sub, :ATT_BLOCK], vp, preferred_element_type=F32)
            o_g[rows_of(start), :] = o
            l_g[rows_of(start), :] = lsub[sub, :]
    y = _group_merge([ls[0], ls[1], ls[2]], [os_[0], os_[1], os_[2]])
    y_ref[...] = y.astype(y_ref.dtype)


def _prompt_attention(qkv, c, sa, sb):
    blk = (SEQ, HEAD_DIM)
    head_cols = ATT_WIDTH // HEAD_DIM

    def col(base, g):
        return pl.BlockSpec(blk, lambda n, h: (n, base + g * HEADS + h))

    in_specs = ([col(0, g) for g in range(3)] + [col(head_cols, g) for g in range(3)]
                + [col(2 * head_cols, g) for g in range(3)] + [pl.BlockSpec(blk, lambda n, h: (0, 0))] * 3)
    res = pl.pallas_call(
        _attn_kernel,
        out_shape=[jax.ShapeDtypeStruct((N_PROMPT, ATT_OUT), BF16)]
        + [jax.ShapeDtypeStruct((N_PROMPT, ATT_OUT), F32)] * 3,
        grid=(BATCH, HEADS),
        in_specs=in_specs,
        out_specs=[pl.BlockSpec(blk, lambda n, h: (n, h))] * 4,
        scratch_shapes=[pltpu.VMEM(blk, F32), pltpu.VMEM((SEQ, 2 * ATT_BLOCK), F32),
                        pltpu.VMEM((SEQ, 2 * ATT_BLOCK), BF16), pltpu.VMEM(blk, F32),
                        pltpu.VMEM((3,) + blk, F32), pltpu.VMEM((3,) + blk, F32)],
        compiler_params=_params("arbitrary", "arbitrary"),
        name="prompt_attention",
    )(*([qkv] * 9), c, sa, sb)
    return res[0], res[1:]


KV_ROWS = 256


def _kv_cache_kernel(k0, k1, k2, v0, v1, v2, o0, o1, o2):
    tb = pl.program_id(1)

    def put(o_ref, k, v):
        for h in range(HEADS):
            o_ref[:, 0, h, :] = k[:, h * HEAD_DIM:(h + 1) * HEAD_DIM]
            o_ref[:, 1, h, :] = v[:, h * HEAD_DIM:(h + 1) * HEAD_DIM]

    for (k_ref, v_ref, o_ref), window in zip(((k0, v0, o0), (k1, v1, o1), (k2, v2, o2)), WINDOWS):
        keep = min(window, SEQ)
        if keep >= KV_ROWS:
            first = (SEQ - keep) // KV_ROWS
            pl.when(tb >= first)(lambda k_ref=k_ref, v_ref=v_ref, o_ref=o_ref: put(o_ref, k_ref[...], v_ref[...]))
        else:
            pl.when(tb == SEQ // KV_ROWS - 1)(
                lambda k_ref=k_ref, v_ref=v_ref, o_ref=o_ref, keep=keep:
                put(o_ref, k_ref[KV_ROWS - keep:, :], v_ref[KV_ROWS - keep:, :]))


def _prompt_kv_caches(k_rot, qkv):
    steps = SEQ // KV_ROWS
    v_col = 2 * ATT_WIDTH // ATT_OUT
    out_shapes, out_specs = [], []
    for window in WINDOWS:
        keep = min(window, SEQ)
        rows = min(keep, KV_ROWS)
        first = (SEQ - keep) // KV_ROWS
        out_shapes.append(jax.ShapeDtypeStruct((1, BATCH, keep, 2, HEADS, HEAD_DIM), F32))
        out_specs.append(pl.BlockSpec((None, None, rows, 2, HEADS, HEAD_DIM),
                                      lambda n, tb, first=first: (0, n, jnp.maximum(tb - first, 0), 0, 0, 0)))
    return pl.pallas_call(
        _kv_cache_kernel,
        out_shape=out_shapes,
        grid=(BATCH, steps),
        in_specs=[pl.BlockSpec((KV_ROWS, ATT_OUT), lambda n, tb: (n * steps + tb, 0))] * 3
        + [pl.BlockSpec((KV_ROWS, ATT_OUT), lambda n, tb, g=g: (n * steps + tb, v_col + g)) for g in range(3)],
        out_specs=out_specs,
        compiler_params=_params("arbitrary", "arbitrary"),
        name="prompt_kv_caches",
    )(*k_rot, qkv, qkv, qkv)


def _lru_gates(uc, wa_ref, ba, wx_ref, bx, lam, dot=_bdot):
    n_blk = uc.shape[1] // LRU_BLOCK
    r = jnp.concatenate([dot(uc[:, j * LRU_BLOCK:(j + 1) * LRU_BLOCK], wa_ref[j]) for j in range(n_blk)], axis=1)
    i = jnp.concatenate([dot(uc[:, j * LRU_BLOCK:(j + 1) * LRU_BLOCK], wx_ref[j]) for j in range(n_blk)], axis=1)
    r = _sigmoid(r + ba)
    i = _sigmoid(i + bx)
    neg = -lam
    softplus = jnp.maximum(neg, 0.0) + jnp.log1p(jnp.exp(-jnp.abs(neg)))
    log_a = (-LRU_C * r) * softplus
    a = jnp.exp(log_a)
    b = jnp.sqrt(-jnp.tanh(log_a) * (a * a + 1.0)) * (i * uc)
    return a, b


def _lru_kernel(u_ref, g_ref, cw_ref, cb_ref, wa_ref, ba_ref, wx_ref, bx_ref, lam_ref,
                y_ref, hl_ref, hist, carry):
    @pl.when(pl.program_id(2) == 0)
    def _():
        hist[...] = jnp.zeros_like(hist)
        carry[...] = jnp.zeros_like(carry)

    u = u_ref[...]
    cw = cw_ref[...]
    ext = jnp.concatenate([hist[...], u], axis=0)
    off = SUBLANES - (CONV_W - 1)
    conv = ext[off:off + LRU_ROWS] * cw[0:1]
    for t in range(1, CONV_W):
        conv = conv + ext[off + t:off + t + LRU_ROWS] * cw[t:t + 1]
    uc = cb_ref[...] + conv
    hist[...] = u[LRU_ROWS - SUBLANES:]

    a, b = _lru_gates(uc, wa_ref, ba_ref[...], wx_ref, bx_ref[...], lam_ref[...])

    step = lax.broadcasted_iota(I32, a.shape, 0) & (SUBLANES - 1)
    for s in (1, 2, 4):
        a_s = pltpu.roll(a, s, 0)
        b_s = pltpu.roll(b, s, 0)
        valid = step >= s
        b = jnp.where(valid, a * b_s + b, b)
        a = jnp.where(valid, a * a_s, a)
    h_prev = carry[0:1, :]
    hs = []
    for j in range(LRU_ROWS // SUBLANES):
        h_j = a[j * SUBLANES:(j + 1) * SUBLANES] * h_prev + b[j * SUBLANES:(j + 1) * SUBLANES]
        h_prev = h_j[SUBLANES - 1:SUBLANES]
        hs.append(h_j)
    h = jnp.concatenate(hs, axis=0)
    carry[...] = jnp.broadcast_to(h_prev, carry.shape)
    y_ref[...] = (h * _gelu_tanh(g_ref[...])).astype(y_ref.dtype)
    hl_ref[...] = h_prev


def _prompt_lru(ug, conv_w, conv_b, wa, ba, wx, bx, lam):
    t_blocks = SEQ // LRU_ROWS
    c_blocks = LRU_WIDTH // LRU_COLS
    row = pl.BlockSpec((1, LRU_COLS), lambda n, cb, tb: (0, cb))
    gate_w = pl.BlockSpec((LRU_COLS // LRU_BLOCK, LRU_BLOCK, LRU_BLOCK), lambda n, cb, tb: (cb, 0, 0))
    return pl.pallas_call(
        _lru_kernel,
        out_shape=[jax.ShapeDtypeStruct((N_PROMPT, LRU_WIDTH), BF16),
                   jax.ShapeDtypeStruct((BATCH, 1, LRU_WIDTH), F32)],
        grid=(BATCH, c_blocks, t_blocks),
        in_specs=[pl.BlockSpec((LRU_ROWS, LRU_COLS), lambda n, cb, tb: (n * t_blocks + tb, cb)),
                  pl.BlockSpec((LRU_ROWS, LRU_COLS), lambda n, cb, tb: (n * t_blocks + tb, c_blocks + cb)),
                  pl.BlockSpec((CONV_W, LRU_COLS), lambda n, cb, tb: (0, cb)),
                  row, gate_w, row, gate_w, row, row],
        out_specs=[pl.BlockSpec((LRU_ROWS, LRU_COLS), lambda n, cb, tb: (n * t_blocks + tb, cb)),
                   pl.BlockSpec((None, 1, LRU_COLS), lambda n, cb, tb: (n, 0, cb))],
        scratch_shapes=[pltpu.VMEM((SUBLANES, LRU_COLS), F32), pltpu.VMEM((SUBLANES, LRU_COLS), F32)],
        compiler_params=_params("arbitrary", "arbitrary", "arbitrary"),
        name="prompt_lru",
    )(ug, ug, conv_w, conv_b, wa, ba, wx, bx, lam)


def _sample_lru_kernel(h_ref_in, sc_ref, h0_ref, cw_ref, cb_ref, wa_ref, ba_ref, wx_ref, bx_ref, lam_ref,
                       y_ref, h_ref):
    u = h_ref_in[:, 3 * ATT_WIDTH:3 * ATT_WIDTH + LRU_WIDTH]
    g = h_ref_in[:, 3 * ATT_WIDTH + LRU_WIDTH:3 * ATT_WIDTH + 2 * LRU_WIDTH]
    cw = cw_ref[...]
    conv = sc_ref[:, 0:LRU_WIDTH] * cw[0:1]
    for t in range(1, CONV_W - 1):
        conv = conv + sc_ref[:, t * LRU_WIDTH:(t + 1) * LRU_WIDTH] * cw[t:t + 1]
    conv = conv + u * cw[CONV_W - 1:CONV_W]
    uc = cb_ref[...] + conv
    a, b = _lru_gates(uc, wa_ref, ba_ref[...], wx_ref, bx_ref[...], lam_ref[...], dot=_hdot)
    h = b + a * h0_ref[...]
    h_ref[...] = h
    y_ref[...] = h * _gelu_tanh(g)


def _sample_lru(h_in, state_conv, h0, conv_w, conv_b, wa, ba, wx, bx, lam):
    args = (h_in, state_conv, h0, conv_w, conv_b, wa, ba, wx, bx, lam)
    full = lambda shape: pl.BlockSpec(shape, lambda i: (0,) * len(shape))
    return pl.pallas_call(
        _sample_lru_kernel,
        out_shape=[jax.ShapeDtypeStruct((DEC_BATCH, LRU_WIDTH), F32)] * 2,
        grid=(1,),
        in_specs=[full(a.shape) for a in args],
        out_specs=[full((DEC_BATCH, LRU_WIDTH))] * 2,
        compiler_params=_params("arbitrary"),
        name="sample_lru",
    )(*args)


def _mem_attn_kernel(q_ref, k_ref, v_ref, y_ref):
    dn = (((1,), (1,)), ((), ()))
    s = lax.dot_general(q_ref[...].astype(BF16), k_ref[...].astype(BF16), dn,
                        preferred_element_type=F32) * (MEM_HEAD_DIM ** -0.5)
    m = jnp.max(s, axis=-1, keepdims=True)
    p = jnp.exp(s - m)
    p = p / jnp.sum(p, axis=-1, keepdims=True)
    y_ref[...] = _bdot(p, v_ref[...]).astype(y_ref.dtype)


def _prompt_mem_attention(ug, mem_kv):
    rows = 1024
    r_blocks = SEQ // rows
    q_col = 2 * LRU_WIDTH // MEM_HEAD_DIM
    return pl.pallas_call(
        _mem_attn_kernel,
        out_shape=jax.ShapeDtypeStruct((N_PROMPT, MEM_WIDTH), BF16),
        grid=(BATCH, MEM_HEADS, r_blocks),
        in_specs=[pl.BlockSpec((rows, MEM_HEAD_DIM), lambda n, h, r: (n * r_blocks + r, q_col + h)),
                  pl.BlockSpec((MEM_TOKENS, MEM_HEAD_DIM), lambda n, h, r: (n, h)),
                  pl.BlockSpec((MEM_TOKENS, MEM_HEAD_DIM), lambda n, h, r: (n, MEM_HEADS + h))],
        out_specs=pl.BlockSpec((rows, MEM_HEAD_DIM), lambda n, h, r: (n * r_blocks + r, h)),
        compiler_params=_params("arbitrary", "arbitrary", "arbitrary"),
        name="prompt_mem_attention",
    )(ug, mem_kv, mem_kv)


def _heads_attend(q, keys, values, scale, extra=None):
    s = jnp.sum(keys * q[None], axis=-1, keepdims=True) * scale
    m = jnp.max(s, axis=0)
    if extra is not None:
        s_x = jnp.sum(extra[0] * q, axis=-1, keepdims=True) * scale
        m = jnp.maximum(m, s_x)
        p_x = jnp.exp(s_x - m)
    p = jnp.exp(s - m[None])
    den = jnp.sum(p, axis=0)
    if extra is not None:
        den = den + p_x
    o = jnp.sum((p / den[None]) * values, axis=0)
    if extra is not None:
        o = o + (p_x / den) * extra[1]
    return o, m + jnp.log(den)


def _sample_attn_kernel(h_ref, c_ref, sa_ref, sb_ref, w0_ref, w1_ref, w2_ref, mem_ref, ya_ref, ym_ref, kr_ref):
    c, sa, sb = c_ref[...], sa_ref[...], sb_ref[...]
    outs, lses = [], []
    for g, w_ref in enumerate((w0_ref, w1_ref, w2_ref)):
        q, k_new, v_new = [], [], []
        for h in range(HEADS):
            col = (g * HEADS + h) * HEAD_DIM
            q.append(_rope(h_ref[:, col:col + HEAD_DIM], c, sa, sb))
            k_new.append(_rope(h_ref[:, ATT_WIDTH + col:ATT_WIDTH + col + HEAD_DIM], c, sa, sb))
            v_new.append(h_ref[:, 2 * ATT_WIDTH + col:2 * ATT_WIDTH + col + HEAD_DIM])
            kr_ref[:, col:col + HEAD_DIM] = k_new[h]
        stack = lambda rows: jnp.concatenate(rows, axis=0)
        o, lse = _heads_attend(stack(q), w_ref[:, 0], w_ref[:, 1], HEAD_DIM ** -0.5,
                               extra=(stack(k_new), stack(v_new)))
        outs.append(o)
        lses.append(lse)
    y = _group_merge(lses, outs, operand=_keep_f32)
    for h in range(HEADS):
        ya_ref[:, h * HEAD_DIM:(h + 1) * HEAD_DIM] = y[h:h + 1]
    q_col = 3 * ATT_WIDTH + 2 * LRU_WIDTH
    qm = jnp.concatenate([h_ref[:, q_col + h * MEM_HEAD_DIM:q_col + (h + 1) * MEM_HEAD_DIM]
                          for h in range(MEM_HEADS)], axis=0)
    om, _ = _heads_attend(qm, mem_ref[:, 0], mem_ref[:, 1], MEM_HEAD_DIM ** -0.5)
    for h in range(MEM_HEADS):
        ym_ref[:, h * MEM_HEAD_DIM:(h + 1) * MEM_HEAD_DIM] = om[h:h + 1]


def _sample_attention(h_in, c, sa, sb, caches, cache_mem):
    full = lambda shape: pl.BlockSpec(shape, lambda i: (0,) * len(shape))
    row = lambda w: pl.BlockSpec((None, 1, w), lambda i: (i, 0, 0))
    window = pl.BlockSpec((None, ATT_BLOCK, None, 2, HEADS, HEAD_DIM), lambda i: (i, 0, 0, 0, 0, 0))
    widths = (ATT_OUT, MEM_WIDTH, ATT_WIDTH)
    y_att, y_mem, k_rot = pl.pallas_call(
        _sample_attn_kernel,
        out_shape=[jax.ShapeDtypeStruct((DEC_BATCH, 1, w), F32) for w in widths],
        grid=(DEC_BATCH,),
        in_specs=[row(h_in.shape[1]), full(c.shape), full(sa.shape), full(sb.shape), window, window, window,
                  pl.BlockSpec((None, MEM_TOKENS, 2, MEM_HEADS, MEM_HEAD_DIM), lambda i: (i, 0, 0, 0, 0))],
        out_specs=[row(w) for w in widths],
        compiler_params=_params("arbitrary"),
        name="sample_attention",
    )(h_in.reshape(DEC_BATCH, 1, -1), c, sa, sb, *caches, cache_mem)
    return [a.reshape(DEC_BATCH, -1) for a in (y_att, y_mem, k_rot)]


def _merge_kernel(ya_ref, yl_ref, ym_ref, ga_ref, gl_ref, gm_ref, pa_ref, pl_ref, pm_ref, o_ref):
    merged = _sigmoid(ga_ref[...]) * _bdot(ya_ref[...], pa_ref[...])
    merged = merged + _sigmoid(gl_ref[...]) * _bdot(yl_ref[...], pl_ref[...])
    merged = merged + _sigmoid(gm_ref[...]) * _bdot(ym_ref[...], pm_ref[...])
    o_ref[...] = merged.astype(o_ref.dtype)


def _merge(y_att, y_lru, y_mem, gates, p_att, p_lru, p_mem):
    tm = 256
    rows = lambda w: pl.BlockSpec((tm, w), lambda i: (i, 0))
    full = lambda a: pl.BlockSpec(a.shape, lambda i: (0, 0))
    return pl.pallas_call(
        _merge_kernel,
        out_shape=jax.ShapeDtypeStruct((N_PROMPT, D_MODEL), BF16),
        grid=(N_PROMPT // tm,),
        in_specs=[rows(ATT_OUT), rows(LRU_WIDTH), rows(MEM_WIDTH)]
        + [pl.BlockSpec((tm, D_MODEL), lambda i, b=b: (i, b)) for b in range(3)]
        + [full(p_att), full(p_lru), full(p_mem)],
        out_specs=rows(D_MODEL),
        compiler_params=_params("arbitrary"),
        name="branch_merge",
    )(y_att, y_lru, y_mem, gates, gates, gates, p_att, p_lru, p_mem)


def _sample_merge_kernel(h_ref, pa_ref, pl_ref, pm_ref, o_ref):
    gate_col = 3 * ATT_WIDTH + UG_WIDTH
    gate = lambda b: _sigmoid(h_ref[:, gate_col + b * D_MODEL:gate_col + (b + 1) * D_MODEL])
    o_ref[...] = (gate(0) * pa_ref[...] + gate(1) * pl_ref[...]) + gate(2) * pm_ref[...]


def _sample_merge(h_in, pa, pl_, pm):
    args = (h_in, pa, pl_, pm)
    full = lambda a: pl.BlockSpec(a.shape, lambda i: (0, 0))
    return pl.pallas_call(
        _sample_merge_kernel,
        out_shape=jax.ShapeDtypeStruct((DEC_BATCH, D_MODEL), F32),
        grid=(1,),
        in_specs=[full(a) for a in args],
        out_specs=pl.BlockSpec((DEC_BATCH, D_MODEL), lambda i: (0, 0)),
        compiler_params=_params("arbitrary"),
        name="sample_merge",
    )(*args)


def _ln_route(mix, x_ref, g_ref, b_ref, wr_ref, br_ref, x1_ref, e_ref, gt_ref, dot):
    x1 = _layer_norm(DN_ALPHA * x_ref[...] + mix, g_ref[...], b_ref[...])
    x1_ref[...] = x1
    logits = dot(x1, wr_ref[...]) + br_ref[...]
    lane = lax.broadcasted_iota(I32, logits.shape, 1).astype(F32)
    out_lane = lax.broadcasted_iota(I32, e_ref.shape, 1)
    top_v = []
    e_out = jnp.zeros(e_ref.shape, I32)
    for k in range(TOP_K):
        v = jnp.max(logits, axis=-1, keepdims=True)
        e = jnp.min(jnp.where(logits == v, lane, float(N_EXPERTS)), axis=-1, keepdims=True)
        logits = jnp.where(lane == e, -jnp.inf, logits)
        top_v.append(v)
        e_out = jnp.where(out_lane == k, e.astype(I32), e_out)
    ps = [jnp.exp(v - top_v[0]) for v in top_v]
    den = ps[0] + ps[1] + ps[2] + ps[3]
    g_out = jnp.zeros(gt_ref.shape, F32)
    for k in range(TOP_K):
        g_out = jnp.where(out_lane == k, ps[k] / den, g_out)
    e_ref[...] = e_out
    gt_ref[...] = g_out


def _outproj_kernel(m_ref, x_ref, w_ref, g_ref, b_ref, wr_ref, br_ref, x1_ref, x1r_ref, e_ref, gt_ref):
    @pl.when(pl.program_id(0) < N_PROMPT // OUTPROJ_ROWS)
    def _():
        mix = jnp.dot(m_ref[...], w_ref[...], preferred_element_type=F32)
        _ln_route(mix, x_ref, g_ref, b_ref, wr_ref, br_ref, x1_ref, e_ref, gt_ref, _bdot)
        _to_token_rows(x1_ref[...], x1r_ref, OUTPROJ_ROWS)

    @pl.when(pl.program_id(0) >= N_PROMPT // OUTPROJ_ROWS)
    def _():
        for ref in (x1_ref, x1r_ref, e_ref, gt_ref):
            ref[...] = jnp.zeros_like(ref)


def _sample_ln_router_kernel(mix_ref, x_ref, g_ref, b_ref, wr_ref, br_ref, x1_ref, e_ref, gt_ref):
    _ln_route(mix_ref[...], x_ref, g_ref, b_ref, wr_ref, br_ref, x1_ref, e_ref, gt_ref, _hdot)


def _sample_ln_router(mix, x, ln_g, ln_b, w_router, b_router):
    args = (mix, x, ln_g, ln_b, w_router, b_router)
    full = lambda shape: pl.BlockSpec(shape, lambda i: (0, 0))
    out_shapes = [jax.ShapeDtypeStruct((DEC_BATCH, D_MODEL), F32), jax.ShapeDtypeStruct((DEC_BATCH, HEAD_DIM), I32),
                  jax.ShapeDtypeStruct((DEC_BATCH, HEAD_DIM), F32)]
    return pl.pallas_call(
        _sample_ln_router_kernel,
        out_shape=out_shapes,
        grid=(1,),
        in_specs=[full(a.shape) for a in args],
        out_specs=[full(s.shape) for s in out_shapes],
        compiler_params=_params("arbitrary"),
        name="sample_ln_router",
    )(*args)


def _outproj_router(merged, x, w_out, ln_g, ln_b, w_router, b_router):
    tm = OUTPROJ_ROWS
    last = N_PROMPT // tm - 1
    in_rows = lambda w: pl.BlockSpec((tm, w), lambda i: (jnp.minimum(i, last), 0))
    rows = lambda w: pl.BlockSpec((tm, w), lambda i: (i, 0))
    full = lambda a: pl.BlockSpec(a.shape, lambda i: (0, 0))
    return pl.pallas_call(
        _outproj_kernel,
        out_shape=[jax.ShapeDtypeStruct((TOKENS, D_MODEL), F32),
                   jax.ShapeDtypeStruct((TOKENS * TOKEN_ROWS, LANES), F32),
                   jax.ShapeDtypeStruct((TOKENS, HEAD_DIM), I32),
                   jax.ShapeDtypeStruct((TOKENS, HEAD_DIM), F32)],
        grid=(TOKENS // tm,),
        in_specs=[in_rows(D_MODEL), in_rows(D_MODEL), full(w_out), full(ln_g), full(ln_b), full(w_router),
                  full(b_router)],
        out_specs=[rows(D_MODEL), pl.BlockSpec((tm * TOKEN_ROWS, LANES), lambda i: (i, 0)), rows(HEAD_DIM),
                   rows(HEAD_DIM)],
        compiler_params=_params("arbitrary"),
        name="outproj_ln_router",
    )(merged, x, w_out, ln_g, ln_b, w_router, b_router)


def _to_token_rows(x, ref, n):
    for j in range(TOKEN_ROWS):
        ref[pl.ds(j, n, stride=TOKEN_ROWS), :] = x[:, j * LANES:(j + 1) * LANES]


def _from_token_rows(ref, first, n):
    return jnp.concatenate([ref[pl.ds(first * TOKEN_ROWS + j, n, stride=TOKEN_ROWS), :] for j in range(TOKEN_ROWS)],
                           axis=1)


def _gather_pipeline(t, last, idx_hbm, src_hbm, bufs, idx_smem, idx_sem, row_sem, per_step, variants):
    depth = GATHER_DEPTH

    def idx_copy(step, slot):
        return pltpu.make_async_copy(idx_hbm.at[step], idx_smem.at[slot], idx_sem.at[slot])

    def row_copy(slot, r):
        tok = idx_smem[slot, 0, r]
        return pltpu.make_async_copy(src_hbm.at[pl.ds(tok * TOKEN_ROWS, TOKEN_ROWS)],
                                     bufs[slot].at[pl.ds(r * TOKEN_ROWS, TOKEN_ROWS)], row_sem.at[slot])

    def wait_rows(slot):
        pltpu.make_async_copy(src_hbm.at[pl.ds(0, per_step * TOKEN_ROWS)], bufs[slot], row_sem.at[slot]).wait()

    @pl.when(t == 0)
    def _():
        for s in range(depth - 1):
            idx_copy(s, s).start()
            idx_copy(s, s).wait()

            def first(r, carry, s=s):
                row_copy(s, r).start()
                return carry

            lax.fori_loop(0, per_step, first, 0)
        idx_copy(depth - 1, depth - 1).start()

    def step(slot, body):
        ahead = (slot + depth - 1) % depth
        wait_rows(slot)
        idx_copy(t + depth - 1, ahead).wait()
        for r in range(per_step):
            row_copy(ahead, r).start()
        idx_copy(t + depth, slot).start()
        body(bufs[slot])

    for slot in range(depth):
        mine = t % depth == slot
        for cond, body in variants:
            pl.when(mine if cond is None else jnp.logical_and(mine, cond))(
                lambda slot=slot, body=body: step(slot, body))

    @pl.when(t == last)
    def _():
        idx_copy(0, last % depth).wait()
        for s in range(1, depth):
            wait_rows((last + s) % depth)


def _gather_scratch(per_step):
    return ([pltpu.VMEM((per_step * TOKEN_ROWS, LANES), F32)] * GATHER_DEPTH
            + [pltpu.SMEM((GATHER_DEPTH, 1, per_step), I32), pltpu.SemaphoreType.DMA((GATHER_DEPTH,)),
               pltpu.SemaphoreType.DMA((GATHER_DEPTH,))])


def _expert_changed(be_ref, i):
    return jnp.logical_or(i == 0, be_ref[i] != be_ref[jnp.maximum(i - 1, 0)])


def _gate_up_kernel(be_ref, nu_ref, tok_hbm, x_hbm, wg_ref, wu_ref, bg_ref, bu_ref, o_ref,
                    wg_b, wu_b, *gather):
    i = pl.program_id(1)
    t = pl.program_id(0) * MOE_BLOCKS + i
    last = D_FF // FF_TILE * MOE_BLOCKS - 1

    @pl.when(_expert_changed(be_ref, i))
    def _():
        wg_b[...] = wg_ref[...].astype(BF16)
        wu_b[...] = wu_ref[...].astype(BF16)

    def compute(buf):
        x = _from_token_rows(buf, 0, MOE_ROWS).astype(BF16)
        g = jnp.dot(x, wg_b[...], preferred_element_type=F32) + bg_ref[...]
        u = jnp.dot(x, wu_b[...], preferred_element_type=F32) + bu_ref[...]
        g = jnp.minimum(g, SWIGLU_LIMIT)
        u = jnp.clip(u, -SWIGLU_LIMIT, SWIGLU_LIMIT)
        o_ref[...] = ((u + 1.0) * (g * _sigmoid(SWIGLU_ALPHA * g))).astype(o_ref.dtype)

    def unused(buf):
        o_ref[...] = jnp.zeros_like(o_ref)

    used = i < nu_ref[0]
    _gather_pipeline(t, last, tok_hbm, x_hbm, gather[:GATHER_DEPTH], *gather[GATHER_DEPTH:], MOE_ROWS,
                     [(used, compute), (jnp.logical_not(used), unused)])


def _expert_gate_up(blk_expert, n_used, slot_tok, x1_rows, w_gate_up, b_gate_up):
    ff_tiles = D_FF // FF_TILE
    blocks = slot_tok.reshape(MOE_BLOCKS, 1, MOE_ROWS)
    tok_steps = jnp.concatenate([blocks] * ff_tiles + [blocks[:GATHER_DEPTH]], axis=0)
    return pl.pallas_call(
        _gate_up_kernel,
        out_shape=jax.ShapeDtypeStruct((MOE_SLOTS, D_FF), BF16),
        grid_spec=pltpu.PrefetchScalarGridSpec(
            num_scalar_prefetch=2,
            grid=(ff_tiles, MOE_BLOCKS),
            in_specs=[pl.BlockSpec(memory_space=pl.ANY), pl.BlockSpec(memory_space=pl.ANY),
                      pl.BlockSpec((None, D_MODEL, FF_TILE), lambda j, i, be, nu: (be[i], 0, j)),
                      pl.BlockSpec((None, D_MODEL, FF_TILE), lambda j, i, be, nu: (be[i], 0, ff_tiles + j)),
                      pl.BlockSpec((None, 1, FF_TILE), lambda j, i, be, nu: (be[i], 0, j)),
                      pl.BlockSpec((None, 1, FF_TILE), lambda j, i, be, nu: (be[i], 0, ff_tiles + j))],
            out_specs=pl.BlockSpec((MOE_ROWS, FF_TILE), lambda j, i, be, nu: (i, j)),
            scratch_shapes=[pltpu.VMEM((D_MODEL, FF_TILE), BF16), pltpu.VMEM((D_MODEL, FF_TILE), BF16)]
            + _gather_scratch(MOE_ROWS)),
        compiler_params=_params("arbitrary", "arbitrary"),
        name="moe_gate_up",
    )(blk_expert, n_used, tok_steps, x1_rows, w_gate_up, w_gate_up, b_gate_up, b_gate_up)


def _down_kernel(be_ref, nu_ref, h_ref, w_ref, b_ref, o_ref, w_b):
    i = pl.program_id(1)

    @pl.when(_expert_changed(be_ref, i))
    def _():
        w_b[...] = w_ref[...].astype(BF16)

    @pl.when(i < nu_ref[0])
    def _():
        y = jnp.dot(h_ref[...], w_b[...], preferred_element_type=F32) + b_ref[...]
        _to_token_rows(y, o_ref, MOE_ROWS)

    @pl.when(i >= nu_ref[0])
    def _():
        o_ref[...] = jnp.zeros_like(o_ref)


def _expert_down(blk_expert, n_used, hdn, w_down, b_down):
    row_blk = lambda j, i, be, nu: jnp.minimum(i, nu[0] - 1)
    return pl.pallas_call(
        _down_kernel,
        out_shape=jax.ShapeDtypeStruct((MOE_SLOTS * TOKEN_ROWS, LANES), F32),
        grid_spec=pltpu.PrefetchScalarGridSpec(
            num_scalar_prefetch=2,
            grid=(1, MOE_BLOCKS),
            in_specs=[pl.BlockSpec((MOE_ROWS, D_FF), lambda j, i, be, nu: (row_blk(j, i, be, nu), 0)),
                      pl.BlockSpec((None, D_FF, D_MODEL), lambda j, i, be, nu: (be[i], 0, 0)),
                      pl.BlockSpec((None, 1, D_MODEL), lambda j, i, be, nu: (be[i], 0, 0))],
            out_specs=pl.BlockSpec((MOE_ROWS * TOKEN_ROWS, LANES), lambda j, i, be, nu: (i, 0)),
            scratch_shapes=[pltpu.VMEM((D_FF, D_MODEL), BF16)]),
        compiler_params=_params("arbitrary", "arbitrary"),
        name="moe_down",
    )(blk_expert, n_used, hdn, w_down, b_down)


def _combine_kernel(dest_hbm, ys_hbm, gate_ref, x1_ref, g_ref, b_ref, op_ref, ot_ref, *gather):
    i = pl.program_id(0)
    prompt_steps = N_PROMPT // COMBINE_TOKENS

    def compute(buf):
        gate = gate_ref[...]
        ffn = None
        for k in range(TOP_K):
            term = _from_token_rows(buf, k * COMBINE_TOKENS, COMBINE_TOKENS) * gate[:, k:k + 1]
            ffn = term if ffn is None else ffn + term
        y = _layer_norm(DN_ALPHA * x1_ref[...] + ffn, g_ref[...], b_ref[...])

        @pl.when(i < prompt_steps)
        def _():
            op_ref[...] = y

        @pl.when(i >= prompt_steps)
        def _():
            ot_ref[...] = y

    _gather_pipeline(i, TOKENS // COMBINE_TOKENS - 1, dest_hbm, ys_hbm, gather[:GATHER_DEPTH],
                     *gather[GATHER_DEPTH:], COMBINE_TOKENS * TOP_K, [(None, compute)])


def _combine(dest, y_slots, gate, x1, ln_g, ln_b):
    steps = TOKENS // COMBINE_TOKENS
    prompt_steps = N_PROMPT // COMBINE_TOKENS
    per_step = COMBINE_TOKENS * TOP_K
    rows = lambda w: pl.BlockSpec((COMBINE_TOKENS, w), lambda i: (i, 0))
    full = lambda a: pl.BlockSpec(a.shape, lambda i: (0, 0))
    order = dest.reshape(steps, COMBINE_TOKENS, TOP_K).transpose(0, 2, 1).reshape(steps, 1, per_step)
    order = jnp.concatenate([order, order[:GATHER_DEPTH]], axis=0)
    return pl.pallas_call(
        _combine_kernel,
        out_shape=[jax.ShapeDtypeStruct((N_PROMPT, D_MODEL), F32),
                   jax.ShapeDtypeStruct((TOKENS - N_PROMPT, D_MODEL), F32)],
        grid=(steps,),
        in_specs=[pl.BlockSpec(memory_space=pl.ANY), pl.BlockSpec(memory_space=pl.ANY),
                  rows(HEAD_DIM), rows(D_MODEL), full(ln_g), full(ln_b)],
        out_specs=[pl.BlockSpec((COMBINE_TOKENS, D_MODEL), lambda i: (jnp.minimum(i, prompt_steps - 1), 0)),
                   pl.BlockSpec((COMBINE_TOKENS, D_MODEL), lambda i: (jnp.maximum(i - prompt_steps, 0), 0))],
        scratch_shapes=_gather_scratch(per_step),
        compiler_params=_params("arbitrary"),
        name="moe_combine_ln",
    )(order, y_slots, gate, x1, ln_g, ln_b)


def _expert_onehot(e_ref, k):
    e = e_ref[...]
    lane = lax.broadcasted_iota(I32, e.shape, 1)
    return e[:, k:k + 1] == lane


def _rank_kernel(e_ref, tri_ref, rank_ref, cnt_ref, base):
    @pl.when(pl.program_id(0) == 0)
    def _():
        base[...] = jnp.zeros_like(base)

    lane = lax.broadcasted_iota(I32, rank_ref.shape, 1)
    seen = base[0:1, :]
    out = jnp.zeros(rank_ref.shape, F32)
    for k in range(TOP_K):
        hit = _expert_onehot(e_ref, k)
        ones = jnp.where(hit, 1.0, 0.0)
        before = jnp.dot(tri_ref[...], ones.astype(BF16), preferred_element_type=F32) + seen
        out = jnp.where(lane == k, jnp.sum(jnp.where(hit, before, 0.0), axis=-1, keepdims=True), out)
        seen = seen + jnp.sum(ones, axis=0, keepdims=True)
    base[...] = jnp.broadcast_to(seen, base.shape)
    rank_ref[...] = out.astype(I32)
    cnt_ref[...] = jnp.broadcast_to(seen, cnt_ref.shape).astype(I32)


def _dest_kernel(e_ref, rank_ref, start_ref, dest_ref):
    lane = lax.broadcasted_iota(I32, dest_ref.shape, 1)
    out = jnp.zeros(dest_ref.shape, F32)
    for k in range(TOP_K):
        start = jnp.sum(jnp.where(_expert_onehot(e_ref, k), start_ref[0:1, :], 0.0), axis=-1, keepdims=True)
        out = jnp.where(lane == k, start, out)
    dest_ref[...] = out.astype(I32) + rank_ref[...]


def _moe_layout(top_e):
    tm = 512
    steps = -(-TOKENS // tm)
    tri = (jnp.arange(tm)[:, None] > jnp.arange(tm)[None, :]).astype(BF16)
    rows = pl.BlockSpec((tm, LANES), lambda i: (i, 0))
    fixed = lambda shape: pl.BlockSpec(shape, lambda i: (0, 0))
    e_pad = jnp.pad(top_e, ((0, steps * tm - TOKENS), (0, 0)), constant_values=N_EXPERTS)
    rank, counts = pl.pallas_call(
        _rank_kernel,
        out_shape=[jax.ShapeDtypeStruct((steps * tm, LANES), I32), jax.ShapeDtypeStruct((SUBLANES, LANES), I32)],
        grid=(steps,),
        in_specs=[rows, fixed((tm, tm))],
        out_specs=[rows, fixed((SUBLANES, LANES))],
        scratch_shapes=[pltpu.VMEM((SUBLANES, LANES), F32)],
        compiler_params=_params("arbitrary"),
        name="moe_rank",
    )(e_pad, tri)
    counts = counts[0, :N_EXPERTS]
    padded = (counts + MOE_ROWS - 1) // MOE_ROWS * MOE_ROWS
    pad_end = jnp.cumsum(padded)
    starts = jnp.zeros((SUBLANES, LANES), F32).at[:, :N_EXPERTS].set((pad_end - padded).astype(F32))
    dest = pl.pallas_call(
        _dest_kernel,
        out_shape=jax.ShapeDtypeStruct((steps * tm, LANES), I32),
        grid=(steps,),
        in_specs=[rows, rows, fixed((SUBLANES, LANES))],
        out_specs=rows,
        compiler_params=_params("arbitrary"),
        name="moe_dest",
    )(e_pad, rank, starts)[:TOKENS, :TOP_K]
    tok = jnp.broadcast_to(jnp.arange(TOKENS, dtype=I32)[:, None], (TOKENS, TOP_K))
    slot_tok = jnp.zeros((MOE_SLOTS,), I32).at[dest.reshape(-1)].set(tok.reshape(-1), unique_indices=True)
    n_used = pad_end[-1] // MOE_ROWS
    blk = jnp.minimum(jnp.arange(MOE_BLOCKS, dtype=I32), n_used - 1) * MOE_ROWS
    blk_expert = jnp.sum((pad_end[None, :] <= blk[:, None]).astype(I32), axis=1)
    return dest, slot_tok, jnp.minimum(blk_expert, N_EXPERTS - 1), n_used.reshape(1).astype(I32)


def kernel(x_prompt, x_sample, mem_prompt, cache_kv_w128, cache_kv_w512, cache_kv_w2048, cache_mem_kv, state_conv, state_lru_h, w_in, conv_w, conv_b, w_rg_a, b_rg_a, w_rg_x, b_rg_x, lru_lambda, w_mem_kv, p_att, p_lru, p_mem, w_out, ln1_g, ln1_b, w_router, b_router, w_gate_up, b_gate_up, w_down, b_down, ln2_g, ln2_b):
    row = lambda a: a.reshape(1, -1)
    layer = lambda a: a.reshape(a.shape[1:])
    x_p = x_prompt.reshape(N_PROMPT, D_MODEL)
    x_s = x_sample.reshape(DEC_BATCH, D_MODEL)
    xb = x_p.astype(BF16)
    w_in_b = layer(w_in).astype(BF16)
    qkv_cols = 3 * ATT_WIDTH
    gate_col = qkv_cols + UG_WIDTH

    qkv = _matmul(xb, w_in_b[:, :qkv_cols], 1024, 1536, "in_proj_qkv")
    ug = _matmul(xb, w_in_b[:, qkv_cols:gate_col], 1024, 1024, "in_proj_lru_mem")
    gates = _matmul(xb, w_in_b[:, gate_col:], 1024, 1536, "in_proj_gates")

    c_p, sa_p, sb_p = _rope_tables(jnp.arange(SEQ, dtype=I32))
    y_att_p, k_rot = _prompt_attention(qkv, c_p, sa_p, sb_p)
    wa, wx = layer(w_rg_a).astype(BF16), layer(w_rg_x).astype(BF16)
    lru_args = (layer(conv_w), row(conv_b), wa, row(b_rg_a), wx, row(b_rg_x), row(lru_lambda))
    y_lru_p, h_last_p = _prompt_lru(ug, *lru_args)
    mem_kv_p = _matmul(mem_prompt.reshape(BATCH * MEM_TOKENS, D_MODEL), layer(w_mem_kv).astype(BF16), 512, 1024,
                       "mem_kv_proj")
    y_mem_p = _prompt_mem_attention(ug, mem_kv_p)

    merged = _merge(y_att_p, y_lru_p, y_mem_p, gates,
                    layer(p_att).astype(BF16), layer(p_lru).astype(BF16), layer(p_mem).astype(BF16))
    x1, x1_rows, top_e, gate = _outproj_router(merged, x_p, layer(w_out).astype(BF16), row(ln1_g), row(ln1_b),
                                               layer(w_router).astype(BF16), row(b_router))

    h_s_in = _matmul(x_s, layer(w_in), DEC_BATCH, 512, "sample_in_proj", full_precision=True)
    c_s, sa_s, sb_s = _rope_tables(jnp.full((1,), PAST_LEN, I32))
    caches = [cache.reshape(DEC_BATCH, ATT_BLOCK, dil, 2, HEADS, HEAD_DIM)
              for cache, dil in zip((cache_kv_w128, cache_kv_w512, cache_kv_w2048), DILATIONS)]
    y_att_s, y_mem_s, k_rot_s = _sample_attention(h_s_in, c_s, sa_s, sb_s, caches, layer(cache_mem_kv))
    y_lru_s, h_s = _sample_lru(h_s_in, state_conv.reshape(DEC_BATCH, (CONV_W - 1) * LRU_WIDTH), layer(state_lru_h),
                               layer(conv_w), row(conv_b), layer(w_rg_a), row(b_rg_a), layer(w_rg_x), row(b_rg_x),
                               row(lru_lambda))
    hi_mm = lambda a, w, name: _matmul(a, layer(w), DEC_BATCH, 512, name, full_precision=True)
    merged_s = _sample_merge(h_s_in, hi_mm(y_att_s, p_att, "sample_p_att"), hi_mm(y_lru_s, p_lru, "sample_p_lru"),
                             hi_mm(y_mem_s, p_mem, "sample_p_mem"))
    mix_s = hi_mm(merged_s, w_out, "sample_out_proj")
    x1_s, top_e_s, gate_s = _sample_ln_router(mix_s, x_s, row(ln1_g), row(ln1_b), layer(w_router), row(b_router))

    pad = TOKENS - N_PROMPT - DEC_BATCH
    tail = lambda s, p: jnp.concatenate([s, p], axis=0)
    pad_e = (jnp.arange(pad, dtype=I32)[:, None] * TOP_K + jnp.arange(HEAD_DIM, dtype=I32)[None, :]) % N_EXPERTS
    x1_tail = tail(x1_s, jnp.zeros((pad, D_MODEL), F32))
    x1 = lax.dynamic_update_slice(x1, x1_tail, (N_PROMPT, 0))
    x1_rows = lax.dynamic_update_slice(x1_rows, x1_tail.reshape(-1, LANES), (N_PROMPT * TOKEN_ROWS, 0))
    top_e = lax.dynamic_update_slice(top_e, tail(top_e_s, pad_e), (N_PROMPT, 0))
    gate = lax.dynamic_update_slice(gate, tail(gate_s, jnp.zeros((pad, HEAD_DIM), F32)), (N_PROMPT, 0))

    dest, slot_tok, blk_expert, n_used = _moe_layout(top_e)
    hdn = _expert_gate_up(blk_expert, n_used, slot_tok, x1_rows, layer(w_gate_up),
                          b_gate_up.reshape(N_EXPERTS, 1, 2 * D_FF))
    y_slots = _expert_down(blk_expert, n_used, hdn, layer(w_down), b_down.reshape(N_EXPERTS, 1, D_MODEL))
    y_p, y_tail = _combine(dest, y_slots, gate, x1, row(ln2_g), row(ln2_b))

    y_prompt = y_p.reshape(BATCH, SEQ, D_MODEL)
    y_sample = y_tail[:DEC_BATCH].reshape(DEC_BATCH, 1, D_MODEL)
    p_kv = _prompt_kv_caches(k_rot, qkv)
    s_kv = []
    for g in range(len(WINDOWS)):
        v_cols = slice(2 * ATT_WIDTH + g * ATT_OUT, 2 * ATT_WIDTH + (g + 1) * ATT_OUT)
        ks = k_rot_s[:, g * ATT_OUT:(g + 1) * ATT_OUT].reshape(DEC_BATCH, 1, HEADS, HEAD_DIM)
        vs = h_s_in[:, v_cols].reshape(DEC_BATCH, 1, HEADS, HEAD_DIM)
        s_kv.append(jnp.stack([ks, vs], axis=2)[None])
    p_mem_kv = mem_kv_p.reshape(1, BATCH, MEM_TOKENS, 2, MEM_HEADS, MEM_HEAD_DIM)
    p_conv = ug.reshape(BATCH, SEQ, UG_WIDTH)[:, SEQ - (CONV_W - 1):, :LRU_WIDTH][None]
    p_lru_h = h_last_p.reshape(1, BATCH, LRU_WIDTH)
    u_s = h_s_in[:, qkv_cols:qkv_cols + LRU_WIDTH]
    s_conv = jnp.concatenate([layer(state_conv)[:, 1:], u_s[:, None, :]], axis=1)[None]
    s_lru_h = h_s[None]
    return (y_prompt, y_sample, p_kv[0], p_kv[1], p_kv[2], p_mem_kv, p_conv, p_lru_h,
            s_kv[0], s_kv[1], s_kv[2], s_conv, s_lru_h)
```

```python
import jax
import jax.numpy as jnp
from jax import lax
from jax.experimental import pallas as pl
from jax.experimental.pallas import tpu as pltpu

F32 = jnp.float32
BF16 = jnp.bfloat16
I32 = jnp.int32

D_MODEL = 2048
BATCH = 8
SEQ = 2048
DEC_BATCH = 32
PAST_LEN = 8192
HEAD_DIM = 128
HEADS = 4
DILATIONS = (1, 4, 16)
WINDOWS = (128, 512, 2048)
ATT_BLOCK = 128
ATT_WIDTH = 1536
ATT_OUT = 512
ROT_DIM = 32
ROPE_THETA = 500000.0
LRU_WIDTH = 1536
LRU_BLOCK = 128
LRU_C = 8.0
CONV_W = 4
MEM_TOKENS = 256
MEM_HEADS = 4
MEM_HEAD_DIM = 256
MEM_WIDTH = 1024
N_EXPERTS = 32
TOP_K = 4
D_FF = 2048
SWIGLU_LIMIT = 7.0
SWIGLU_ALPHA = 1.702
LN_EPS = 1e-5
DN_ALPHA = 2.0 ** 0.25

N_PROMPT = BATCH * SEQ
OUTPROJ_ROWS = 256
TOKENS = N_PROMPT + OUTPROJ_ROWS
UG_WIDTH = 2 * LRU_WIDTH + MEM_WIDTH

V7X_VMEM_LIMIT = 56 * 1024 * 1024

LRU_COLS = 512
LRU_ROWS = 256
SUBLANES = 8
MOE_ROWS = 320
MOE_BLOCKS = TOKENS * TOP_K // MOE_ROWS + N_EXPERTS
MOE_SLOTS = MOE_BLOCKS * MOE_ROWS
FF_TILE = 1024
COMBINE_TOKENS = 128
GATHER_DEPTH = 3
LANES = 128
TOKEN_ROWS = D_MODEL // LANES


def _params(*sem):
    return pltpu.CompilerParams(dimension_semantics=sem, vmem_limit_bytes=V7X_VMEM_LIMIT)


def _bdot(a, b):
    return jnp.dot(a.astype(BF16), b.astype(BF16), preferred_element_type=F32)


def _hdot(a, b):
    return jnp.dot(a, b, precision=lax.Precision.HIGHEST, preferred_element_type=F32)


def _round_bf16(x):
    return x.astype(BF16).astype(F32)


def _keep_f32(x):
    return x


def _sigmoid(x):
    return 1.0 / (1.0 + jnp.exp(-x))


def _gelu_tanh(x):
    return 0.5 * x * (1.0 + jnp.tanh(0.7978845608028654 * (x + 0.044715 * (x * x * x))))


def _layer_norm(x, g, b):
    mu = jnp.mean(x, axis=-1, keepdims=True)
    xc = x - mu
    var = jnp.mean(xc * xc, axis=-1, keepdims=True)
    return xc * lax.rsqrt(var + LN_EPS) * g + b


def _mm_kernel(x_ref, w_ref, o_ref):
    o_ref[...] = _bdot(x_ref[...], w_ref[...]).astype(o_ref.dtype)


def _mm_hi_kernel(x_ref, w_ref, o_ref):
    o_ref[...] = _hdot(x_ref[...], w_ref[...])


def _matmul(x, w, tm, tn, name, full_precision=False):
    m, k = x.shape
    n = w.shape[1]
    return pl.pallas_call(
        _mm_hi_kernel if full_precision else _mm_kernel,
        out_shape=jax.ShapeDtypeStruct((m, n), F32),
        grid=(n // tn, m // tm),
        in_specs=[pl.BlockSpec((tm, k), lambda j, i: (i, 0)),
                  pl.BlockSpec((k, tn), lambda j, i: (0, j))],
        out_specs=pl.BlockSpec((tm, tn), lambda j, i: (i, j)),
        compiler_params=_params("arbitrary", "arbitrary"),
        name=name,
    )(x, w)


def _rope_tables(pos):
    half = ROT_DIM // 2
    inv_freq = ROPE_THETA ** (-jnp.arange(half, dtype=F32) / half)
    ang = pos.astype(F32)[:, None] * inv_freq[None, :]
    cos, sin = jnp.cos(ang), jnp.sin(ang)
    t = pos.shape[0]
    rest = HEAD_DIM - ROT_DIM
    c = jnp.concatenate([cos, cos, jnp.ones((t, rest), F32)], axis=1)
    sa = jnp.concatenate([-sin, jnp.zeros((t, half + rest), F32)], axis=1)
    sb = jnp.concatenate([jnp.zeros((t, half), F32), sin, jnp.zeros((t, rest), F32)], axis=1)
    return c, sa, sb


def _rope(x, c, sa, sb):
    half = ROT_DIM // 2
    return x * c + pltpu.roll(x, HEAD_DIM - half, 1) * sa + pltpu.roll(x, half, 1) * sb


def _group_merge(lses, outs, operand=_round_bf16):
    m = jnp.maximum(jnp.maximum(lses[0], lses[1]), lses[2])
    es = [jnp.exp(l - m) for l in lses]
    den = es[0] + es[1] + es[2]
    y = None
    for e, o in zip(es, outs):
        term = operand(e / den) * operand(o)
        y = term if y is None else y + term
    return y


def _attn_kernel(q0, q1, q2, k0, k1, k2, v0, v1, v2, c_ref, sa_ref, sb_ref,
                 y_ref, kr0, kr1, kr2, qs, ss, ps, lsub, os_, ls):
    c, sa, sb = c_ref[...], sa_ref[...], sb_ref[...]
    scale = HEAD_DIM ** -0.5
    dn = (((1,), (1,)), ((), ()))
    chunk = 4 * ATT_BLOCK
    for g, (q_ref, k_ref, v_ref, kr_ref) in enumerate(((q0, k0, v0, kr0), (q1, k1, v1, kr1), (q2, k2, v2, kr2))):
        dil = DILATIONS[g]
        nb = SEQ // (dil * ATT_BLOCK)
        qs[...] = _rope(q_ref[...], c, sa, sb)
        kr_ref[...] = _rope(k_ref[...], c, sa, sb)
        o_g, l_g = os_.at[g], ls.at[g]

        def rows_of(start, dil=dil):
            return pl.ds(start, ATT_BLOCK, stride=dil) if dil > 1 else pl.ds(start, ATT_BLOCK)

        blocks = [(cls * nb + b, cls + dil * ATT_BLOCK * b, b) for cls in range(dil) for b in range(nb)]
        for idx, start, b in blocks:
            sub = pl.ds(idx * ATT_BLOCK, ATT_BLOCK)
            q = qs[rows_of(start), :].astype(BF16)
            kc = kr_ref[rows_of(start), :].astype(BF16)
            ss[sub, ATT_BLOCK:] = lax.dot_general(q, kc, dn, preferred_element_type=F32) * scale
            if b > 0:
                kp = kr_ref[rows_of(start - dil * ATT_BLOCK), :].astype(BF16)
                ss[sub, :ATT_BLOCK] = lax.dot_general(q, kp, dn, preferred_element_type=F32) * scale
            else:
                ss[sub, :ATT_BLOCK] = jnp.zeros((ATT_BLOCK, ATT_BLOCK), F32)
        for r0 in range(0, SEQ, chunk):
            row = lax.broadcasted_iota(I32, (chunk, 2 * ATT_BLOCK), 0) + r0
            col = lax.broadcasted_iota(I32, (chunk, 2 * ATT_BLOCK), 1)
            i = row & (ATT_BLOCK - 1)
            has_prev = (row & ((nb - 1) * ATT_BLOCK)) > 0
            in_prev = col < ATT_BLOCK
            valid = jnp.logical_or(jnp.logical_and(in_prev, jnp.logical_and(col >= i, has_prev)),
                                   jnp.logical_and(jnp.logical_not(in_prev), col - ATT_BLOCK <= i))
            s = jnp.where(valid, ss[pl.ds(r0, chunk), :], -jnp.inf)
            m = jnp.max(s, axis=-1, keepdims=True)
            p = jnp.exp(s - m)
            den = jnp.sum(p, axis=-1, keepdims=True)
            ps[pl.ds(r0, chunk), :] = (p / den).astype(BF16)
            lsub[pl.ds(r0, chunk), :] = jnp.broadcast_to(m + jnp.log(den), (chunk, HEAD_DIM))
        for idx, start, b in blocks:
            sub = pl.ds(idx * ATT_BLOCK, ATT_BLOCK)
            vc = v_ref[rows_of(start), :].astype(BF16)
            o = jnp.dot(ps[sub, ATT_BLOCK:], vc, preferred_element_type=F32)
            if b > 0:
                vp = v_ref[rows_of(start - dil * ATT_BLOCK), :].astype(BF16)
                o = o + jnp.dot(ps[sub, :ATT_BLOCK], vp, preferred_element_type=F32)
            o_g[rows_of(start), :] = o
            l_g[rows_of(start), :] = lsub[sub, :]
    y = _group_merge([ls[0], ls[1], ls[2]], [os_[0], os_[1], os_[2]])
    y_ref[...] = y.astype(y_ref.dtype)


def _prompt_attention(qkv, c, sa, sb):
    blk = (SEQ, HEAD_DIM)
    head_cols = ATT_WIDTH // HEAD_DIM

    def col(base, g):
        return pl.BlockSpec(blk, lambda n, h: (n, base + g * HEADS + h))

    in_specs = ([col(0, g) for g in range(3)] + [col(head_cols, g) for g in range(3)]
                + [col(2 * head_cols, g) for g in range(3)] + [pl.BlockSpec(blk, lambda n, h: (0, 0))] * 3)
    res = pl.pallas_call(
        _attn_kernel,
        out_shape=[jax.ShapeDtypeStruct((N_PROMPT, ATT_OUT), BF16)]
        + [jax.ShapeDtypeStruct((N_PROMPT, ATT_OUT), F32)] * 3,
        grid=(BATCH, HEADS),
        in_specs=in_specs,
        out_specs=[pl.BlockSpec(blk, lambda n, h: (n, h))] * 4,
        scratch_shapes=[pltpu.VMEM(blk, F32), pltpu.VMEM((SEQ, 2 * ATT_BLOCK), F32),
                        pltpu.VMEM((SEQ, 2 * ATT_BLOCK), BF16), pltpu.VMEM(blk, F32),
                        pltpu.VMEM((3,) + blk, F32), pltpu.VMEM((3,) + blk, F32)],
        compiler_params=_params("arbitrary", "arbitrary"),
        name="prompt_attention",
    )(*([qkv] * 9), c, sa, sb)
    return res[0], res[1:]


KV_ROWS = 256


def _kv_cache_kernel(k0, k1, k2, v0, v1, v2, o0, o1, o2):
    tb = pl.program_id(1)

    def put(o_ref, k, v):
        for h in range(HEADS):
            o_ref[:, 0, h, :] = k[:, h * HEAD_DIM:(h + 1) * HEAD_DIM]
            o_ref[:, 1, h, :] = v[:, h * HEAD_DIM:(h + 1) * HEAD_DIM]

    for (k_ref, v_ref, o_ref), window in zip(((k0, v0, o0), (k1, v1, o1), (k2, v2, o2)), WINDOWS):
        keep = min(window, SEQ)
        if keep >= KV_ROWS:
            first = (SEQ - keep) // KV_ROWS
            pl.when(tb >= first)(lambda k_ref=k_ref, v_ref=v_ref, o_ref=o_ref: put(o_ref, k_ref[...], v_ref[...]))
        else:
            pl.when(tb == SEQ // KV_ROWS - 1)(
                lambda k_ref=k_ref, v_ref=v_ref, o_ref=o_ref, keep=keep:
                put(o_ref, k_ref[KV_ROWS - keep:, :], v_ref[KV_ROWS - keep:, :]))


def _prompt_kv_caches(k_rot, qkv):
    steps = SEQ // KV_ROWS
    v_col = 2 * ATT_WIDTH // ATT_OUT
    out_shapes, out_specs = [], []
    for window in WINDOWS:
        keep = min(window, SEQ)
        rows = min(keep, KV_ROWS)
        first = (SEQ - keep) // KV_ROWS
        out_shapes.append(jax.ShapeDtypeStruct((1, BATCH, keep, 2, HEADS, HEAD_DIM), F32))
        out_specs.append(pl.BlockSpec((None, None, rows, 2, HEADS, HEAD_DIM),
                                      lambda n, tb, first=first: (0, n, jnp.maximum(tb - first, 0), 0, 0, 0)))
    return pl.pallas_call(
        _kv_cache_kernel,
        out_shape=out_shapes,
        grid=(BATCH, steps),
        in_specs=[pl.BlockSpec((KV_ROWS, ATT_OUT), lambda n, tb: (n * steps + tb, 0))] * 3
        + [pl.BlockSpec((KV_ROWS, ATT_OUT), lambda n, tb, g=g: (n * steps + tb, v_col + g)) for g in range(3)],
        out_specs=out_specs,
        compiler_params=_params("arbitrary", "arbitrary"),
        name="prompt_kv_caches",
    )(*k_rot, qkv, qkv, qkv)


def _lru_gates(uc, wa_ref, ba, wx_ref, bx, lam, dot=_bdot):
    n_blk = uc.shape[1] // LRU_BLOCK
    r = jnp.concatenate([dot(uc[:, j * LRU_BLOCK:(j + 1) * LRU_BLOCK], wa_ref[j]) for j in range(n_blk)], axis=1)
    i = jnp.concatenate([dot(uc[:, j * LRU_BLOCK:(j + 1) * LRU_BLOCK], wx_ref[j]) for j in range(n_blk)], axis=1)
    r = _sigmoid(r + ba)
    i = _sigmoid(i + bx)
    neg = -lam
    softplus = jnp.maximum(neg, 0.0) + jnp.log1p(jnp.exp(-jnp.abs(neg)))
    log_a = (-LRU_C * r) * softplus
    a = jnp.exp(log_a)
    b = jnp.sqrt(-jnp.tanh(log_a) * (a * a + 1.0)) * (i * uc)
    return a, b


def _lru_kernel(u_ref, g_ref, cw_ref, cb_ref, wa_ref, ba_ref, wx_ref, bx_ref, lam_ref,
                y_ref, hl_ref, hist, carry):
    @pl.when(pl.program_id(2) == 0)
    def _():
        hist[...] = jnp.zeros_like(hist)
        carry[...] = jnp.zeros_like(carry)

    u = u_ref[...]
    cw = cw_ref[...]
    ext = jnp.concatenate([hist[...], u], axis=0)
    off = SUBLANES - (CONV_W - 1)
    conv = ext[off:off + LRU_ROWS] * cw[0:1]
    for t in range(1, CONV_W):
        conv = conv + ext[off + t:off + t + LRU_ROWS] * cw[t:t + 1]
    uc = cb_ref[...] + conv
    hist[...] = u[LRU_ROWS - SUBLANES:]

    a, b = _lru_gates(uc, wa_ref, ba_ref[...], wx_ref, bx_ref[...], lam_ref[...])

    step = lax.broadcasted_iota(I32, a.shape, 0) & (SUBLANES - 1)
    for s in (1, 2, 4):
        a_s = pltpu.roll(a, s, 0)
        b_s = pltpu.roll(b, s, 0)
        valid = step >= s
        b = jnp.where(valid, a * b_s + b, b)
        a = jnp.where(valid, a * a_s, a)
    h_prev = carry[0:1, :]
    hs = []
    for j in range(LRU_ROWS // SUBLANES):
        h_j = a[j * SUBLANES:(j + 1) * SUBLANES] * h_prev + b[j * SUBLANES:(j + 1) * SUBLANES]
        h_prev = h_j[SUBLANES - 1:SUBLANES]
        hs.append(h_j)
    h = jnp.concatenate(hs, axis=0)
    carry[...] = jnp.broadcast_to(h_prev, carry.shape)
    y_ref[...] = (h * _gelu_tanh(g_ref[...])).astype(y_ref.dtype)
    hl_ref[...] = h_prev


def _prompt_lru(ug, conv_w, conv_b, wa, ba, wx, bx, lam):
    t_blocks = SEQ // LRU_ROWS
    c_blocks = LRU_WIDTH // LRU_COLS
    row = pl.BlockSpec((1, LRU_COLS), lambda n, cb, tb: (0, cb))
    gate_w = pl.BlockSpec((LRU_COLS // LRU_BLOCK, LRU_BLOCK, LRU_BLOCK), lambda n, cb, tb: (cb, 0, 0))
    return pl.pallas_call(
        _lru_kernel,
        out_shape=[jax.ShapeDtypeStruct((N_PROMPT, LRU_WIDTH), BF16),
                   jax.ShapeDtypeStruct((BATCH, 1, LRU_WIDTH), F32)],
        grid=(BATCH, c_blocks, t_blocks),
        in_specs=[pl.BlockSpec((LRU_ROWS, LRU_COLS), lambda n, cb, tb: (n * t_blocks + tb, cb)),
                  pl.BlockSpec((LRU_ROWS, LRU_COLS), lambda n, cb, tb: (n * t_blocks + tb, c_blocks + cb)),
                  pl.BlockSpec((CONV_W, LRU_COLS), lambda n, cb, tb: (0, cb)),
                  row, gate_w, row, gate_w, row, row],
        out_specs=[pl.BlockSpec((LRU_ROWS, LRU_COLS), lambda n, cb, tb: (n * t_blocks + tb, cb)),
                   pl.BlockSpec((None, 1, LRU_COLS), lambda n, cb, tb: (n, 0, cb))],
        scratch_shapes=[pltpu.VMEM((SUBLANES, LRU_COLS), F32), pltpu.VMEM((SUBLANES, LRU_COLS), F32)],
        compiler_params=_params("arbitrary", "arbitrary", "arbitrary"),
        name="prompt_lru",
    )(ug, ug, conv_w, conv_b, wa, ba, wx, bx, lam)


def _sample_lru_kernel(h_ref_in, sc_ref, h0_ref, cw_ref, cb_ref, wa_ref, ba_ref, wx_ref, bx_ref, lam_ref,
                       y_ref, h_ref):
    u = h_ref_in[:, 3 * ATT_WIDTH:3 * ATT_WIDTH + LRU_WIDTH]
    g = h_ref_in[:, 3 * ATT_WIDTH + LRU_WIDTH:3 * ATT_WIDTH + 2 * LRU_WIDTH]
    cw = cw_ref[...]
    conv = sc_ref[:, 0:LRU_WIDTH] * cw[0:1]
    for t in range(1, CONV_W - 1):
        conv = conv + sc_ref[:, t * LRU_WIDTH:(t + 1) * LRU_WIDTH] * cw[t:t + 1]
    conv = conv + u * cw[CONV_W - 1:CONV_W]
    uc = cb_ref[...] + conv
    a, b = _lru_gates(uc, wa_ref, ba_ref[...], wx_ref, bx_ref[...], lam_ref[...], dot=_hdot)
    h = b + a * h0_ref[...]
    h_ref[...] = h
    y_ref[...] = h * _gelu_tanh(g)


def _sample_lru(h_in, state_conv, h0, conv_w, conv_b, wa, ba, wx, bx, lam):
    args = (h_in, state_conv, h0, conv_w, conv_b, wa, ba, wx, bx, lam)
    full = lambda shape: pl.BlockSpec(shape, lambda i: (0,) * len(shape))
    return pl.pallas_call(
        _sample_lru_kernel,
        out_shape=[jax.ShapeDtypeStruct((DEC_BATCH, LRU_WIDTH), F32)] * 2,
        grid=(1,),
        in_specs=[full(a.shape) for a in args],
        out_specs=[full((DEC_BATCH, LRU_WIDTH))] * 2,
        compiler_params=_params("arbitrary"),
        name="sample_lru",
    )(*args)


def _mem_attn_kernel(q_ref, k_ref, v_ref, y_ref):
    dn = (((1,), (1,)), ((), ()))
    s = lax.dot_general(q_ref[...].astype(BF16), k_ref[...].astype(BF16), dn,
                        preferred_element_type=F32) * (MEM_HEAD_DIM ** -0.5)
    m = jnp.max(s, axis=-1, keepdims=True)
    p = jnp.exp(s - m)
    p = p / jnp.sum(p, axis=-1, keepdims=True)
    y_ref[...] = _bdot(p, v_ref[...]).astype(y_ref.dtype)


def _prompt_mem_attention(ug, mem_kv):
    rows = 1024
    r_blocks = SEQ // rows
    q_col = 2 * LRU_WIDTH // MEM_HEAD_DIM
    return pl.pallas_call(
        _mem_attn_kernel,
        out_shape=jax.ShapeDtypeStruct((N_PROMPT, MEM_WIDTH), BF16),
        grid=(BATCH, MEM_HEADS, r_blocks),
        in_specs=[pl.BlockSpec((rows, MEM_HEAD_DIM), lambda n, h, r: (n * r_blocks + r, q_col + h)),
                  pl.BlockSpec((MEM_TOKENS, MEM_HEAD_DIM), lambda n, h, r: (n, h)),
                  pl.BlockSpec((MEM_TOKENS, MEM_HEAD_DIM), lambda n, h, r: (n, MEM_HEADS + h))],
        out_specs=pl.BlockSpec((rows, MEM_HEAD_DIM), lambda n, h, r: (n * r_blocks + r, h)),
        compiler_params=_params("arbitrary", "arbitrary", "arbitrary"),
        name="prompt_mem_attention",
    )(ug, mem_kv, mem_kv)


def _heads_attend(q, keys, values, scale, extra=None):
    s = jnp.sum(keys * q[None], axis=-1, keepdims=True) * scale
    m = jnp.max(s, axis=0)
    if extra is not None:
        s_x = jnp.sum(extra[0] * q, axis=-1, keepdims=True) * scale
        m = jnp.maximum(m, s_x)
        p_x = jnp.exp(s_x - m)
    p = jnp.exp(s - m[None])
    den = jnp.sum(p, axis=0)
    if extra is not None:
        den = den + p_x
    o = jnp.sum((p / den[None]) * values, axis=0)
    if extra is not None:
        o = o + (p_x / den) * extra[1]
    return o, m + jnp.log(den)


def _sample_attn_kernel(h_ref, c_ref, sa_ref, sb_ref, w0_ref, w1_ref, w2_ref, mem_ref, ya_ref, ym_ref, kr_ref):
    c, sa, sb = c_ref[...], sa_ref[...], sb_ref[...]
    outs, lses = [], []
    for g, w_ref in enumerate((w0_ref, w1_ref, w2_ref)):
        q, k_new, v_new = [], [], []
        for h in range(HEADS):
            col = (g * HEADS + h) * HEAD_DIM
            q.append(_rope(h_ref[:, col:col + HEAD_DIM], c, sa, sb))
            k_new.append(_rope(h_ref[:, ATT_WIDTH + col:ATT_WIDTH + col + HEAD_DIM], c, sa, sb))
            v_new.append(h_ref[:, 2 * ATT_WIDTH + col:2 * ATT_WIDTH + col + HEAD_DIM])
            kr_ref[:, col:col + HEAD_DIM] = k_new[h]
        stack = lambda rows: jnp.concatenate(rows, axis=0)
        o, lse = _heads_attend(stack(q), w_ref[:, 0], w_ref[:, 1], HEAD_DIM ** -0.5,
                               extra=(stack(k_new), stack(v_new)))
        outs.append(o)
        lses.append(lse)
    y = _group_merge(lses, outs, operand=_keep_f32)
    for h in range(HEADS):
        ya_ref[:, h * HEAD_DIM:(h + 1) * HEAD_DIM] = y[h:h + 1]
    q_col = 3 * ATT_WIDTH + 2 * LRU_WIDTH
    qm = jnp.concatenate([h_ref[:, q_col + h * MEM_HEAD_DIM:q_col + (h + 1) * MEM_HEAD_DIM]
                          for h in range(MEM_HEADS)], axis=0)
    om, _ = _heads_attend(qm, mem_ref[:, 0], mem_ref[:, 1], MEM_HEAD_DIM ** -0.5)
    for h in range(MEM_HEADS):
        ym_ref[:, h * MEM_HEAD_DIM:(h + 1) * MEM_HEAD_DIM] = om[h:h + 1]


def _sample_attention(h_in, c, sa, sb, caches, cache_mem):
    full = lambda shape: pl.BlockSpec(shape, lambda i: (0,) * len(shape))
    row = lambda w: pl.BlockSpec((None, 1, w), lambda i: (i, 0, 0))
    window = pl.BlockSpec((None, ATT_BLOCK, None, 2, HEADS, HEAD_DIM), lambda i: (i, 0, 0, 0, 0, 0))
    widths = (ATT_OUT, MEM_WIDTH, ATT_WIDTH)
    y_att, y_mem, k_rot = pl.pallas_call(
        _sample_attn_kernel,
        out_shape=[jax.ShapeDtypeStruct((DEC_BATCH, 1, w), F32) for w in widths],
        grid=(DEC_BATCH,),
        in_specs=[row(h_in.shape[1]), full(c.shape), full(sa.shape), full(sb.shape), window, window, window,
                  pl.BlockSpec((None, MEM_TOKENS, 2, MEM_HEADS, MEM_HEAD_DIM), lambda i: (i, 0, 0, 0, 0))],
        out_specs=[row(w) for w in widths],
        compiler_params=_params("arbitrary"),
        name="sample_attention",
    )(h_in.reshape(DEC_BATCH, 1, -1), c, sa, sb, *caches, cache_mem)
    return [a.reshape(DEC_BATCH, -1) for a in (y_att, y_mem, k_rot)]


def _merge_kernel(ya_ref, yl_ref, ym_ref, ga_ref, gl_ref, gm_ref, pa_ref, pl_ref, pm_ref, o_ref):
    merged = _sigmoid(ga_ref[...]) * _bdot(ya_ref[...], pa_ref[...])
    merged = merged + _sigmoid(gl_ref[...]) * _bdot(yl_ref[...], pl_ref[...])
    merged = merged + _sigmoid(gm_ref[...]) * _bdot(ym_ref[...], pm_ref[...])
    o_ref[...] = merged.astype(o_ref.dtype)


def _merge(y_att, y_lru, y_mem, gates, p_att, p_lru, p_mem):
    tm = 256
    rows = lambda w: pl.BlockSpec((tm, w), lambda i: (i, 0))
    full = lambda a: pl.BlockSpec(a.shape, lambda i: (0, 0))
    return pl.pallas_call(
        _merge_kernel,
        out_shape=jax.ShapeDtypeStruct((N_PROMPT, D_MODEL), BF16),
        grid=(N_PROMPT // tm,),
        in_specs=[rows(ATT_OUT), rows(LRU_WIDTH), rows(MEM_WIDTH)]
        + [pl.BlockSpec((tm, D_MODEL), lambda i, b=b: (i, b)) for b in range(3)]
        + [full(p_att), full(p_lru), full(p_mem)],
        out_specs=rows(D_MODEL),
        compiler_params=_params("arbitrary"),
        name="branch_merge",
    )(y_att, y_lru, y_mem, gates, gates, gates, p_att, p_lru, p_mem)


def _sample_merge_kernel(h_ref, pa_ref, pl_ref, pm_ref, o_ref):
    gate_col = 3 * ATT_WIDTH + UG_WIDTH
    gate = lambda b: _sigmoid(h_ref[:, gate_col + b * D_MODEL:gate_col + (b + 1) * D_MODEL])
    o_ref[...] = (gate(0) * pa_ref[...] + gate(1) * pl_ref[...]) + gate(2) * pm_ref[...]


def _sample_merge(h_in, pa, pl_, pm):
    args = (h_in, pa, pl_, pm)
    full = lambda a: pl.BlockSpec(a.shape, lambda i: (0, 0))
    return pl.pallas_call(
        _sample_merge_kernel,
        out_shape=jax.ShapeDtypeStruct((DEC_BATCH, D_MODEL), F32),
        grid=(1,),
        in_specs=[full(a) for a in args],
        out_specs=pl.BlockSpec((DEC_BATCH, D_MODEL), lambda i: (0, 0)),
        compiler_params=_params("arbitrary"),
        name="sample_merge",
    )(*args)


def _ln_route(mix, x_ref, g_ref, b_ref, wr_ref, br_ref, x1_ref, e_ref, gt_ref, dot):
    x1 = _layer_norm(DN_ALPHA * x_ref[...] + mix, g_ref[...], b_ref[...])
    x1_ref[...] = x1
    logits = dot(x1, wr_ref[...]) + br_ref[...]
    lane = lax.broadcasted_iota(I32, logits.shape, 1).astype(F32)
    out_lane = lax.broadcasted_iota(I32, e_ref.shape, 1)
    top_v = []
    e_out = jnp.zeros(e_ref.shape, I32)
    for k in range(TOP_K):
        v = jnp.max(logits, axis=-1, keepdims=True)
        e = jnp.min(jnp.where(logits == v, lane, float(N_EXPERTS)), axis=-1, keepdims=True)
        logits = jnp.where(lane == e, -jnp.inf, logits)
        top_v.append(v)
        e_out = jnp.where(out_lane == k, e.astype(I32), e_out)
    ps = [jnp.exp(v - top_v[0]) for v in top_v]
    den = ps[0] + ps[1] + ps[2] + ps[3]
    g_out = jnp.zeros(gt_ref.shape, F32)
    for k in range(TOP_K):
        g_out = jnp.where(out_lane == k, ps[k] / den, g_out)
    e_ref[...] = e_out
    gt_ref[...] = g_out


def _outproj_kernel(m_ref, x_ref, w_ref, g_ref, b_ref, wr_ref, br_ref, x1_ref, x1r_ref, e_ref, gt_ref):
    @pl.when(pl.program_id(0) < N_PROMPT // OUTPROJ_ROWS)
    def _():
        mix = jnp.dot(m_ref[...], w_ref[...], preferred_element_type=F32)
        _ln_route(mix, x_ref, g_ref, b_ref, wr_ref, br_ref, x1_ref, e_ref, gt_ref, _bdot)
        _to_token_rows(x1_ref[...], x1r_ref, OUTPROJ_ROWS)

    @pl.when(pl.program_id(0) >= N_PROMPT // OUTPROJ_ROWS)
    def _():
        for ref in (x1_ref, x1r_ref, e_ref, gt_ref):
            ref[...] = jnp.zeros_like(ref)


def _sample_ln_router_kernel(mix_ref, x_ref, g_ref, b_ref, wr_ref, br_ref, x1_ref, e_ref, gt_ref):
    _ln_route(mix_ref[...], x_ref, g_ref, b_ref, wr_ref, br_ref, x1_ref, e_ref, gt_ref, _hdot)


def _sample_ln_router(mix, x, ln_g, ln_b, w_router, b_router):
    args = (mix, x, ln_g, ln_b, w_router, b_router)
    full = lambda shape: pl.BlockSpec(shape, lambda i: (0, 0))
    out_shapes = [jax.ShapeDtypeStruct((DEC_BATCH, D_MODEL), F32), jax.ShapeDtypeStruct((DEC_BATCH, HEAD_DIM), I32),
                  jax.ShapeDtypeStruct((DEC_BATCH, HEAD_DIM), F32)]
    return pl.pallas_call(
        _sample_ln_router_kernel,
        out_shape=out_shapes,
        grid=(1,),
        in_specs=[full(a.shape) for a in args],
        out_specs=[full(s.shape) for s in out_shapes],
        compiler_params=_params("arbitrary"),
        name="sample_ln_router",
    )(*args)


def _outproj_router(merged, x, w_out, ln_g, ln_b, w_router, b_router):
    tm = OUTPROJ_ROWS
    last = N_PROMPT // tm - 1
    in_rows = lambda w: pl.BlockSpec((tm, w), lambda i: (jnp.minimum(i, last), 0))
    rows = lambda w: pl.BlockSpec((tm, w), lambda i: (i, 0))
    full = lambda a: pl.BlockSpec(a.shape, lambda i: (0, 0))
    return pl.pallas_call(
        _outproj_kernel,
        out_shape=[jax.ShapeDtypeStruct((TOKENS, D_MODEL), F32),
                   jax.ShapeDtypeStruct((TOKENS * TOKEN_ROWS, LANES), F32),
                   jax.ShapeDtypeStruct((TOKENS, HEAD_DIM), I32),
                   jax.ShapeDtypeStruct((TOKENS, HEAD_DIM), F32)],
        grid=(TOKENS // tm,),
        in_specs=[in_rows(D_MODEL), in_rows(D_MODEL), full(w_out), full(ln_g), full(ln_b), full(w_router),
                  full(b_router)],
        out_specs=[rows(D_MODEL), pl.BlockSpec((tm * TOKEN_ROWS, LANES), lambda i: (i, 0)), rows(HEAD_DIM),
                   rows(HEAD_DIM)],
        compiler_params=_params("arbitrary"),
        name="outproj_ln_router",
    )(merged, x, w_out, ln_g, ln_b, w_router, b_router)


def _to_token_rows(x, ref, n):
    for j in range(TOKEN_ROWS):
        ref[pl.ds(j, n, stride=TOKEN_ROWS), :] = x[:, j * LANES:(j + 1) * LANES]


def _from_token_rows(ref, first, n):
    return jnp.concatenate([ref[pl.ds(first * TOKEN_ROWS + j, n, stride=TOKEN_ROWS), :] for j in range(TOKEN_ROWS)],
                           axis=1)


def _gather_pipeline(t, last, idx_hbm, src_hbm, bufs, idx_smem, idx_sem, row_sem, per_step, variants):
    depth = GATHER_DEPTH

    def idx_copy(step, slot):
        return pltpu.make_async_copy(idx_hbm.at[step], idx_smem.at[slot], idx_sem.at[slot])

    def row_copy(slot, r):
        tok = idx_smem[slot, 0, r]
        return pltpu.make_async_copy(src_hbm.at[pl.ds(tok * TOKEN_ROWS, TOKEN_ROWS)],
                                     bufs[slot].at[pl.ds(r * TOKEN_ROWS, TOKEN_ROWS)], row_sem.at[slot])

    def wait_rows(slot):
        pltpu.make_async_copy(src_hbm.at[pl.ds(0, per_step * TOKEN_ROWS)], bufs[slot], row_sem.at[slot]).wait()

    @pl.when(t == 0)
    def _():
        for s in range(depth - 1):
            idx_copy(s, s).start()
            idx_copy(s, s).wait()

            def first(r, carry, s=s):
                row_copy(s, r).start()
                return carry

            lax.fori_loop(0, per_step, first, 0)
        idx_copy(depth - 1, depth - 1).start()

    def step(slot, body):
        ahead = (slot + depth - 1) % depth
        wait_rows(slot)
        idx_copy(t + depth - 1, ahead).wait()
        for r in range(per_step):
            row_copy(ahead, r).start()
        idx_copy(t + depth, slot).start()
        body(bufs[slot])

    for slot in range(depth):
        mine = t % depth == slot
        for cond, body in variants:
            pl.when(mine if cond is None else jnp.logical_and(mine, cond))(
                lambda slot=slot, body=body: step(slot, body))

    @pl.when(t == last)
    def _():
        idx_copy(0, last % depth).wait()
        for s in range(1, depth):
            wait_rows((last + s) % depth)


def _gather_scratch(per_step):
    return ([pltpu.VMEM((per_step * TOKEN_ROWS, LANES), F32)] * GATHER_DEPTH
            + [pltpu.SMEM((GATHER_DEPTH, 1, per_step), I32), pltpu.SemaphoreType.DMA((GATHER_DEPTH,)),
               pltpu.SemaphoreType.DMA((GATHER_DEPTH,))])


def _expert_changed(be_ref, i):
    return jnp.logical_or(i == 0, be_ref[i] != be_ref[jnp.maximum(i - 1, 0)])


def _gate_up_kernel(be_ref, nu_ref, tok_hbm, x_hbm, wg_ref, wu_ref, bg_ref, bu_ref, o_ref,
                    wg_b, wu_b, *gather):
    i = pl.program_id(1)
    t = pl.program_id(0) * MOE_BLOCKS + i
    last = D_FF // FF_TILE * MOE_BLOCKS - 1

    @pl.when(_expert_changed(be_ref, i))
    def _():
        wg_b[...] = wg_ref[...].astype(BF16)
        wu_b[...] = wu_ref[...].astype(BF16)

    def compute(buf):
        x = _from_token_rows(buf, 0, MOE_ROWS).astype(BF16)
        g = jnp.dot(x, wg_b[...], preferred_element_type=F32) + bg_ref[...]
        u = jnp.dot(x, wu_b[...], preferred_element_type=F32) + bu_ref[...]
        g = jnp.minimum(g, SWIGLU_LIMIT)
        u = jnp.clip(u, -SWIGLU_LIMIT, SWIGLU_LIMIT)
        o_ref[...] = ((u + 1.0) * (g * _sigmoid(SWIGLU_ALPHA * g))).astype(o_ref.dtype)

    def unused(buf):
        o_ref[...] = jnp.zeros_like(o_ref)

    used = i < nu_ref[0]
    _gather_pipeline(t, last, tok_hbm, x_hbm, gather[:GATHER_DEPTH], *gather[GATHER_DEPTH:], MOE_ROWS,
                     [(used, compute), (jnp.logical_not(used), unused)])


def _expert_gate_up(blk_expert, n_used, slot_tok, x1_rows, w_gate_up, b_gate_up):
    ff_tiles = D_FF // FF_TILE
    blocks = slot_tok.reshape(MOE_BLOCKS, 1, MOE_ROWS)
    tok_steps = jnp.concatenate([blocks] * ff_tiles + [blocks[:GATHER_DEPTH]], axis=0)
    return pl.pallas_call(
        _gate_up_kernel,
        out_shape=jax.ShapeDtypeStruct((MOE_SLOTS, D_FF), BF16),
        grid_spec=pltpu.PrefetchScalarGridSpec(
            num_scalar_prefetch=2,
            grid=(ff_tiles, MOE_BLOCKS),
            in_specs=[pl.BlockSpec(memory_space=pl.ANY), pl.BlockSpec(memory_space=pl.ANY),
                      pl.BlockSpec((None, D_MODEL, FF_TILE), lambda j, i, be, nu: (be[i], 0, j)),
                      pl.BlockSpec((None, D_MODEL, FF_TILE), lambda j, i, be, nu: (be[i], 0, ff_tiles + j)),
                      pl.BlockSpec((None, 1, FF_TILE), lambda j, i, be, nu: (be[i], 0, j)),
                      pl.BlockSpec((None, 1, FF_TILE), lambda j, i, be, nu: (be[i], 0, ff_tiles + j))],
            out_specs=pl.BlockSpec((MOE_ROWS, FF_TILE), lambda j, i, be, nu: (i, j)),
            scratch_shapes=[pltpu.VMEM((D_MODEL, FF_TILE), BF16), pltpu.VMEM((D_MODEL, FF_TILE), BF16)]
            + _gather_scratch(MOE_ROWS)),
        compiler_params=_params("arbitrary", "arbitrary"),
        name="moe_gate_up",
    )(blk_expert, n_used, tok_steps, x1_rows, w_gate_up, w_gate_up, b_gate_up, b_gate_up)


def _down_kernel(be_ref, nu_ref, h_ref, w_ref, b_ref, o_ref, w_b):
    i = pl.program_id(1)

    @pl.when(_expert_changed(be_ref, i))
    def _():
        w_b[...] = w_ref[...].astype(BF16)

    @pl.when(i < nu_ref[0])
    def _():
        y = jnp.dot(h_ref[...], w_b[...], preferred_element_type=F32) + b_ref[...]
        _to_token_rows(y, o_ref, MOE_ROWS)

    @pl.when(i >= nu_ref[0])
    def _():
        o_ref[...] = jnp.zeros_like(o_ref)


def _expert_down(blk_expert, n_used, hdn, w_down, b_down):
    row_blk = lambda j, i, be, nu: jnp.minimum(i, nu[0] - 1)
    return pl.pallas_call(
        _down_kernel,
        out_shape=jax.ShapeDtypeStruct((MOE_SLOTS * TOKEN_ROWS, LANES), F32),
        grid_spec=pltpu.PrefetchScalarGridSpec(
            num_scalar_prefetch=2,
            grid=(1, MOE_BLOCKS),
            in_specs=[pl.BlockSpec((MOE_ROWS, D_FF), lambda j, i, be, nu: (row_blk(j, i, be, nu), 0)),
                      pl.BlockSpec((None, D_FF, D_MODEL), lambda j, i, be, nu: (be[i], 0, 0)),
                      pl.BlockSpec((None, 1, D_MODEL), lambda j, i, be, nu: (be[i], 0, 0))],
            out_specs=pl.BlockSpec((MOE_ROWS * TOKEN_ROWS, LANES), lambda j, i, be, nu: (i, 0)),
            scratch_shapes=[pltpu.VMEM((D_FF, D_MODEL), BF16)]),
        compiler_params=_params("arbitrary", "arbitrary"),
        name="moe_down",
    )(blk_expert, n_used, hdn, w_down, b_down)


def _combine_kernel(dest_hbm, ys_hbm, gate_ref, x1_ref, g_ref, b_ref, op_ref, ot_ref, *gather):
    i = pl.program_id(0)
    prompt_steps = N_PROMPT // COMBINE_TOKENS

    def compute(buf):
        gate = gate_ref[...]
        ffn = None
        for k in range(TOP_K):
            term = _from_token_rows(buf, k * COMBINE_TOKENS, COMBINE_TOKENS) * gate[:, k:k + 1]
            ffn = term if ffn is None else ffn + term
        y = _layer_norm(DN_ALPHA * x1_ref[...] + ffn, g_ref[...], b_ref[...])

        @pl.when(i < prompt_steps)
        def _():
            op_ref[...] = y

        @pl.when(i >= prompt_steps)
        def _():
            ot_ref[...] = y

    _gather_pipeline(i, TOKENS // COMBINE_TOKENS - 1, dest_hbm, ys_hbm, gather[:GATHER_DEPTH],
                     *gather[GATHER_DEPTH:], COMBINE_TOKENS * TOP_K, [(None, compute)])


def _combine(dest, y_slots, gate, x1, ln_g, ln_b):
    steps = TOKENS // COMBINE_TOKENS
    prompt_steps = N_PROMPT // COMBINE_TOKENS
    per_step = COMBINE_TOKENS * TOP_K
    rows = lambda w: pl.BlockSpec((COMBINE_TOKENS, w), lambda i: (i, 0))
    full = lambda a: pl.BlockSpec(a.shape, lambda i: (0, 0))
    order = dest.reshape(steps, COMBINE_TOKENS, TOP_K).transpose(0, 2, 1).reshape(steps, 1, per_step)
    order = jnp.concatenate([order, order[:GATHER_DEPTH]], axis=0)
    return pl.pallas_call(
        _combine_kernel,
        out_shape=[jax.ShapeDtypeStruct((N_PROMPT, D_MODEL), F32),
                   jax.ShapeDtypeStruct((TOKENS - N_PROMPT, D_MODEL), F32)],
        grid=(steps,),
        in_specs=[pl.BlockSpec(memory_space=pl.ANY), pl.BlockSpec(memory_space=pl.ANY),
                  rows(HEAD_DIM), rows(D_MODEL), full(ln_g), full(ln_b)],
        out_specs=[pl.BlockSpec((COMBINE_TOKENS, D_MODEL), lambda i: (jnp.minimum(i, prompt_steps - 1), 0)),
                   pl.BlockSpec((COMBINE_TOKENS, D_MODEL), lambda i: (jnp.maximum(i - prompt_steps, 0), 0))],
        scratch_shapes=_gather_scratch(per_step),
        compiler_params=_params("arbitrary"),
        name="moe_combine_ln",
    )(order, y_slots, gate, x1, ln_g, ln_b)


def _expert_onehot(e_ref, k):
    e = e_ref[...]
    lane = lax.broadcasted_iota(I32, e.shape, 1)
    return e[:, k:k + 1] == lane


def _rank_kernel(e_ref, tri_ref, rank_ref, cnt_ref, base):
    @pl.when(pl.program_id(0) == 0)
    def _():
        base[...] = jnp.zeros_like(base)

    lane = lax.broadcasted_iota(I32, rank_ref.shape, 1)
    seen = base[0:1, :]
    out = jnp.zeros(rank_ref.shape, F32)
    for k in range(TOP_K):
        hit = _expert_onehot(e_ref, k)
        ones = jnp.where(hit, 1.0, 0.0)
        before = jnp.dot(tri_ref[...], ones.astype(BF16), preferred_element_type=F32) + seen
        out = jnp.where(lane == k, jnp.sum(jnp.where(hit, before, 0.0), axis=-1, keepdims=True), out)
        seen = seen + jnp.sum(ones, axis=0, keepdims=True)
    base[...] = jnp.broadcast_to(seen, base.shape)
    rank_ref[...] = out.astype(I32)
    cnt_ref[...] = jnp.broadcast_to(seen, cnt_ref.shape).astype(I32)


def _dest_kernel(e_ref, rank_ref, start_ref, dest_ref):
    lane = lax.broadcasted_iota(I32, dest_ref.shape, 1)
    out = jnp.zeros(dest_ref.shape, F32)
    for k in range(TOP_K):
        start = jnp.sum(jnp.where(_expert_onehot(e_ref, k), start_ref[0:1, :], 0.0), axis=-1, keepdims=True)
        out = jnp.where(lane == k, start, out)
    dest_ref[...] = out.astype(I32) + rank_ref[...]


def _moe_layout(top_e):
    tm = 512
    steps = -(-TOKENS // tm)
    tri = (jnp.arange(tm)[:, None] > jnp.arange(tm)[None, :]).astype(BF16)
    rows = pl.BlockSpec((tm, LANES), lambda i: (i, 0))
    fixed = lambda shape: pl.BlockSpec(shape, lambda i: (0, 0))
    e_pad = jnp.pad(top_e, ((0, steps * tm - TOKENS), (0, 0)), constant_values=N_EXPERTS)
    rank, counts = pl.pallas_call(
        _rank_kernel,
        out_shape=[jax.ShapeDtypeStruct((steps * tm, LANES), I32), jax.ShapeDtypeStruct((SUBLANES, LANES), I32)],
        grid=(steps,),
        in_specs=[rows, fixed((tm, tm))],
        out_specs=[rows, fixed((SUBLANES, LANES))],
        scratch_shapes=[pltpu.VMEM((SUBLANES, LANES), F32)],
        compiler_params=_params("arbitrary"),
        name="moe_rank",
    )(e_pad, tri)
    counts = counts[0, :N_EXPERTS]
    padded = (counts + MOE_ROWS - 1) // MOE_ROWS * MOE_ROWS
    pad_end = jnp.cumsum(padded)
    starts = jnp.zeros((SUBLANES, LANES), F32).at[:, :N_EXPERTS].set((pad_end - padded).astype(F32))
    dest = pl.pallas_call(
        _dest_kernel,
        out_shape=jax.ShapeDtypeStruct((steps * tm, LANES), I32),
        grid=(steps,),
        in_specs=[rows, rows, fixed((SUBLANES, LANES))],
        out_specs=rows,
        compiler_params=_params("arbitrary"),
        name="moe_dest",
    )(e_pad, rank, starts)[:TOKENS, :TOP_K]
    tok = jnp.broadcast_to(jnp.arange(TOKENS, dtype=I32)[:, None], (TOKENS, TOP_K))
    slot_tok = jnp.zeros((MOE_SLOTS,), I32).at[dest.reshape(-1)].set(tok.reshape(-1), unique_indices=True)
    n_used = pad_end[-1] // MOE_ROWS
    blk = jnp.minimum(jnp.arange(MOE_BLOCKS, dtype=I32), n_used - 1) * MOE_ROWS
    blk_expert = jnp.sum((pad_end[None, :] <= blk[:, None]).astype(I32), axis=1)
    return dest, slot_tok, jnp.minimum(blk_expert, N_EXPERTS - 1), n_used.reshape(1).astype(I32)


def kernel(x_prompt, x_sample, mem_prompt, cache_kv_w128, cache_kv_w512, cache_kv_w2048, cache_mem_kv, state_conv, state_lru_h, w_in, conv_w, conv_b, w_rg_a, b_rg_a, w_rg_x, b_rg_x, lru_lambda, w_mem_kv, p_att, p_lru, p_mem, w_out, ln1_g, ln1_b, w_router, b_router, w_gate_up, b_gate_up, w_down, b_down, ln2_g, ln2_b):
    row = lambda a: a.reshape(1, -1)
    layer = lambda a: a.reshape(a.shape[1:])
    x_p = x_prompt.reshape(N_PROMPT, D_MODEL)
    x_s = x_sample.reshape(DEC_BATCH, D_MODEL)
    xb = x_p.astype(BF16)
    w_in_b = layer(w_in).astype(BF16)
    qkv_cols = 3 * ATT_WIDTH
    gate_col = qkv_cols + UG_WIDTH

    qkv = _matmul(xb, w_in_b[:, :qkv_cols], 1024, 1536, "in_proj_qkv")
    ug = _matmul(xb, w_in_b[:, qkv_cols:gate_col], 1024, 1024, "in_proj_lru_mem")
    gates = _matmul(xb, w_in_b[:, gate_col:], 1024, 1536, "in_proj_gates")

    c_p, sa_p, sb_p = _rope_tables(jnp.arange(SEQ, dtype=I32))
    y_att_p, k_rot = _prompt_attention(qkv, c_p, sa_p, sb_p)
    wa, wx = layer(w_rg_a).astype(BF16), layer(w_rg_x).astype(BF16)
    lru_args = (layer(conv_w), row(conv_b), wa, row(b_rg_a), wx, row(b_rg_x), row(lru_lambda))
    y_lru_p, h_last_p = _prompt_lru(ug, *lru_args)
    mem_kv_p = _matmul(mem_prompt.reshape(BATCH * MEM_TOKENS, D_MODEL), layer(w_mem_kv).astype(BF16), 512, 1024,
                       "mem_kv_proj")
    y_mem_p = _prompt_mem_attention(ug, mem_kv_p)

    merged = _merge(y_att_p, y_lru_p, y_mem_p, gates,
                    layer(p_att).astype(BF16), layer(p_lru).astype(BF16), layer(p_mem).astype(BF16))
    x1, x1_rows, top_e, gate = _outproj_router(merged, x_p, layer(w_out).astype(BF16), row(ln1_g), row(ln1_b),
                                               layer(w_router).astype(BF16), row(b_router))

    h_s_in = _matmul(x_s, layer(w_in), DEC_BATCH, 512, "sample_in_proj", full_precision=True)
    c_s, sa_s, sb_s = _rope_tables(jnp.full((1,), PAST_LEN, I32))
    caches = [cache.reshape(DEC_BATCH, ATT_BLOCK, dil, 2, HEADS, HEAD_DIM)
              for cache, dil in zip((cache_kv_w128, cache_kv_w512, cache_kv_w2048), DILATIONS)]
    y_att_s, y_mem_s, k_rot_s = _sample_attention(h_s_in, c_s, sa_s, sb_s, caches, layer(cache_mem_kv))
    y_lru_s, h_s = _sample_lru(h_s_in, state_conv.reshape(DEC_BATCH, (CONV_W - 1) * LRU_WIDTH), layer(state_lru_h),
                               layer(conv_w), row(conv_b), layer(w_rg_a), row(b_rg_a), layer(w_rg_x), row(b_rg_x),
                               row(lru_lambda))
    hi_mm = lambda a, w, name: _matmul(a, layer(w), DEC_BATCH, 512, name, full_precision=True)
    merged_s = _sample_merge(h_s_in, hi_mm(y_att_s, p_att, "sample_p_att"), hi_mm(y_lru_s, p_lru, "sample_p_lru"),
                             hi_mm(y_mem_s, p_mem, "sample_p_mem"))
    mix_s = hi_mm(merged_s, w_out, "sample_out_proj")
    x1_s, top_e_s, gate_s = _sample_ln_router(mix_s, x_s, row(ln1_g), row(ln1_b), layer(w_router), row(b_router))

    pad = TOKENS - N_PROMPT - DEC_BATCH
    tail = lambda s, p: jnp.concatenate([s, p], axis=0)
    pad_e = (jnp.arange(pad, dtype=I32)[:, None] * TOP_K + jnp.arange(HEAD_DIM, dtype=I32)[None, :]) % N_EXPERTS
    x1_tail = tail(x1_s, jnp.zeros((pad, D_MODEL), F32))
    x1 = lax.dynamic_update_slice(x1, x1_tail, (N_PROMPT, 0))
    x1_rows = lax.dynamic_update_slice(x1_rows, x1_tail.reshape(-1, LANES), (N_PROMPT * TOKEN_ROWS, 0))
    top_e = lax.dynamic_update_slice(top_e, tail(top_e_s, pad_e), (N_PROMPT, 0))
    gate = lax.dynamic_update_slice(gate, tail(gate_s, jnp.zeros((pad, HEAD_DIM), F32)), (N_PROMPT, 0))

    dest, slot_tok, blk_expert, n_used = _moe_layout(top_e)
    hdn = _expert_gate_up(blk_expert, n_used, slot_tok, x1_rows, layer(w_gate_up),
                          b_gate_up.reshape(N_EXPERTS, 1, 2 * D_FF))
    y_slots = _expert_down(blk_expert, n_used, hdn, layer(w_down), b_down.reshape(N_EXPERTS, 1, D_MODEL))
    y_p, y_tail = _combine(dest, y_slots, gate, x1, row(ln2_g), row(ln2_b))

    y_prompt = y_p.reshape(BATCH, SEQ, D_MODEL)
    y_sample = y_tail[:DEC_BATCH].reshape(DEC_BATCH, 1, D_MODEL)
    p_kv = _prompt_kv_caches(k_rot, qkv)
    s_kv = []
    for g in range(len(WINDOWS)):
        v_cols = slice(2 * ATT_WIDTH + g * ATT_OUT, 2 * ATT_WIDTH + (g + 1) * ATT_OUT)
        ks = k_rot_s[:, g * ATT_OUT:(g + 1) * ATT_OUT].reshape(DEC_BATCH, 1, HEADS, HEAD_DIM)
        vs = h_s_in[:, v_cols].reshape(DEC_BATCH, 1, HEADS, HEAD_DIM)
        s_kv.append(jnp.stack([ks, vs], axis=2)[None])
    p_mem_kv = mem_kv_p.reshape(1, BATCH, MEM_TOKENS, 2, MEM_HEADS, MEM_HEAD_DIM)
    p_conv = ug.reshape(BATCH, SEQ, UG_WIDTH)[:, SEQ - (CONV_W - 1):, :LRU_WIDTH][None]
    p_lru_h = h_last_p.reshape(1, BATCH, LRU_WIDTH)
    u_s = h_s_in[:, qkv_cols:qkv_cols + LRU_WIDTH]
    s_conv = jnp.concatenate([layer(state_conv)[:, 1:], u_s[:, None, :]], axis=1)[None]
    s_lru_h = h_s[None]
    return (y_prompt, y_sample, p_kv[0], p_kv[1], p_kv[2], p_mem_kv, p_conv, p_lru_h,
            s_kv[0], s_kv[1], s_kv[2], s_conv, s_lru_h)
```

```python
import jax
import jax.numpy as jnp
from jax import lax
from jax.experimental import pallas as pl
from jax.experimental.pallas import tpu as pltpu

F32 = jnp.float32
BF16 = jnp.bfloat16
I32 = jnp.int32

D_MODEL = 2048
BATCH = 8
SEQ = 2048
DEC_BATCH = 32
PAST_LEN = 8192
HEAD_DIM = 128
HEADS = 4
DILATIONS = (1, 4, 16)
WINDOWS = (128, 512, 2048)
ATT_BLOCK = 128
ATT_WIDTH = 1536
ATT_OUT = 512
ROT_DIM = 32
ROPE_THETA = 500000.0
LRU_WIDTH = 1536
LRU_BLOCK = 128
LRU_C = 8.0
CONV_W = 4
MEM_TOKENS = 256
MEM_HEADS = 4
MEM_HEAD_DIM = 256
MEM_WIDTH = 1024
N_EXPERTS = 32
TOP_K = 4
D_FF = 2048
SWIGLU_LIMIT = 7.0
SWIGLU_ALPHA = 1.702
LN_EPS = 1e-5
DN_ALPHA = 2.0 ** 0.25

N_PROMPT = BATCH * SEQ
OUTPROJ_ROWS = 256
TOKENS = N_PROMPT + OUTPROJ_ROWS
UG_WIDTH = 2 * LRU_WIDTH + MEM_WIDTH

V7X_VMEM_LIMIT = 56 * 1024 * 1024

LRU_COLS = 512
LRU_ROWS = 256
SUBLANES = 8
MOE_ROWS = 256
MOE_BLOCKS = TOKENS * TOP_K // MOE_ROWS + N_EXPERTS
MOE_SLOTS = MOE_BLOCKS * MOE_ROWS
FF_TILE = 1024
COMBINE_TOKENS = 128
GATHER_DEPTH = 3
LANES = 128
TOKEN_ROWS = D_MODEL // LANES


def _params(*sem):
    return pltpu.CompilerParams(dimension_semantics=sem, vmem_limit_bytes=V7X_VMEM_LIMIT)


def _bdot(a, b):
    return jnp.dot(a.astype(BF16), b.astype(BF16), preferred_element_type=F32)


def _hdot(a, b):
    return jnp.dot(a, b, precision=lax.Precision.HIGHEST, preferred_element_type=F32)


def _round_bf16(x):
    return x.astype(BF16).astype(F32)


def _keep_f32(x):
    return x


def _sigmoid(x):
    return 1.0 / (1.0 + jnp.exp(-x))


def _gelu_tanh(x):
    return 0.5 * x * (1.0 + jnp.tanh(0.7978845608028654 * (x + 0.044715 * (x * x * x))))


def _layer_norm(x, g, b):
    mu = jnp.mean(x, axis=-1, keepdims=True)
    xc = x - mu
    var = jnp.mean(xc * xc, axis=-1, keepdims=True)
    return xc * lax.rsqrt(var + LN_EPS) * g + b


def _mm_kernel(x_ref, w_ref, o_ref):
    o_ref[...] = _bdot(x_ref[...], w_ref[...]).astype(o_ref.dtype)


def _mm_hi_kernel(x_ref, w_ref, o_ref):
    o_ref[...] = _hdot(x_ref[...], w_ref[...])


def _matmul(x, w, tm, tn, name, full_precision=False):
    m, k = x.shape
    n = w.shape[1]
    return pl.pallas_call(
        _mm_hi_kernel if full_precision else _mm_kernel,
        out_shape=jax.ShapeDtypeStruct((m, n), F32),
        grid=(n // tn, m // tm),
        in_specs=[pl.BlockSpec((tm, k), lambda j, i: (i, 0)),
                  pl.BlockSpec((k, tn), lambda j, i: (0, j))],
        out_specs=pl.BlockSpec((tm, tn), lambda j, i: (i, j)),
        compiler_params=_params("arbitrary", "arbitrary"),
        name=name,
    )(x, w)


def _rope_tables(pos):
    half = ROT_DIM // 2
    inv_freq = ROPE_THETA ** (-jnp.arange(half, dtype=F32) / half)
    ang = pos.astype(F32)[:, None] * inv_freq[None, :]
    cos, sin = jnp.cos(ang), jnp.sin(ang)
    t = pos.shape[0]
    rest = HEAD_DIM - ROT_DIM
    c = jnp.concatenate([cos, cos, jnp.ones((t, rest), F32)], axis=1)
    sa = jnp.concatenate([-sin, jnp.zeros((t, half + rest), F32)], axis=1)
    sb = jnp.concatenate([jnp.zeros((t, half), F32), sin, jnp.zeros((t, rest), F32)], axis=1)
    return c, sa, sb


def _rope(x, c, sa, sb):
    half = ROT_DIM // 2
    return x * c + pltpu.roll(x, HEAD_DIM - half, 1) * sa + pltpu.roll(x, half, 1) * sb


def _group_merge(lses, outs, operand=_round_bf16):
    m = jnp.maximum(jnp.maximum(lses[0], lses[1]), lses[2])
    es = [jnp.exp(l - m) for l in lses]
    den = es[0] + es[1] + es[2]
    y = None
    for e, o in zip(es, outs):
        term = operand(e / den) * operand(o)
        y = term if y is None else y + term
    return y


def _attn_kernel(q0, q1, q2, k0, k1, k2, v0, v1, v2, c_ref, sa_ref, sb_ref,
                 y_ref, kr0, kr1, kr2, qs, ss, ps, lsub, os_, ls):
    c, sa, sb = c_ref[...], sa_ref[...], sb_ref[...]
    scale = HEAD_DIM ** -0.5
    dn = (((1,), (1,)), ((), ()))
    chunk = 4 * ATT_BLOCK
    for g, (q_ref, k_ref, v_ref, kr_ref) in enumerate(((q0, k0, v0, kr0), (q1, k1, v1, kr1), (q2, k2, v2, kr2))):
        dil = DILATIONS[g]
        nb = SEQ // (dil * ATT_BLOCK)
        qs[...] = _rope(q_ref[...], c, sa, sb)
        kr_ref[...] = _rope(k_ref[...], c, sa, sb)
        o_g, l_g = os_.at[g], ls.at[g]

        def rows_of(start, dil=dil):
            return pl.ds(start, ATT_BLOCK, stride=dil) if dil > 1 else pl.ds(start, ATT_BLOCK)

        blocks = [(cls * nb + b, cls + dil * ATT_BLOCK * b, b) for cls in range(dil) for b in range(nb)]
        for idx, start, b in blocks:
            sub = pl.ds(idx * ATT_BLOCK, ATT_BLOCK)
            q = qs[rows_of(start), :].astype(BF16)
            kc = kr_ref[rows_of(start), :].astype(BF16)
            ss[sub, ATT_BLOCK:] = lax.dot_general(q, kc, dn, preferred_element_type=F32) * scale
            if b > 0:
                kp = kr_ref[rows_of(start - dil * ATT_BLOCK), :].astype(BF16)
                ss[sub, :ATT_BLOCK] = lax.dot_general(q, kp, dn, preferred_element_type=F32) * scale
            else:
                ss[sub, :ATT_BLOCK] = jnp.zeros((ATT_BLOCK, ATT_BLOCK), F32)
        for r0 in range(0, SEQ, chunk):
            row = lax.broadcasted_iota(I32, (chunk, 2 * ATT_BLOCK), 0) + r0
            col = lax.broadcasted_iota(I32, (chunk, 2 * ATT_BLOCK), 1)
            i = row & (ATT_BLOCK - 1)
            has_prev = (row & ((nb - 1) * ATT_BLOCK)) > 0
            in_prev = col < ATT_BLOCK
            valid = jnp.logical_or(jnp.logical_and(in_prev, jnp.logical_and(col >= i, has_prev)),
                                   jnp.logical_and(jnp.logical_not(in_prev), col - ATT_BLOCK <= i))
            s = jnp.where(valid, ss[pl.ds(r0, chunk), :], -jnp.inf)
            m = jnp.max(s, axis=-1, keepdims=True)
            p = jnp.exp(s - m)
            den = jnp.sum(p, axis=-1, keepdims=True)
            ps[pl.ds(r0, chunk), :] = (p / den).astype(BF16)
            lsub[pl.ds(r0, chunk), :] = jnp.broadcast_to(m + jnp.log(den), (chunk, HEAD_DIM))
        for idx, start, b in blocks:
            sub = pl.ds(idx * ATT_BLOCK, ATT_BLOCK)
            vc = v_ref[rows_of(start), :].astype(BF16)
            o = jnp.dot(ps[sub, ATT_BLOCK:], vc, preferred_element_type=F32)
            if b > 0:
                vp = v_ref[rows_of(start - dil * ATT_BLOCK), :].astype(BF16)
                o = o + jnp.dot(ps[sub, :ATT_BLOCK], vp, preferred_element_type=F32)
            o_g[rows_of(start), :] = o
            l_g[rows_of(start), :] = lsub[sub, :]
    y = _group_merge([ls[0], ls[1], ls[2]], [os_[0], os_[1], os_[2]])
    y_ref[...] = y.astype(y_ref.dtype)


def _prompt_attention(qkv, c, sa, sb):
    blk = (SEQ, HEAD_DIM)
    head_cols = ATT_WIDTH // HEAD_DIM

    def col(base, g):
        return pl.BlockSpec(blk, lambda n, h: (n, base + g * HEADS + h))

    in_specs = ([col(0, g) for g in range(3)] + [col(head_cols, g) for g in range(3)]
                + [col(2 * head_cols, g) for g in range(3)] + [pl.BlockSpec(blk, lambda n, h: (0, 0))] * 3)
    res = pl.pallas_call(
        _attn_kernel,
        out_shape=[jax.ShapeDtypeStruct((N_PROMPT, ATT_OUT), BF16)]
        + [jax.ShapeDtypeStruct((N_PROMPT, ATT_OUT), F32)] * 3,
        grid=(BATCH, HEADS),
        in_specs=in_specs,
        out_specs=[pl.BlockSpec(blk, lambda n, h: (n, h))] * 4,
        scratch_shapes=[pltpu.VMEM(blk, F32), pltpu.VMEM((SEQ, 2 * ATT_BLOCK), F32),
                        pltpu.VMEM((SEQ, 2 * ATT_BLOCK), BF16), pltpu.VMEM(blk, F32),
                        pltpu.VMEM((3,) + blk, F32), pltpu.VMEM((3,) + blk, F32)],
        compiler_params=_params("arbitrary", "arbitrary"),
        name="prompt_attention",
    )(*([qkv] * 9), c, sa, sb)
    return res[0], res[1:]


KV_ROWS = 256


def _kv_cache_kernel(k0, k1, k2, v0, v1, v2, o0, o1, o2):
    tb = pl.program_id(1)

    def put(o_ref, k, v):
        for h in range(HEADS):
            o_ref[:, 0, h, :] = k[:, h * HEAD_DIM:(h + 1) * HEAD_DIM]
            o_ref[:, 1, h, :] = v[:, h * HEAD_DIM:(h + 1) * HEAD_DIM]

    for (k_ref, v_ref, o_ref), window in zip(((k0, v0, o0), (k1, v1, o1), (k2, v2, o2)), WINDOWS):
        keep = min(window, SEQ)
        if keep >= KV_ROWS:
            first = (SEQ - keep) // KV_ROWS
            pl.when(tb >= first)(lambda k_ref=k_ref, v_ref=v_ref, o_ref=o_ref: put(o_ref, k_ref[...], v_ref[...]))
        else:
            pl.when(tb == SEQ // KV_ROWS - 1)(
                lambda k_ref=k_ref, v_ref=v_ref, o_ref=o_ref, keep=keep:
                put(o_ref, k_ref[KV_ROWS - keep:, :], v_ref[KV_ROWS - keep:, :]))


def _prompt_kv_caches(k_rot, qkv):
    steps = SEQ // KV_ROWS
    v_col = 2 * ATT_WIDTH // ATT_OUT
    out_shapes, out_specs = [], []
    for window in WINDOWS:
        keep = min(window, SEQ)
        rows = min(keep, KV_ROWS)
        first = (SEQ - keep) // KV_ROWS
        out_shapes.append(jax.ShapeDtypeStruct((1, BATCH, keep, 2, HEADS, HEAD_DIM), F32))
        out_specs.append(pl.BlockSpec((None, None, rows, 2, HEADS, HEAD_DIM),
                                      lambda n, tb, first=first: (0, n, jnp.maximum(tb - first, 0), 0, 0, 0)))
    return pl.pallas_call(
        _kv_cache_kernel,
        out_shape=out_shapes,
        grid=(BATCH, steps),
        in_specs=[pl.BlockSpec((KV_ROWS, ATT_OUT), lambda n, tb: (n * steps + tb, 0))] * 3
        + [pl.BlockSpec((KV_ROWS, ATT_OUT), lambda n, tb, g=g: (n * steps + tb, v_col + g)) for g in range(3)],
        out_specs=out_specs,
        compiler_params=_params("arbitrary", "arbitrary"),
        name="prompt_kv_caches",
    )(*k_rot, qkv, qkv, qkv)


def _lru_gates(uc, wa_ref, ba, wx_ref, bx, lam, dot=_bdot):
    n_blk = uc.shape[1] // LRU_BLOCK
    r = jnp.concatenate([dot(uc[:, j * LRU_BLOCK:(j + 1) * LRU_BLOCK], wa_ref[j]) for j in range(n_blk)], axis=1)
    i = jnp.concatenate([dot(uc[:, j * LRU_BLOCK:(j + 1) * LRU_BLOCK], wx_ref[j]) for j in range(n_blk)], axis=1)
    r = _sigmoid(r + ba)
    i = _sigmoid(i + bx)
    neg = -lam
    softplus = jnp.maximum(neg, 0.0) + jnp.log1p(jnp.exp(-jnp.abs(neg)))
    log_a = (-LRU_C * r) * softplus
    a = jnp.exp(log_a)
    b = jnp.sqrt(-jnp.tanh(log_a) * (a * a + 1.0)) * (i * uc)
    return a, b


def _lru_kernel(u_ref, g_ref, cw_ref, cb_ref, wa_ref, ba_ref, wx_ref, bx_ref, lam_ref,
                y_ref, hl_ref, hist, carry):
    @pl.when(pl.program_id(2) == 0)
    def _():
        hist[...] = jnp.zeros_like(hist)
        carry[...] = jnp.zeros_like(carry)

    u = u_ref[...]
    cw = cw_ref[...]
    ext = jnp.concatenate([hist[...], u], axis=0)
    off = SUBLANES - (CONV_W - 1)
    conv = ext[off:off + LRU_ROWS] * cw[0:1]
    for t in range(1, CONV_W):
        conv = conv + ext[off + t:off + t + LRU_ROWS] * cw[t:t + 1]
    uc = cb_ref[...] + conv
    hist[...] = u[LRU_ROWS - SUBLANES:]

    a, b = _lru_gates(uc, wa_ref, ba_ref[...], wx_ref, bx_ref[...], lam_ref[...])

    step = lax.broadcasted_iota(I32, a.shape, 0) & (SUBLANES - 1)
    for s in (1, 2, 4):
        a_s = pltpu.roll(a, s, 0)
        b_s = pltpu.roll(b, s, 0)
        valid = step >= s
        b = jnp.where(valid, a * b_s + b, b)
        a = jnp.where(valid, a * a_s, a)
    h_prev = carry[0:1, :]
    hs = []
    for j in range(LRU_ROWS // SUBLANES):
        h_j = a[j * SUBLANES:(j + 1) * SUBLANES] * h_prev + b[j * SUBLANES:(j + 1) * SUBLANES]
        h_prev = h_j[SUBLANES - 1:SUBLANES]
        hs.append(h_j)
    h = jnp.concatenate(hs, axis=0)
    carry[...] = jnp.broadcast_to(h_prev, carry.shape)
    y_ref[...] = (h * _gelu_tanh(g_ref[...])).astype(y_ref.dtype)
    hl_ref[...] = h_prev


def _prompt_lru(ug, conv_w, conv_b, wa, ba, wx, bx, lam):
    t_blocks = SEQ // LRU_ROWS
    c_blocks = LRU_WIDTH // LRU_COLS
    row = pl.BlockSpec((1, LRU_COLS), lambda n, cb, tb: (0, cb))
    gate_w = pl.BlockSpec((LRU_COLS // LRU_BLOCK, LRU_BLOCK, LRU_BLOCK), lambda n, cb, tb: (cb, 0, 0))
    return pl.pallas_call(
        _lru_kernel,
        out_shape=[jax.ShapeDtypeStruct((N_PROMPT, LRU_WIDTH), BF16),
                   jax.ShapeDtypeStruct((BATCH, 1, LRU_WIDTH), F32)],
        grid=(BATCH, c_blocks, t_blocks),
        in_specs=[pl.BlockSpec((LRU_ROWS, LRU_COLS), lambda n, cb, tb: (n * t_blocks + tb, cb)),
                  pl.BlockSpec((LRU_ROWS, LRU_COLS), lambda n, cb, tb: (n * t_blocks + tb, c_blocks + cb)),
                  pl.BlockSpec((CONV_W, LRU_COLS), lambda n, cb, tb: (0, cb)),
                  row, gate_w, row, gate_w, row, row],
        out_specs=[pl.BlockSpec((LRU_ROWS, LRU_COLS), lambda n, cb, tb: (n * t_blocks + tb, cb)),
                   pl.BlockSpec((None, 1, LRU_COLS), lambda n, cb, tb: (n, 0, cb))],
        scratch_shapes=[pltpu.VMEM((SUBLANES, LRU_COLS), F32), pltpu.VMEM((SUBLANES, LRU_COLS), F32)],
        compiler_params=_params("arbitrary", "arbitrary", "arbitrary"),
        name="prompt_lru",
    )(ug, ug, conv_w, conv_b, wa, ba, wx, bx, lam)


def _sample_lru_kernel(h_ref_in, sc_ref, h0_ref, cw_ref, cb_ref, wa_ref, ba_ref, wx_ref, bx_ref, lam_ref,
                       y_ref, h_ref):
    u = h_ref_in[:, 3 * ATT_WIDTH:3 * ATT_WIDTH + LRU_WIDTH]
    g = h_ref_in[:, 3 * ATT_WIDTH + LRU_WIDTH:3 * ATT_WIDTH + 2 * LRU_WIDTH]
    cw = cw_ref[...]
    conv = sc_ref[:, 0:LRU_WIDTH] * cw[0:1]
    for t in range(1, CONV_W - 1):
        conv = conv + sc_ref[:, t * LRU_WIDTH:(t + 1) * LRU_WIDTH] * cw[t:t + 1]
    conv = conv + u * cw[CONV_W - 1:CONV_W]
    uc = cb_ref[...] + conv
    a, b = _lru_gates(uc, wa_ref, ba_ref[...], wx_ref, bx_ref[...], lam_ref[...], dot=_hdot)
    h = b + a * h0_ref[...]
    h_ref[...] = h
    y_ref[...] = h * _gelu_tanh(g)


def _sample_lru(h_in, state_conv, h0, conv_w, conv_b, wa, ba, wx, bx, lam):
    args = (h_in, state_conv, h0, conv_w, conv_b, wa, ba, wx, bx, lam)
    full = lambda shape: pl.BlockSpec(shape, lambda i: (0,) * len(shape))
    return pl.pallas_call(
        _sample_lru_kernel,
        out_shape=[jax.ShapeDtypeStruct((DEC_BATCH, LRU_WIDTH), F32)] * 2,
        grid=(1,),
        in_specs=[full(a.shape) for a in args],
        out_specs=[full((DEC_BATCH, LRU_WIDTH))] * 2,
        compiler_params=_params("arbitrary"),
        name="sample_lru",
    )(*args)


def _mem_attn_kernel(q_ref, k_ref, v_ref, y_ref):
    dn = (((1,), (1,)), ((), ()))
    s = lax.dot_general(q_ref[...].astype(BF16), k_ref[...].astype(BF16), dn,
                        preferred_element_type=F32) * (MEM_HEAD_DIM ** -0.5)
    m = jnp.max(s, axis=-1, keepdims=True)
    p = jnp.exp(s - m)
    p = p / jnp.sum(p, axis=-1, keepdims=True)
    y_ref[...] = _bdot(p, v_ref[...]).astype(y_ref.dtype)


def _prompt_mem_attention(ug, mem_kv):
    rows = 1024
    r_blocks = SEQ // rows
    q_col = 2 * LRU_WIDTH // MEM_HEAD_DIM
    return pl.pallas_call(
        _mem_attn_kernel,
        out_shape=jax.ShapeDtypeStruct((N_PROMPT, MEM_WIDTH), BF16),
        grid=(BATCH, MEM_HEADS, r_blocks),
        in_specs=[pl.BlockSpec((rows, MEM_HEAD_DIM), lambda n, h, r: (n * r_blocks + r, q_col + h)),
                  pl.BlockSpec((MEM_TOKENS, MEM_HEAD_DIM), lambda n, h, r: (n, h)),
                  pl.BlockSpec((MEM_TOKENS, MEM_HEAD_DIM), lambda n, h, r: (n, MEM_HEADS + h))],
        out_specs=pl.BlockSpec((rows, MEM_HEAD_DIM), lambda n, h, r: (n * r_blocks + r, h)),
        compiler_params=_params("arbitrary", "arbitrary", "arbitrary"),
        name="prompt_mem_attention",
    )(ug, mem_kv, mem_kv)


def _heads_attend(q, keys, values, scale, extra=None):
    s = jnp.sum(keys * q[None], axis=-1, keepdims=True) * scale
    m = jnp.max(s, axis=0)
    if extra is not None:
        s_x = jnp.sum(extra[0] * q, axis=-1, keepdims=True) * scale
        m = jnp.maximum(m, s_x)
        p_x = jnp.exp(s_x - m)
    p = jnp.exp(s - m[None])
    den = jnp.sum(p, axis=0)
    if extra is not None:
        den = den + p_x
    o = jnp.sum((p / den[None]) * values, axis=0)
    if extra is not None:
        o = o + (p_x / den) * extra[1]
    return o, m + jnp.log(den)


def _sample_attn_kernel(h_ref, c_ref, sa_ref, sb_ref, w0_ref, w1_ref, w2_ref, mem_ref, ya_ref, ym_ref, kr_ref):
    c, sa, sb = c_ref[...], sa_ref[...], sb_ref[...]
    outs, lses = [], []
    for g, w_ref in enumerate((w0_ref, w1_ref, w2_ref)):
        q, k_new, v_new = [], [], []
        for h in range(HEADS):
            col = (g * HEADS + h) * HEAD_DIM
            q.append(_rope(h_ref[:, col:col + HEAD_DIM], c, sa, sb))
            k_new.append(_rope(h_ref[:, ATT_WIDTH + col:ATT_WIDTH + col + HEAD_DIM], c, sa, sb))
            v_new.append(h_ref[:, 2 * ATT_WIDTH + col:2 * ATT_WIDTH + col + HEAD_DIM])
            kr_ref[:, col:col + HEAD_DIM] = k_new[h]
        stack = lambda rows: jnp.concatenate(rows, axis=0)
        o, lse = _heads_attend(stack(q), w_ref[:, 0], w_ref[:, 1], HEAD_DIM ** -0.5,
                               extra=(stack(k_new), stack(v_new)))
        outs.append(o)
        lses.append(lse)
    y = _group_merge(lses, outs, operand=_keep_f32)
    for h in range(HEADS):
        ya_ref[:, h * HEAD_DIM:(h + 1) * HEAD_DIM] = y[h:h + 1]
    q_col = 3 * ATT_WIDTH + 2 * LRU_WIDTH
    qm = jnp.concatenate([h_ref[:, q_col + h * MEM_HEAD_DIM:q_col + (h + 1) * MEM_HEAD_DIM]
                          for h in range(MEM_HEADS)], axis=0)
    om, _ = _heads_attend(qm, mem_ref[:, 0], mem_ref[:, 1], MEM_HEAD_DIM ** -0.5)
    for h in range(MEM_HEADS):
        ym_ref[:, h * MEM_HEAD_DIM:(h + 1) * MEM_HEAD_DIM] = om[h:h + 1]


def _sample_attention(h_in, c, sa, sb, caches, cache_mem):
    full = lambda shape: pl.BlockSpec(shape, lambda i: (0,) * len(shape))
    row = lambda w: pl.BlockSpec((None, 1, w), lambda i: (i, 0, 0))
    window = pl.BlockSpec((None, ATT_BLOCK, None, 2, HEADS, HEAD_DIM), lambda i: (i, 0, 0, 0, 0, 0))
    widths = (ATT_OUT, MEM_WIDTH, ATT_WIDTH)
    y_att, y_mem, k_rot = pl.pallas_call(
        _sample_attn_kernel,
        out_shape=[jax.ShapeDtypeStruct((DEC_BATCH, 1, w), F32) for w in widths],
        grid=(DEC_BATCH,),
        in_specs=[row(h_in.shape[1]), full(c.shape), full(sa.shape), full(sb.shape), window, window, window,
                  pl.BlockSpec((None, MEM_TOKENS, 2, MEM_HEADS, MEM_HEAD_DIM), lambda i: (i, 0, 0, 0, 0))],
        out_specs=[row(w) for w in widths],
        compiler_params=_params("arbitrary"),
        name="sample_attention",
    )(h_in.reshape(DEC_BATCH, 1, -1), c, sa, sb, *caches, cache_mem)
    return [a.reshape(DEC_BATCH, -1) for a in (y_att, y_mem, k_rot)]


def _merge_kernel(ya_ref, yl_ref, ym_ref, ga_ref, gl_ref, gm_ref, pa_ref, pl_ref, pm_ref, o_ref):
    merged = _sigmoid(ga_ref[...]) * _bdot(ya_ref[...], pa_ref[...])
    merged = merged + _sigmoid(gl_ref[...]) * _bdot(yl_ref[...], pl_ref[...])
    merged = merged + _sigmoid(gm_ref[...]) * _bdot(ym_ref[...], pm_ref[...])
    o_ref[...] = merged.astype(o_ref.dtype)


def _merge(y_att, y_lru, y_mem, gates, p_att, p_lru, p_mem):
    tm = 256
    rows = lambda w: pl.BlockSpec((tm, w), lambda i: (i, 0))
    full = lambda a: pl.BlockSpec(a.shape, lambda i: (0, 0))
    return pl.pallas_call(
        _merge_kernel,
        out_shape=jax.ShapeDtypeStruct((N_PROMPT, D_MODEL), BF16),
        grid=(N_PROMPT // tm,),
        in_specs=[rows(ATT_OUT), rows(LRU_WIDTH), rows(MEM_WIDTH)]
        + [pl.BlockSpec((tm, D_MODEL), lambda i, b=b: (i, b)) for b in range(3)]
        + [full(p_att), full(p_lru), full(p_mem)],
        out_specs=rows(D_MODEL),
        compiler_params=_params("arbitrary"),
        name="branch_merge",
    )(y_att, y_lru, y_mem, gates, gates, gates, p_att, p_lru, p_mem)


def _sample_merge_kernel(h_ref, pa_ref, pl_ref, pm_ref, o_ref):
    gate_col = 3 * ATT_WIDTH + UG_WIDTH
    gate = lambda b: _sigmoid(h_ref[:, gate_col + b * D_MODEL:gate_col + (b + 1) * D_MODEL])
    o_ref[...] = (gate(0) * pa_ref[...] + gate(1) * pl_ref[...]) + gate(2) * pm_ref[...]


def _sample_merge(h_in, pa, pl_, pm):
    args = (h_in, pa, pl_, pm)
    full = lambda a: pl.BlockSpec(a.shape, lambda i: (0, 0))
    return pl.pallas_call(
        _sample_merge_kernel,
        out_shape=jax.ShapeDtypeStruct((DEC_BATCH, D_MODEL), F32),
        grid=(1,),
        in_specs=[full(a) for a in args],
        out_specs=pl.BlockSpec((DEC_BATCH, D_MODEL), lambda i: (0, 0)),
        compiler_params=_params("arbitrary"),
        name="sample_merge",
    )(*args)


def _ln_route(mix, x_ref, g_ref, b_ref, wr_ref, br_ref, x1_ref, e_ref, gt_ref, dot):
    x1 = _layer_norm(DN_ALPHA * x_ref[...] + mix, g_ref[...], b_ref[...])
    x1_ref[...] = x1
    logits = dot(x1, wr_ref[...]) + br_ref[...]
    lane = lax.broadcasted_iota(I32, logits.shape, 1).astype(F32)
    out_lane = lax.broadcasted_iota(I32, e_ref.shape, 1)
    top_v = []
    e_out = jnp.zeros(e_ref.shape, I32)
    for k in range(TOP_K):
        v = jnp.max(logits, axis=-1, keepdims=True)
        e = jnp.min(jnp.where(logits == v, lane, float(N_EXPERTS)), axis=-1, keepdims=True)
        logits = jnp.where(lane == e, -jnp.inf, logits)
        top_v.append(v)
        e_out = jnp.where(out_lane == k, e.astype(I32), e_out)
    ps = [jnp.exp(v - top_v[0]) for v in top_v]
    den = ps[0] + ps[1] + ps[2] + ps[3]
    g_out = jnp.zeros(gt_ref.shape, F32)
    for k in range(TOP_K):
        g_out = jnp.where(out_lane == k, ps[k] / den, g_out)
    e_ref[...] = e_out
    gt_ref[...] = g_out


def _outproj_kernel(m_ref, x_ref, w_ref, g_ref, b_ref, wr_ref, br_ref, x1_ref, x1r_ref, e_ref, gt_ref):
    @pl.when(pl.program_id(0) < N_PROMPT // OUTPROJ_ROWS)
    def _():
        mix = jnp.dot(m_ref[...], w_ref[...], preferred_element_type=F32)
        _ln_route(mix, x_ref, g_ref, b_ref, wr_ref, br_ref, x1_ref, e_ref, gt_ref, _bdot)
        _to_token_rows(x1_ref[...], x1r_ref, OUTPROJ_ROWS)

    @pl.when(pl.program_id(0) >= N_PROMPT // OUTPROJ_ROWS)
    def _():
        for ref in (x1_ref, x1r_ref, e_ref, gt_ref):
            ref[...] = jnp.zeros_like(ref)


def _sample_ln_router_kernel(mix_ref, x_ref, g_ref, b_ref, wr_ref, br_ref, x1_ref, e_ref, gt_ref):
    _ln_route(mix_ref[...], x_ref, g_ref, b_ref, wr_ref, br_ref, x1_ref, e_ref, gt_ref, _hdot)


def _sample_ln_router(mix, x, ln_g, ln_b, w_router, b_router):
    args = (mix, x, ln_g, ln_b, w_router, b_router)
    full = lambda shape: pl.BlockSpec(shape, lambda i: (0, 0))
    out_shapes = [jax.ShapeDtypeStruct((DEC_BATCH, D_MODEL), F32), jax.ShapeDtypeStruct((DEC_BATCH, HEAD_DIM), I32),
                  jax.ShapeDtypeStruct((DEC_BATCH, HEAD_DIM), F32)]
    return pl.pallas_call(
        _sample_ln_router_kernel,
        out_shape=out_shapes,
        grid=(1,),
        in_specs=[full(a.shape) for a in args],
        out_specs=[full(s.shape) for s in out_shapes],
        compiler_params=_params("arbitrary"),
        name="sample_ln_router",
    )(*args)


def _outproj_router(merged, x, w_out, ln_g, ln_b, w_router, b_router):
    tm = OUTPROJ_ROWS
    last = N_PROMPT // tm - 1
    in_rows = lambda w: pl.BlockSpec((tm, w), lambda i: (jnp.minimum(i, last), 0))
    rows = lambda w: pl.BlockSpec((tm, w), lambda i: (i, 0))
    full = lambda a: pl.BlockSpec(a.shape, lambda i: (0, 0))
    return pl.pallas_call(
        _outproj_kernel,
        out_shape=[jax.ShapeDtypeStruct((TOKENS, D_MODEL), F32),
                   jax.ShapeDtypeStruct((TOKENS * TOKEN_ROWS, LANES), F32),
                   jax.ShapeDtypeStruct((TOKENS, HEAD_DIM), I32),
                   jax.ShapeDtypeStruct((TOKENS, HEAD_DIM), F32)],
        grid=(TOKENS // tm,),
        in_specs=[in_rows(D_MODEL), in_rows(D_MODEL), full(w_out), full(ln_g), full(ln_b), full(w_router),
                  full(b_router)],
        out_specs=[rows(D_MODEL), pl.BlockSpec((tm * TOKEN_ROWS, LANES), lambda i: (i, 0)), rows(HEAD_DIM),
                   rows(HEAD_DIM)],
        compiler_params=_params("arbitrary"),
        name="outproj_ln_router",
    )(merged, x, w_out, ln_g, ln_b, w_router, b_router)


def _to_token_rows(x, ref, n):
    for j in range(TOKEN_ROWS):
        ref[pl.ds(j, n, stride=TOKEN_ROWS), :] = x[:, j * LANES:(j + 1) * LANES]


def _from_token_rows(ref, first, n):
    return jnp.concatenate([ref[pl.ds(first * TOKEN_ROWS + j, n, stride=TOKEN_ROWS), :] for j in range(TOKEN_ROWS)],
                           axis=1)


def _gather_pipeline(t, last, idx_hbm, src_hbm, bufs, idx_smem, idx_sem, row_sem, per_step, variants):
    depth = GATHER_DEPTH

    def idx_copy(step, slot):
        return pltpu.make_async_copy(idx_hbm.at[step], idx_smem.at[slot], idx_sem.at[slot])

    def row_copy(slot, r):
        tok = idx_smem[slot, 0, r]
        return pltpu.make_async_copy(src_hbm.at[pl.ds(tok * TOKEN_ROWS, TOKEN_ROWS)],
                                     bufs[slot].at[pl.ds(r * TOKEN_ROWS, TOKEN_ROWS)], row_sem.at[slot])

    def wait_rows(slot):
        pltpu.make_async_copy(src_hbm.at[pl.ds(0, per_step * TOKEN_ROWS)], bufs[slot], row_sem.at[slot]).wait()

    @pl.when(t == 0)
    def _():
        for s in range(depth - 1):
            idx_copy(s, s).start()
            idx_copy(s, s).wait()

            def first(r, carry, s=s):
                row_copy(s, r).start()
                return carry

            lax.fori_loop(0, per_step, first, 0)
        idx_copy(depth - 1, depth - 1).start()

    def step(slot, body):
        ahead = (slot + depth - 1) % depth
        wait_rows(slot)
        idx_copy(t + depth - 1, ahead).wait()
        for r in range(per_step):
            row_copy(ahead, r).start()
        idx_copy(t + depth, slot).start()
        body(bufs[slot])

    for slot in range(depth):
        mine = t % depth == slot
        for cond, body in variants:
            pl.when(mine if cond is None else jnp.logical_and(mine, cond))(
                lambda slot=slot, body=body: step(slot, body))

    @pl.when(t == last)
    def _():
        idx_copy(0, last % depth).wait()
        for s in range(1, depth):
            wait_rows((last + s) % depth)


def _gather_scratch(per_step):
    return ([pltpu.VMEM((per_step * TOKEN_ROWS, LANES), F32)] * GATHER_DEPTH
            + [pltpu.SMEM((GATHER_DEPTH, 1, per_step), I32), pltpu.SemaphoreType.DMA((GATHER_DEPTH,)),
               pltpu.SemaphoreType.DMA((GATHER_DEPTH,))])


def _expert_changed(be_ref, i):
    return jnp.logical_or(i == 0, be_ref[i] != be_ref[jnp.maximum(i - 1, 0)])


def _gate_up_kernel(be_ref, nu_ref, tok_hbm, x_hbm, wg_ref, wu_ref, bg_ref, bu_ref, o_ref,
                    wg_b, wu_b, *gather):
    i = pl.program_id(1)
    t = pl.program_id(0) * MOE_BLOCKS + i
    last = D_FF // FF_TILE * MOE_BLOCKS - 1

    @pl.when(_expert_changed(be_ref, i))
    def _():
        wg_b[...] = wg_ref[...].astype(BF16)
        wu_b[...] = wu_ref[...].astype(BF16)

    def compute(buf):
        x = _from_token_rows(buf, 0, MOE_ROWS).astype(BF16)
        g = jnp.dot(x, wg_b[...], preferred_element_type=F32) + bg_ref[...]
        u = jnp.dot(x, wu_b[...], preferred_element_type=F32) + bu_ref[...]
        g = jnp.minimum(g, SWIGLU_LIMIT)
        u = jnp.clip(u, -SWIGLU_LIMIT, SWIGLU_LIMIT)
        o_ref[...] = ((u + 1.0) * (g * _sigmoid(SWIGLU_ALPHA * g))).astype(o_ref.dtype)

    def unused(buf):
        o_ref[...] = jnp.zeros_like(o_ref)

    used = i < nu_ref[0]
    _gather_pipeline(t, last, tok_hbm, x_hbm, gather[:GATHER_DEPTH], *gather[GATHER_DEPTH:], MOE_ROWS,
                     [(used, compute), (jnp.logical_not(used), unused)])


def _expert_gate_up(blk_expert, n_used, slot_tok, x1_rows, w_gate_up, b_gate_up):
    ff_tiles = D_FF // FF_TILE
    blocks = slot_tok.reshape(MOE_BLOCKS, 1, MOE_ROWS)
    tok_steps = jnp.concatenate([blocks] * ff_tiles + [blocks[:GATHER_DEPTH]], axis=0)
    return pl.pallas_call(
        _gate_up_kernel,
        out_shape=jax.ShapeDtypeStruct((MOE_SLOTS, D_FF), BF16),
        grid_spec=pltpu.PrefetchScalarGridSpec(
            num_scalar_prefetch=2,
            grid=(ff_tiles, MOE_BLOCKS),
            in_specs=[pl.BlockSpec(memory_space=pl.ANY), pl.BlockSpec(memory_space=pl.ANY),
                      pl.BlockSpec((None, D_MODEL, FF_TILE), lambda j, i, be, nu: (be[i], 0, j)),
                      pl.BlockSpec((None, D_MODEL, FF_TILE), lambda j, i, be, nu: (be[i], 0, ff_tiles + j)),
                      pl.BlockSpec((None, 1, FF_TILE), lambda j, i, be, nu: (be[i], 0, j)),
                      pl.BlockSpec((None, 1, FF_TILE), lambda j, i, be, nu: (be[i], 0, ff_tiles + j))],
            out_specs=pl.BlockSpec((MOE_ROWS, FF_TILE), lambda j, i, be, nu: (i, j)),
            scratch_shapes=[pltpu.VMEM((D_MODEL, FF_TILE), BF16), pltpu.VMEM((D_MODEL, FF_TILE), BF16)]
            + _gather_scratch(MOE_ROWS)),
        compiler_params=_params("arbitrary", "arbitrary"),
        name="moe_gate_up",
    )(blk_expert, n_used, tok_steps, x1_rows, w_gate_up, w_gate_up, b_gate_up, b_gate_up)


def _down_kernel(be_ref, nu_ref, h_ref, w_ref, b_ref, o_ref, w_b):
    i = pl.program_id(1)

    @pl.when(_expert_changed(be_ref, i))
    def _():
        w_b[...] = w_ref[...].astype(BF16)

    @pl.when(i < nu_ref[0])
    def _():
        y = jnp.dot(h_ref[...], w_b[...], preferred_element_type=F32) + b_ref[...]
        _to_token_rows(y, o_ref, MOE_ROWS)

    @pl.when(i >= nu_ref[0])
    def _():
        o_ref[...] = jnp.zeros_like(o_ref)


def _expert_down(blk_expert, n_used, hdn, w_down, b_down):
    row_blk = lambda j, i, be, nu: jnp.minimum(i, nu[0] - 1)
    return pl.pallas_call(
        _down_kernel,
        out_shape=jax.ShapeDtypeStruct((MOE_SLOTS * TOKEN_ROWS, LANES), F32),
        grid_spec=pltpu.PrefetchScalarGridSpec(
            num_scalar_prefetch=2,
            grid=(1, MOE_BLOCKS),
            in_specs=[pl.BlockSpec((MOE_ROWS, D_FF), lambda j, i, be, nu: (row_blk(j, i, be, nu), 0)),
                      pl.BlockSpec((None, D_FF, D_MODEL), lambda j, i, be, nu: (be[i], 0, 0)),
                      pl.BlockSpec((None, 1, D_MODEL), lambda j, i, be, nu: (be[i], 0, 0))],
            out_specs=pl.BlockSpec((MOE_ROWS * TOKEN_ROWS, LANES), lambda j, i, be, nu: (i, 0)),
            scratch_shapes=[pltpu.VMEM((D_FF, D_MODEL), BF16)]),
        compiler_params=_params("arbitrary", "arbitrary"),
        name="moe_down",
    )(blk_expert, n_used, hdn, w_down, b_down)


def _combine_kernel(dest_hbm, ys_hbm, gate_ref, x1_ref, g_ref, b_ref, op_ref, ot_ref, *gather):
    i = pl.program_id(0)
    prompt_steps = N_PROMPT // COMBINE_TOKENS

    def compute(buf):
        gate = gate_ref[...]
        ffn = None
        for k in range(TOP_K):
            term = _from_token_rows(buf, k * COMBINE_TOKENS, COMBINE_TOKENS) * gate[:, k:k + 1]
            ffn = term if ffn is None else ffn + term
        y = _layer_norm(DN_ALPHA * x1_ref[...] + ffn, g_ref[...], b_ref[...])

        @pl.when(i < prompt_steps)
        def _():
            op_ref[...] = y

        @pl.when(i >= prompt_steps)
        def _():
            ot_ref[...] = y

    _gather_pipeline(i, TOKENS // COMBINE_TOKENS - 1, dest_hbm, ys_hbm, gather[:GATHER_DEPTH],
                     *gather[GATHER_DEPTH:], COMBINE_TOKENS * TOP_K, [(None, compute)])


def _combine(dest, y_slots, gate, x1, ln_g, ln_b):
    steps = TOKENS // COMBINE_TOKENS
    prompt_steps = N_PROMPT // COMBINE_TOKENS
    per_step = COMBINE_TOKENS * TOP_K
    rows = lambda w: pl.BlockSpec((COMBINE_TOKENS, w), lambda i: (i, 0))
    full = lambda a: pl.BlockSpec(a.shape, lambda i: (0, 0))
    order = dest.reshape(steps, COMBINE_TOKENS, TOP_K).transpose(0, 2, 1).reshape(steps, 1, per_step)
    order = jnp.concatenate([order, order[:GATHER_DEPTH]], axis=0)
    return pl.pallas_call(
        _combine_kernel,
        out_shape=[jax.ShapeDtypeStruct((N_PROMPT, D_MODEL), F32),
                   jax.ShapeDtypeStruct((TOKENS - N_PROMPT, D_MODEL), F32)],
        grid=(steps,),
        in_specs=[pl.BlockSpec(memory_space=pl.ANY), pl.BlockSpec(memory_space=pl.ANY),
                  rows(HEAD_DIM), rows(D_MODEL), full(ln_g), full(ln_b)],
        out_specs=[pl.BlockSpec((COMBINE_TOKENS, D_MODEL), lambda i: (jnp.minimum(i, prompt_steps - 1), 0)),
                   pl.BlockSpec((COMBINE_TOKENS, D_MODEL), lambda i: (jnp.maximum(i - prompt_steps, 0), 0))],
        scratch_shapes=_gather_scratch(per_step),
        compiler_params=_params("arbitrary"),
        name="moe_combine_ln",
    )(order, y_slots, gate, x1, ln_g, ln_b)


def _expert_onehot(e_ref, k):
    e = e_ref[...]
    lane = lax.broadcasted_iota(I32, e.shape, 1)
    return e[:, k:k + 1] == lane


def _rank_kernel(e_ref, tri_ref, rank_ref, cnt_ref, base):
    @pl.when(pl.program_id(0) == 0)
    def _():
        base[...] = jnp.zeros_like(base)

    lane = lax.broadcasted_iota(I32, rank_ref.shape, 1)
    seen = base[0:1, :]
    out = jnp.zeros(rank_ref.shape, F32)
    for k in range(TOP_K):
        hit = _expert_onehot(e_ref, k)
        ones = jnp.where(hit, 1.0, 0.0)
        before = jnp.dot(tri_ref[...], ones.astype(BF16), preferred_element_type=F32) + seen
        out = jnp.where(lane == k, jnp.sum(jnp.where(hit, before, 0.0), axis=-1, keepdims=True), out)
        seen = seen + jnp.sum(ones, axis=0, keepdims=True)
    base[...] = jnp.broadcast_to(seen, base.shape)
    rank_ref[...] = out.astype(I32)
    cnt_ref[...] = jnp.broadcast_to(seen, cnt_ref.shape).astype(I32)


def _dest_kernel(e_ref, rank_ref, start_ref, dest_ref):
    lane = lax.broadcasted_iota(I32, dest_ref.shape, 1)
    out = jnp.zeros(dest_ref.shape, F32)
    for k in range(TOP_K):
        start = jnp.sum(jnp.where(_expert_onehot(e_ref, k), start_ref[0:1, :], 0.0), axis=-1, keepdims=True)
        out = jnp.where(lane == k, start, out)
    dest_ref[...] = out.astype(I32) + rank_ref[...]


def _moe_layout(top_e):
    tm = 512
    steps = -(-TOKENS // tm)
    tri = (jnp.arange(tm)[:, None] > jnp.arange(tm)[None, :]).astype(BF16)
    rows = pl.BlockSpec((tm, LANES), lambda i: (i, 0))
    fixed = lambda shape: pl.BlockSpec(shape, lambda i: (0, 0))
    e_pad = jnp.pad(top_e, ((0, steps * tm - TOKENS), (0, 0)), constant_values=N_EXPERTS)
    rank, counts = pl.pallas_call(
        _rank_kernel,
        out_shape=[jax.ShapeDtypeStruct((steps * tm, LANES), I32), jax.ShapeDtypeStruct((SUBLANES, LANES), I32)],
        grid=(steps,),
        in_specs=[rows, fixed((tm, tm))],
        out_specs=[rows, fixed((SUBLANES, LANES))],
        scratch_shapes=[pltpu.VMEM((SUBLANES, LANES), F32)],
        compiler_params=_params("arbitrary"),
        name="moe_rank",
    )(e_pad, tri)
    counts = counts[0, :N_EXPERTS]
    padded = (counts + MOE_ROWS - 1) // MOE_ROWS * MOE_ROWS
    pad_end = jnp.cumsum(padded)
    starts = jnp.zeros((SUBLANES, LANES), F32).at[:, :N_EXPERTS].set((pad_end - padded).astype(F32))
    dest = pl.pallas_call(
        _dest_kernel,
        out_shape=jax.ShapeDtypeStruct((steps * tm, LANES), I32),
        grid=(steps,),
        in_specs=[rows, rows, fixed((SUBLANES, LANES))],
        out_specs=rows,
        compiler_params=_params("arbitrary"),
        name="moe_dest",
    )(e_pad, rank, starts)[:TOKENS, :TOP_K]
    tok = jnp.broadcast_to(jnp.arange(TOKENS, dtype=I32)[:, None], (TOKENS, TOP_K))
    slot_tok = jnp.zeros((MOE_SLOTS,), I32).at[dest.reshape(-1)].set(
        tok.reshape(-1), unique_indices=True, mode="promise_in_bounds")
    n_used = pad_end[-1] // MOE_ROWS
    blk = jnp.minimum(jnp.arange(MOE_BLOCKS, dtype=I32), n_used - 1) * MOE_ROWS
    blk_expert = jnp.sum((pad_end[None, :] <= blk[:, None]).astype(I32), axis=1)
    return dest, slot_tok, jnp.minimum(blk_expert, N_EXPERTS - 1), n_used.reshape(1).astype(I32)


def kernel(x_prompt, x_sample, mem_prompt, cache_kv_w128, cache_kv_w512, cache_kv_w2048, cache_mem_kv, state_conv, state_lru_h, w_in, conv_w, conv_b, w_rg_a, b_rg_a, w_rg_x, b_rg_x, lru_lambda, w_mem_kv, p_att, p_lru, p_mem, w_out, ln1_g, ln1_b, w_router, b_router, w_gate_up, b_gate_up, w_down, b_down, ln2_g, ln2_b):
    row = lambda a: a.reshape(1, -1)
    layer = lambda a: a.reshape(a.shape[1:])
    x_p = x_prompt.reshape(N_PROMPT, D_MODEL)
    x_s = x_sample.reshape(DEC_BATCH, D_MODEL)
    xb = x_p.astype(BF16)
    w_in_b = layer(w_in).astype(BF16)
    qkv_cols = 3 * ATT_WIDTH
    gate_col = qkv_cols + UG_WIDTH

    qkv = _matmul(xb, w_in_b[:, :qkv_cols], 1024, 1536, "in_proj_qkv")
    ug = _matmul(xb, w_in_b[:, qkv_cols:gate_col], 1024, 1024, "in_proj_lru_mem")
    gates = _matmul(xb, w_in_b[:, gate_col:], 1024, 1536, "in_proj_gates")

    c_p, sa_p, sb_p = _rope_tables(jnp.arange(SEQ, dtype=I32))
    y_att_p, k_rot = _prompt_attention(qkv, c_p, sa_p, sb_p)
    wa, wx = layer(w_rg_a).astype(BF16), layer(w_rg_x).astype(BF16)
    lru_args = (layer(conv_w), row(conv_b), wa, row(b_rg_a), wx, row(b_rg_x), row(lru_lambda))
    y_lru_p, h_last_p = _prompt_lru(ug, *lru_args)
    mem_kv_p = _matmul(mem_prompt.reshape(BATCH * MEM_TOKENS, D_MODEL), layer(w_mem_kv).astype(BF16), 512, 1024,
                       "mem_kv_proj")
    y_mem_p = _prompt_mem_attention(ug, mem_kv_p)

    merged = _merge(y_att_p, y_lru_p, y_mem_p, gates,
                    layer(p_att).astype(BF16), layer(p_lru).astype(BF16), layer(p_mem).astype(BF16))
    x1, x1_rows, top_e, gate = _outproj_router(merged, x_p, layer(w_out).astype(BF16), row(ln1_g), row(ln1_b),
                                               layer(w_router).astype(BF16), row(b_router))

    h_s_in = _matmul(x_s, layer(w_in), DEC_BATCH, 512, "sample_in_proj", full_precision=True)
    c_s, sa_s, sb_s = _rope_tables(jnp.full((1,), PAST_LEN, I32))
    caches = [cache.reshape(DEC_BATCH, ATT_BLOCK, dil, 2, HEADS, HEAD_DIM)
              for cache, dil in zip((cache_kv_w128, cache_kv_w512, cache_kv_w2048), DILATIONS)]
    y_att_s, y_mem_s, k_rot_s = _sample_attention(h_s_in, c_s, sa_s, sb_s, caches, layer(cache_mem_kv))
    y_lru_s, h_s = _sample_lru(h_s_in, state_conv.reshape(DEC_BATCH, (CONV_W - 1) * LRU_WIDTH), layer(state_lru_h),
                               layer(conv_w), row(conv_b), layer(w_rg_a), row(b_rg_a), layer(w_rg_x), row(b_rg_x),
                               row(lru_lambda))
    hi_mm = lambda a, w, name: _matmul(a, layer(w), DEC_BATCH, 512, name, full_precision=True)
    merged_s = _sample_merge(h_s_in, hi_mm(y_att_s, p_att, "sample_p_att"), hi_mm(y_lru_s, p_lru, "sample_p_lru"),
                             hi_mm(y_mem_s, p_mem, "sample_p_mem"))
    mix_s = hi_mm(merged_s, w_out, "sample_out_proj")
    x1_s, top_e_s, gate_s = _sample_ln_router(mix_s, x_s, row(ln1_g), row(ln1_b), layer(w_router), row(b_router))

    pad = TOKENS - N_PROMPT - DEC_BATCH
    tail = lambda s, p: jnp.concatenate([s, p], axis=0)
    pad_e = (jnp.arange(pad, dtype=I32)[:, None] * TOP_K + jnp.arange(HEAD_DIM, dtype=I32)[None, :]) % N_EXPERTS
    x1_tail = tail(x1_s, jnp.zeros((pad, D_MODEL), F32))
    x1 = lax.dynamic_update_slice(x1, x1_tail, (N_PROMPT, 0))
    x1_rows = lax.dynamic_update_slice(x1_rows, x1_tail.reshape(-1, LANES), (N_PROMPT * TOKEN_ROWS, 0))
    top_e = lax.dynamic_update_slice(top_e, tail(top_e_s, pad_e), (N_PROMPT, 0))
    gate = lax.dynamic_update_slice(gate, tail(gate_s, jnp.zeros((pad, HEAD_DIM), F32)), (N_PROMPT, 0))

    dest, slot_tok, blk_expert, n_used = _moe_layout(top_e)
    hdn = _expert_gate_up(blk_expert, n_used, slot_tok, x1_rows, layer(w_gate_up),
                          b_gate_up.reshape(N_EXPERTS, 1, 2 * D_FF))
    y_slots = _expert_down(blk_expert, n_used, hdn, layer(w_down), b_down.reshape(N_EXPERTS, 1, D_MODEL))
    y_p, y_tail = _combine(dest, y_slots, gate, x1, row(ln2_g), row(ln2_b))

    y_prompt = y_p.reshape(BATCH, SEQ, D_MODEL)
    y_sample = y_tail[:DEC_BATCH].reshape(DEC_BATCH, 1, D_MODEL)
    p_kv = _prompt_kv_caches(k_rot, qkv)
    s_kv = []
    for g in range(len(WINDOWS)):
        v_cols = slice(2 * ATT_WIDTH + g * ATT_OUT, 2 * ATT_WIDTH + (g + 1) * ATT_OUT)
        ks = k_rot_s[:, g * ATT_OUT:(g + 1) * ATT_OUT].reshape(DEC_BATCH, 1, HEADS, HEAD_DIM)
        vs = h_s_in[:, v_cols].reshape(DEC_BATCH, 1, HEADS, HEAD_DIM)
        s_kv.append(jnp.stack([ks, vs], axis=2)[None])
    p_mem_kv = mem_kv_p.reshape(1, BATCH, MEM_TOKENS, 2, MEM_HEADS, MEM_HEAD_DIM)
    p_conv = ug.reshape(BATCH, SEQ, UG_WIDTH)[:, SEQ - (CONV_W - 1):, :LRU_WIDTH][None]
    p_lru_h = h_last_p.reshape(1, BATCH, LRU_WIDTH)
    u_s = h_s_in[:, qkv_cols:qkv_cols + LRU_WIDTH]
    s_conv = jnp.concatenate([layer(state_conv)[:, 1:], u_s[:, None, :]], axis=1)[None]
    s_lru_h = h_s[None]
    return (y_prompt, y_sample, p_kv[0], p_kv[1], p_kv[2], p_mem_kv, p_conv, p_lru_h,
            s_kv[0], s_kv[1], s_kv[2], s_conv, s_lru_h)
```
